```python
import jax, jax.numpy as jnp
from jax import lax
import numpy as np

D_MODEL = 1024
BATCH = 16
SEQ = 2048
DEPTH = 2

CHUNK = 64
N_EVEN = (DEPTH + 1) // 2
N_ODD = DEPTH // 2
D_A = 512
CONV_A = 31
D_B = 512
CONV_B = 3
D_C = 512
HEADS_C = 8
SGU_BLOCK = 128
D_D = 512
HEADS_D = 8
HEAD_DIM_D = D_D // HEADS_D
LEFT_CHUNKS = 8
BAND_LEN = (LEFT_CHUNKS + 1) * CHUNK
MAX_REL = 256
W_IN_AB = 2 * D_A + 3 * D_B
W_IN_CD = 2 * D_C + 3 * D_D
N_GROUPS = 4
EXPERTS_PER_GROUP = 8
N_EXPERTS = N_GROUPS * EXPERTS_PER_GROUP
TOP_K = 2
D_EXPERT = 512
EXPERT_BLOCK = 256
ALPHA = (2 * DEPTH) ** 0.25
BETA = (8 * DEPTH) ** -0.25
LN_EPS = 1e-5
NEG_INF = -1e30

kernel_name = "chunk_causal_hybrid_conv_sgu_attn_hmoe"


def layer_norm(x, g, b):
    xf = x.astype(jnp.float32)
    mu = jnp.mean(xf, axis=-1, keepdims=True)
    var = jnp.mean(jnp.square(xf - mu), axis=-1, keepdims=True)
    return ((xf - mu) * lax.rsqrt(var + LN_EPS)).astype(x.dtype) * g + b


def causal_dwconv(x, w):
    k, c = w.shape
    return lax.conv_general_dilated(
        x, w[:, None, :].astype(x.dtype), window_strides=(1,), padding=[(k - 1, 0)],
        dimension_numbers=("NWC", "WIO", "NWC"), feature_group_count=c)


def conv_shortconv_mixer(x, w_in, b_in, a_dw, a_dw_b, a_ln_g, a_ln_b, b_dw, w_out):
    h = x @ w_in + b_in
    a_val, a_gate, g_b, g_c, b_h = jnp.split(
        h, [D_A, 2 * D_A, 2 * D_A + D_B, 2 * D_A + 2 * D_B], axis=-1)
    a = a_val * jax.nn.sigmoid(a_gate)
    a = causal_dwconv(a, a_dw) + a_dw_b
    a = jax.nn.silu(layer_norm(a, a_ln_g, a_ln_b))
    s = g_b * causal_dwconv(g_c * b_h, b_dw)
    return jnp.concatenate([a, s], axis=-1) @ w_out


def sgu_chunk_attention_mixer(x, w_in, b_in, c_ln_g, c_ln_b, c_ws, c_ws_b, d_rel_bias, w_out):
    bt, s_len, _ = x.shape
    h = x @ w_in + b_in
    u, v, q, k, val = jnp.split(
        h, [D_C, 2 * D_C, 2 * D_C + D_D, 2 * D_C + 2 * D_D], axis=-1)

    v = layer_norm(v, c_ln_g, c_ln_b)
    nb = s_len // SGU_BLOCK
    v = v.reshape(bt, nb, SGU_BLOCK, HEADS_C, D_C // HEADS_C)
    pos = jnp.arange(SGU_BLOCK)
    sgu_mask = (pos[None, :] // CHUNK) <= (pos[:, None] // CHUNK)
    ws = jnp.where(sgu_mask[None], c_ws, jnp.zeros((), c_ws.dtype))
    gate = jnp.einsum("hij,bnjhc->bnihc", ws, v) + c_ws_b.T[None, None, :, :, None]
    c_out = u * gate.reshape(bt, s_len, D_C)

    n_chunks = s_len // CHUNK
    q = (q * (HEAD_DIM_D ** -0.5)).reshape(bt, n_chunks, CHUNK, HEADS_D, HEAD_DIM_D)
    q = jnp.transpose(q, (1, 0, 2, 3, 4))
    pad = ((0, 0), (LEFT_CHUNKS * CHUNK, 0), (0, 0), (0, 0))
    kp = jnp.pad(k.reshape(bt, s_len, HEADS_D, HEAD_DIM_D), pad)
    vp = jnp.pad(val.reshape(bt, s_len, HEADS_D, HEAD_DIM_D), pad)
    qi = jnp.arange(CHUNK)[:, None]
    kj = jnp.arange(BAND_LEN)[None, :]
    rel_idx = jnp.clip(qi + LEFT_CHUNKS * CHUNK - kj, -MAX_REL, MAX_REL) + MAX_REL
    rel = d_rel_bias[:, rel_idx].astype(jnp.float32)

    def one_chunk(args):
        n, qn = args
        kb = lax.dynamic_slice_in_dim(kp, n * CHUNK, BAND_LEN, axis=1)
        vb = lax.dynamic_slice_in_dim(vp, n * CHUNK, BAND_LEN, axis=1)
        sc = jnp.einsum("bqhd,bkhd->bhqk", qn, kb, preferred_element_type=jnp.float32) + rel
        key_ok = (n * CHUNK + jnp.arange(BAND_LEN)) >= LEFT_CHUNKS * CHUNK
        sc = jnp.where(key_ok[None, None, None, :], sc, NEG_INF)
        p = jax.nn.softmax(sc, axis=-1).astype(vb.dtype)
        return jnp.einsum("bhqk,bkhd->bqhd", p, vb)

    o = lax.map(one_chunk, (jnp.arange(n_chunks), q))
    o = jnp.transpose(o, (1, 0, 2, 3, 4)).reshape(bt, s_len, D_D)
    return jnp.concatenate([c_out, o], axis=-1) @ w_out


def grouped_expert_ffn(xt, expert_idx, w_gate, w_up, w_down):
    n_tok, k_sel = expert_idx.shape
    m = n_tok * k_sel
    d = xt.shape[-1]
    n_exp = w_gate.shape[0]
    flat_e = expert_idx.reshape(-1)
    order = jnp.argsort(flat_e)
    sorted_e = flat_e[order]
    counts = jnp.bincount(flat_e, length=n_exp)
    padded = ((counts + EXPERT_BLOCK - 1) // EXPERT_BLOCK) * EXPERT_BLOCK
    starts = jnp.cumsum(counts) - counts
    pad_ends = jnp.cumsum(padded)
    pad_starts = pad_ends - padded
    dest = pad_starts[sorted_e] + (jnp.arange(m) - starts[sorted_e])
    n_blocks = -(-m // EXPERT_BLOCK) + n_exp
    xp = jnp.zeros((n_blocks * EXPERT_BLOCK, d), xt.dtype).at[dest].set(xt[order // k_sel])
    block_e = jnp.searchsorted(pad_ends, jnp.arange(n_blocks) * EXPERT_BLOCK, side="right")
    block_e = jnp.minimum(block_e, n_exp - 1)

    def one_block(args):
        xb, e = args
        hb = jax.nn.silu(xb @ w_gate[e]) * (xb @ w_up[e])
        return hb @ w_down[e]

    yp = lax.map(one_block, (xp.reshape(n_blocks, EXPERT_BLOCK, d), block_e))
    y_sorted = yp.reshape(n_blocks * EXPERT_BLOCK, d)[dest]
    y = jnp.zeros((m, d), y_sorted.dtype).at[order].set(y_sorted)
    return y.reshape(n_tok, k_sel, d)


def hierarchical_moe(x, rg_w, rg_b, re_w, re_b, w_gate, w_up, w_down):
    bt, s_len, d = x.shape
    xt = x.reshape(-1, d)
    g_prob = jax.nn.softmax((xt @ rg_w + rg_b).astype(jnp.float32), axis=-1)
    g_top, g_idx = lax.top_k(g_prob, 1)
    e_all = (jnp.einsum("nd,gde->nge", xt, re_w) + re_b).astype(jnp.float32)
    e_logits = jnp.take_along_axis(e_all, g_idx[:, :, None], axis=1)[:, 0]
    e_top, e_idx = lax.top_k(jax.nn.softmax(e_logits, axis=-1), TOP_K)
    e_w = e_top / jnp.sum(e_top, axis=-1, keepdims=True)
    weights = (g_top * e_w).astype(x.dtype)
    expert_idx = g_idx * EXPERTS_PER_GROUP + e_idx
    y = grouped_expert_ffn(xt, expert_idx, w_gate, w_up, w_down)
    return jnp.einsum("nk,nkd->nd", weights, y).reshape(bt, s_len, d)


def setup_inputs(seed: int = 0) -> dict:
    key = jax.random.key(seed)
    ks = iter(jax.random.split(key, 32))

    def nrm(shape, scale):
        return jax.random.normal(next(ks), shape, jnp.float32) * scale

    return {
        "x": nrm((BATCH, SEQ, D_MODEL), 1.0),
        "ab_w_in": nrm((N_EVEN, D_MODEL, W_IN_AB), D_MODEL ** -0.5),
        "ab_b_in": nrm((N_EVEN, W_IN_AB), 0.02),
        "a_dw": nrm((N_EVEN, CONV_A, D_A), CONV_A ** -0.5),
        "a_dw_b": nrm((N_EVEN, D_A), 0.02),
        "a_ln_g": 1.0 + nrm((N_EVEN, D_A), 0.02),
        "a_ln_b": nrm((N_EVEN, D_A), 0.02),
        "b_dw": nrm((N_EVEN, CONV_B, D_B), CONV_B ** -0.5),
        "ab_w_out": nrm((N_EVEN, D_A + D_B, D_MODEL), BETA * (D_A + D_B) ** -0.5),
        "cd_w_in": nrm((N_ODD, D_MODEL, W_IN_CD), D_MODEL ** -0.5),
        "cd_b_in": nrm((N_ODD, W_IN_CD), 0.02),
        "c_ln_g": 1.0 + nrm((N_ODD, D_C), 0.02),
        "c_ln_b": nrm((N_ODD, D_C), 0.02),
        "c_ws": nrm((N_ODD, HEADS_C, SGU_BLOCK, SGU_BLOCK), SGU_BLOCK ** -0.5),
        "c_ws_b": 1.0 + nrm((N_ODD, HEADS_C, SGU_BLOCK), 0.02),
        "d_rel_bias": nrm((N_ODD, HEADS_D, 2 * MAX_REL + 1), 0.1),
        "cd_w_out": nrm((N_ODD, D_C + D_D, D_MODEL), BETA * (D_C + D_D) ** -0.5),
        "mix_ln_g": 1.0 + nrm((DEPTH, D_MODEL), 0.02),
        "mix_ln_b": nrm((DEPTH, D_MODEL), 0.02),
        "moe_rg_w": nrm((DEPTH, D_MODEL, N_GROUPS), D_MODEL ** -0.5),
        "moe_rg_b": nrm((DEPTH, N_GROUPS), 0.01),
        "moe_re_w": nrm((DEPTH, N_GROUPS, D_MODEL, EXPERTS_PER_GROUP), D_MODEL ** -0.5),
        "moe_re_b": nrm((DEPTH, N_GROUPS, EXPERTS_PER_GROUP), 0.01),
        "moe_w_gate": nrm((DEPTH, N_EXPERTS, D_MODEL, D_EXPERT), D_MODEL ** -0.5),
        "moe_w_up": nrm((DEPTH, N_EXPERTS, D_MODEL, D_EXPERT), D_MODEL ** -0.5),
        "moe_w_down": nrm((DEPTH, N_EXPERTS, D_EXPERT, D_MODEL), BETA * D_EXPERT ** -0.5),
        "ffn_ln_g": 1.0 + nrm((DEPTH, D_MODEL), 0.02),
        "ffn_ln_b": nrm((DEPTH, D_MODEL), 0.02),
    }


def reference(x, ab_w_in, ab_b_in, a_dw, a_dw_b, a_ln_g, a_ln_b, b_dw, ab_w_out,
              cd_w_in, cd_b_in, c_ln_g, c_ln_b, c_ws, c_ws_b, d_rel_bias, cd_w_out,
              mix_ln_g, mix_ln_b, moe_rg_w, moe_rg_b, moe_re_w, moe_re_b,
              moe_w_gate, moe_w_up, moe_w_down, ffn_ln_g, ffn_ln_b):
    for layer in range(DEPTH):
        i = layer // 2
        if layer % 2 == 0:
            mix = conv_shortconv_mixer(x, ab_w_in[i], ab_b_in[i], a_dw[i], a_dw_b[i],
                                       a_ln_g[i], a_ln_b[i], b_dw[i], ab_w_out[i])
        else:
            mix = sgu_chunk_attention_mixer(x, cd_w_in[i], cd_b_in[i], c_ln_g[i], c_ln_b[i],
                                            c_ws[i], c_ws_b[i], d_rel_bias[i], cd_w_out[i])
        x = layer_norm(ALPHA * x + mix, mix_ln_g[layer], mix_ln_b[layer])
        ffn = hierarchical_moe(x, moe_rg_w[layer], moe_rg_b[layer], moe_re_w[layer],
                               moe_re_b[layer], moe_w_gate[layer], moe_w_up[layer],
                               moe_w_down[layer])
        x = layer_norm(ALPHA * x + ffn, ffn_ln_g[layer], ffn_ln_b[layer])
    return x
```

```python
import functools

import jax
import jax.numpy as jnp
from jax import lax
from jax.experimental import pallas as pl
from jax.experimental.pallas import tpu as pltpu

F32 = jnp.float32
BF16 = jnp.bfloat16
I32 = jnp.int32

DEPTH = 2
CHUNK = 64
LEFT_CHUNKS = 8
HIST = LEFT_CHUNKS * CHUNK
CONV_A = 31
CONV_B = 3
HEADS_C = 8
SGU_BLOCK = 128
HEADS_D = 8
MAX_REL = 256
N_GROUPS = 4
EXPERTS_PER_GROUP = 8
N_EXPERTS = N_GROUPS * EXPERTS_PER_GROUP
TOP_K = 2
ALPHA = (2 * DEPTH) ** 0.25
LN_EPS = 1e-5
NEG_INF = -1e30

LANES = 128
SUBLANES = 8
VMEM_LIMIT = 56 * 1024 * 1024

SEQ_TILE = 512
CONV_ROWS = 32
HALO = 32
Q_ROWS = 256
K_ROWS = Q_ROWS + HIST
ROUTER_TILE = 512
EXPERT_ROWS = 256
DISPATCH_TILE = 2048
COMBINE_TILE = 256
ROUTER_LANE0 = N_GROUPS
_B_SHIFTS = sorted({(HALO - (CONV_B - 1) + j) % SUBLANES for j in range(CONV_B)} - {0})


def _layer_norm(x, g, b):
    mu = jnp.mean(x, axis=-1, keepdims=True)
    xc = x - mu
    var = jnp.mean(xc * xc, axis=-1, keepdims=True)
    return xc * lax.rsqrt(var + LN_EPS) * g + b


def _const_spec(shape):
    nd = len(shape)
    return pl.BlockSpec(shape, lambda *_: (0,) * nd, pipeline_mode=pl.Buffered(1))


def _ab_mixer_kernel(x_ref, win_ref, bin_ref, adw_ref, adwb_ref, alng_ref, alnb_ref,
                     bdw_ref, wout_ref, lng_ref, lnb_ref, o_ref,
                     abuf, cbuf, gbuf, mbuf, asht, csht, *, d_a, d_b):
    ts = x_ref.shape[0]

    @pl.when(pl.program_id(1) == 0)
    def _():
        abuf[0:HALO, :] = jnp.zeros((HALO, d_a), F32)
        cbuf[0:HALO, :] = jnp.zeros((HALO, d_b), F32)

    x = x_ref[...]
    h = jnp.dot(x.astype(BF16), win_ref[...], preferred_element_type=F32) + bin_ref[...]
    abuf[HALO:HALO + ts, :] = h[:, 0:d_a] * jax.nn.sigmoid(h[:, d_a:2 * d_a])
    gbuf[...] = h[:, 2 * d_a:2 * d_a + d_b]
    cbuf[HALO:HALO + ts, :] = h[:, 2 * d_a + d_b:2 * d_a + 2 * d_b] * h[:, 2 * d_a + 2 * d_b:]

    n_sh = HALO + ts - SUBLANES
    for sh in range(1, SUBLANES):
        asht[sh, 0:n_sh, :] = abuf[sh:sh + n_sh, :]
    for sh in _B_SHIFTS:
        csht[_B_SHIFTS.index(sh), 0:n_sh, :] = cbuf[sh:sh + n_sh, :]

    def tap(buf, sht, slot_of, r0, off):
        base = r0 + (off // SUBLANES) * SUBLANES
        if off % SUBLANES == 0:
            return buf[pl.ds(base, CONV_ROWS), :]
        return sht[slot_of(off % SUBLANES), pl.ds(base, CONV_ROWS), :]

    def conv_chunk(i, carry):
        r0 = pl.multiple_of(i * CONV_ROWS, CONV_ROWS)
        acc = jnp.broadcast_to(adwb_ref[...], (CONV_ROWS, d_a))
        for j in range(CONV_A):
            acc = acc + adw_ref[j:j + 1, :] * tap(abuf, asht, lambda sh: sh, r0, HALO - (CONV_A - 1) + j)
        a = _layer_norm(acc, alng_ref[...], alnb_ref[...])
        mbuf[pl.ds(r0, CONV_ROWS), 0:d_a] = (a * jax.nn.sigmoid(a)).astype(BF16)
        s = jnp.zeros((CONV_ROWS, d_b), F32)
        for j in range(CONV_B):
            s = s + bdw_ref[j:j + 1, :] * tap(cbuf, csht, _B_SHIFTS.index, r0, HALO - (CONV_B - 1) + j)
        mbuf[pl.ds(r0, CONV_ROWS), d_a:d_a + d_b] = (gbuf[pl.ds(r0, CONV_ROWS), :] * s).astype(BF16)
        return carry

    lax.fori_loop(0, ts // CONV_ROWS, conv_chunk, 0)

    abuf[0:HALO, :] = abuf[ts:ts + HALO, :]
    cbuf[0:HALO, :] = cbuf[ts:ts + HALO, :]

    mix = jnp.dot(mbuf[...], wout_ref[...], preferred_element_type=F32)
    o_ref[...] = _layer_norm(ALPHA * x + mix, lng_ref[...], lnb_ref[...])


def _ab_mixer(x, w_in, b_in, a_dw, a_dw_b, a_ln_g, a_ln_b, b_dw, w_out, ln_g, ln_b):
    bt, s_len, d = x.shape
    d_a = a_dw.shape[1]
    d_b = b_dw.shape[1]
    w_in_w = w_in.shape[1]
    ts = SEQ_TILE
    row = lambda v: v.reshape(1, -1)
    kern = functools.partial(_ab_mixer_kernel, d_a=d_a, d_b=d_b)
    return pl.pallas_call(
        kern,
        out_shape=jax.ShapeDtypeStruct((bt, s_len, d), F32),
        grid=(bt, s_len // ts),
        in_specs=[
            pl.BlockSpec((None, ts, d), lambda b, t: (b, t, 0)),
            _const_spec((d, w_in_w)),
            _const_spec((1, w_in_w)),
            _const_spec((CONV_A, d_a)),
            _const_spec((1, d_a)),
            _const_spec((1, d_a)),
            _const_spec((1, d_a)),
            _const_spec((CONV_B, d_b)),
            _const_spec((d_a + d_b, d)),
            _const_spec((1, d)),
            _const_spec((1, d)),
        ],
        out_specs=pl.BlockSpec((None, ts, d), lambda b, t: (b, t, 0)),
        scratch_shapes=[
            pltpu.VMEM((HALO + ts, d_a), F32),
            pltpu.VMEM((HALO + ts, d_b), F32),
            pltpu.VMEM((ts, d_b), F32),
            pltpu.VMEM((ts, d_a + d_b), BF16),
            pltpu.VMEM((SUBLANES, HALO + ts, d_a), F32),
            pltpu.VMEM((len(_B_SHIFTS), HALO + ts, d_b), F32),
        ],
        compiler_params=pltpu.CompilerParams(
            dimension_semantics=("arbitrary", "arbitrary"), vmem_limit_bytes=VMEM_LIMIT),
        name="ab_mixer",
    )(x, w_in.astype(BF16), row(b_in), a_dw, row(a_dw_b), row(a_ln_g), row(a_ln_b),
      b_dw, w_out.astype(BF16), row(ln_g), row(ln_b))


def _cd_mixer_kernel(x_ref, win_ref, bin_ref, clng_ref, clnb_ref, wsc_ref, wsb_ref,
                     bias_ref, wout_ref, lng_ref, lnb_ref, o_ref,
                     qbuf, kbuf, vbuf, wsm, vbd, mbuf, *, d_c, d_d):
    ts = x_ref.shape[0]
    t = pl.program_id(1)
    hc = d_c // HEADS_C
    hd = d_d // HEADS_D

    @pl.when(t == 0)
    def _():
        kbuf[0:HIST, :] = jnp.zeros((HIST, d_d), BF16)
        vbuf[0:HIST, :] = jnp.zeros((HIST, d_d), BF16)
        r = lax.broadcasted_iota(I32, (SGU_BLOCK, HEADS_C * SGU_BLOCK), 0)
        c = lax.broadcasted_iota(I32, (SGU_BLOCK, HEADS_C * SGU_BLOCK), 1) % SGU_BLOCK
        wsm[...] = jnp.where(c // CHUNK <= r // CHUNK, wsc_ref[...], 0.0).astype(BF16)

    x = x_ref[...]
    h = jnp.dot(x.astype(BF16), win_ref[...], preferred_element_type=F32) + bin_ref[...]
    u = h[:, 0:d_c]
    vn = _layer_norm(h[:, d_c:2 * d_c], clng_ref[...], clnb_ref[...])
    qbuf[...] = (h[:, 2 * d_c:2 * d_c + d_d] * (hd ** -0.5)).astype(BF16)
    kbuf[HIST:HIST + ts, :] = h[:, 2 * d_c + d_d:2 * d_c + 2 * d_d].astype(BF16)
    vbuf[HIST:HIST + ts, :] = h[:, 2 * d_c + 2 * d_d:].astype(BF16)

    lane_head = lax.broadcasted_iota(I32, (SGU_BLOCK, d_c), 1) // hc
    for nb in range(ts // SGU_BLOCK):
        v_blk = vn[nb * SGU_BLOCK:(nb + 1) * SGU_BLOCK, :]
        for hh in range(HEADS_C):
            vbd[hh * SGU_BLOCK:(hh + 1) * SGU_BLOCK, :] = jnp.where(
                lane_head == hh, v_blk, 0.0).astype(BF16)
        gate = jnp.dot(wsm[...], vbd[...], preferred_element_type=F32) + wsb_ref[...]
        mbuf[nb * SGU_BLOCK:(nb + 1) * SGU_BLOCK, 0:d_c] = (
            u[nb * SGU_BLOCK:(nb + 1) * SGU_BLOCK, :] * gate).astype(BF16)

    col = lax.broadcasted_iota(I32, (Q_ROWS, K_ROWS), 1)
    for qb in range(ts // Q_ROWS):
        q0 = qb * Q_ROWS
        key_ok = jnp.logical_or(t > 0, col + q0 >= HIST)
        for hh in range(HEADS_D):
            q = qbuf[q0:q0 + Q_ROWS, hh * hd:(hh + 1) * hd]
            k = kbuf[q0:q0 + K_ROWS, hh * hd:(hh + 1) * hd]
            v = vbuf[q0:q0 + K_ROWS, hh * hd:(hh + 1) * hd]
            s = lax.dot_general(q, k, (((1,), (1,)), ((), ())), preferred_element_type=F32)
            s = jnp.where(key_ok, s + bias_ref[hh], NEG_INF)
            p = jnp.exp(s - jnp.max(s, axis=-1, keepdims=True))
            l = jnp.sum(p, axis=-1, keepdims=True)
            o = jnp.dot(p.astype(BF16), v, preferred_element_type=F32)
            mbuf[q0:q0 + Q_ROWS, d_c + hh * hd:d_c + (hh + 1) * hd] = (o / l).astype(BF16)

    kbuf[0:HIST, :] = kbuf[ts:ts + HIST, :]
    vbuf[0:HIST, :] = vbuf[ts:ts + HIST, :]

    mix = jnp.dot(mbuf[...], wout_ref[...], preferred_element_type=F32)
    o_ref[...] = _layer_norm(ALPHA * x + mix, lng_ref[...], lnb_ref[...])


def _attention_bias(d_rel_bias):
    i = jnp.arange(Q_ROWS)[:, None]
    j = jnp.arange(K_ROWS)[None, :]
    jb = j - (i // CHUNK) * CHUNK
    in_band = jnp.logical_and(jb >= 0, jb < (LEFT_CHUNKS + 1) * CHUNK)
    rel_idx = jnp.clip(i + HIST - j, -MAX_REL, MAX_REL) + MAX_REL
    rel = d_rel_bias[:, rel_idx].astype(F32)
    return jnp.where(in_band[None], rel, NEG_INF)


def _cd_mixer(x, w_in, b_in, c_ln_g, c_ln_b, c_ws, c_ws_b, d_rel_bias, w_out, ln_g, ln_b):
    bt, s_len, d = x.shape
    d_c = c_ln_g.shape[0]
    d_d = w_out.shape[0] - d_c
    w_in_w = w_in.shape[1]
    ts = SEQ_TILE
    assert ts == HIST and ts % Q_ROWS == 0 and ts % SGU_BLOCK == 0
    row = lambda v: v.reshape(1, -1)
    ws_cat = jnp.transpose(c_ws, (1, 0, 2)).reshape(SGU_BLOCK, HEADS_C * SGU_BLOCK)
    bias_full = jnp.repeat(c_ws_b.T, d_c // HEADS_C, axis=1)
    att_bias = _attention_bias(d_rel_bias)
    kern = functools.partial(_cd_mixer_kernel, d_c=d_c, d_d=d_d)
    return pl.pallas_call(
        kern,
        out_shape=jax.ShapeDtypeStruct((bt, s_len, d), F32),
        grid=(bt, s_len // ts),
        in_specs=[
            pl.BlockSpec((None, ts, d), lambda b, t: (b, t, 0)),
            _const_spec((d, w_in_w)),
            _const_spec((1, w_in_w)),
            _const_spec((1, d_c)),
            _const_spec((1, d_c)),
            _const_spec((SGU_BLOCK, HEADS_C * SGU_BLOCK)),
            _const_spec((SGU_BLOCK, d_c)),
            _const_spec((HEADS_D, Q_ROWS, K_ROWS)),
            _const_spec((d_c + d_d, d)),
            _const_spec((1, d)),
            _const_spec((1, d)),
        ],
        out_specs=pl.BlockSpec((None, ts, d), lambda b, t: (b, t, 0)),
        scratch_shapes=[
            pltpu.VMEM((ts, d_d), BF16),
            pltpu.VMEM((HIST + ts, d_d), BF16),
            pltpu.VMEM((HIST + ts, d_d), BF16),
            pltpu.VMEM((SGU_BLOCK, HEADS_C * SGU_BLOCK), BF16),
            pltpu.VMEM((HEADS_C * SGU_BLOCK, d_c), BF16),
            pltpu.VMEM((ts, d_c + d_d), BF16),
        ],
        compiler_params=pltpu.CompilerParams(
            dimension_semantics=("arbitrary", "arbitrary"), vmem_limit_bytes=VMEM_LIMIT),
        name="cd_mixer",
    )(x, w_in.astype(BF16), row(b_in), row(c_ln_g), row(c_ln_b), ws_cat, bias_full,
      att_bias, w_out.astype(BF16), row(ln_g), row(ln_b))


def _router_kernel(x_ref, wr_ref, br_ref, rt_ref, rn_ref, cnt_ref, run_ref):
    tr = x_ref.shape[0]

    @pl.when(pl.program_id(0) == 0)
    def _():
        run_ref[...] = jnp.zeros((1, LANES), F32)

    logits = jnp.dot(x_ref[...], wr_ref[...], precision=lax.Precision.HIGHEST,
                     preferred_element_type=F32) + br_ref[...]
    lane = lax.broadcasted_iota(I32, (tr, LANES), 1).astype(F32)

    def top1(vals):
        m = jnp.max(vals, axis=-1, keepdims=True)
        idx = jnp.min(jnp.where(vals == m, lane, float(LANES)), axis=-1, keepdims=True)
        return m, idx

    g_mask = lane < N_GROUPS
    g_max, g_idx = top1(jnp.where(g_mask, logits, NEG_INF))
    g_top = 1.0 / jnp.sum(jnp.where(g_mask, jnp.exp(logits - g_max), 0.0), axis=-1, keepdims=True)

    e_lo = ROUTER_LANE0 + EXPERTS_PER_GROUP * g_idx
    e_vals = jnp.where(jnp.logical_and(lane >= e_lo, lane < e_lo + EXPERTS_PER_GROUP), logits, NEG_INF)
    m1, i1 = top1(e_vals)
    m2, i2 = top1(jnp.where(lane == i1, NEG_INF, e_vals))
    ratio = jnp.exp(m2 - m1)
    w1 = g_top / (1.0 + ratio)
    w2 = g_top * ratio / (1.0 + ratio)

    chosen = jnp.logical_or(lane == i1, lane == i2)
    onehot = jnp.where(chosen, 1.0, 0.0)
    rr = lax.broadcasted_iota(I32, (tr, tr), 0)
    cc = lax.broadcasted_iota(I32, (tr, tr), 1)
    before = jnp.where(cc < rr, 1.0, 0.0).astype(BF16)
    seen = run_ref[...] + jnp.dot(before, onehot.astype(BF16), preferred_element_type=F32)
    r1 = jnp.sum(jnp.where(lane == i1, seen, 0.0), axis=-1, keepdims=True)
    r2 = jnp.sum(jnp.where(lane == i2, seen, 0.0), axis=-1, keepdims=True)
    run_ref[...] = run_ref[...] + jnp.sum(onehot, axis=0, keepdims=True)
    cnt_ref[...] = jnp.broadcast_to(run_ref[...], (SUBLANES, LANES))

    fields = (i1 - ROUTER_LANE0, i2 - ROUTER_LANE0, w1, w2, r1, r2)
    res = jnp.zeros((tr, LANES), F32)
    for f, val in enumerate(fields):
        res = jnp.where(lane == f, val, res)
    rn_ref[...] = res
    rt_ref[...] = res.T[0:SUBLANES, :]


def _router(xt, rg_w, rg_b, re_w, re_b):
    n_tok, d = xt.shape
    tr = ROUTER_TILE
    w_all = jnp.concatenate([rg_w, jnp.transpose(re_w, (1, 0, 2)).reshape(d, N_EXPERTS)], axis=1)
    b_all = jnp.concatenate([rg_b, re_b.reshape(N_EXPERTS)])
    pad = LANES - w_all.shape[1]
    w_all = jnp.pad(w_all, ((0, 0), (0, pad)))
    b_all = jnp.pad(b_all, (0, pad)).reshape(1, LANES)
    return pl.pallas_call(
        _router_kernel,
        out_shape=(
            jax.ShapeDtypeStruct((SUBLANES, n_tok), F32),
            jax.ShapeDtypeStruct((n_tok, LANES), F32),
            jax.ShapeDtypeStruct((SUBLANES, LANES), F32),
        ),
        grid=(n_tok // tr,),
        in_specs=[
            pl.BlockSpec((tr, d), lambda i: (i, 0)),
            _const_spec((d, LANES)),
            _const_spec((1, LANES)),
        ],
        out_specs=(
            pl.BlockSpec((SUBLANES, tr), lambda i: (0, i)),
            pl.BlockSpec((tr, LANES), lambda i: (i, 0)),
            pl.BlockSpec((SUBLANES, LANES), lambda i: (0, 0)),
        ),
        scratch_shapes=[pltpu.VMEM((1, LANES), F32)],
        compiler_params=pltpu.CompilerParams(
            dimension_semantics=("arbitrary",), vmem_limit_bytes=VMEM_LIMIT),
        name="router",
    )(xt, w_all, b_all)


def _routing_tables(rt, cnt, n_tok):
    m = n_tok * TOP_K
    n_blocks = -(-m // EXPERT_ROWS) + N_EXPERTS
    e_idx = rt[0:TOP_K].astype(I32)
    rank = rt[4:4 + TOP_K].astype(I32)
    counts = cnt[0, ROUTER_LANE0:ROUTER_LANE0 + N_EXPERTS].astype(I32)
    padded = ((counts + EXPERT_ROWS - 1) // EXPERT_ROWS) * EXPERT_ROWS
    pad_ends = jnp.cumsum(padded)
    pad_starts = pad_ends - padded
    dest = (pad_starts[e_idx] + rank).T.reshape(m)
    block_e = jnp.searchsorted(pad_ends, jnp.arange(n_blocks, dtype=I32) * EXPERT_ROWS, side="right")
    block_e = jnp.minimum(block_e, N_EXPERTS - 1).astype(I32)
    n_used = (pad_ends[-1] // EXPERT_ROWS).astype(I32).reshape(1)
    p = jnp.arange(N_EXPERTS * EXPERT_ROWS, dtype=I32)
    pe, pj = p // EXPERT_ROWS, p % EXPERT_ROWS
    pad_dest = jnp.where(pj < (padded - counts)[pe], pad_starts[pe] + counts[pe] + pj, -1).astype(I32)
    return dest, block_e, n_used, pad_dest, n_blocks


def _dispatch_kernel(dest_ref, pad_ref, x_hbm, xp_hbm, zrow, sem, zsem):
    i = pl.program_id(0)
    td = dest_ref.shape[-1] // TOP_K
    tok0 = i * td

    def row_copy(tok, k):
        return pltpu.make_async_copy(
            x_hbm.at[pl.ds(tok, 1), :],
            xp_hbm.at[pl.ds(dest_ref[0, 0, (tok - tok0) * TOP_K + k], 1), :], sem)

    def start(j, c):
        for k in range(TOP_K):
            row_copy(tok0 + j, k).start()
        return c

    def wait(j, c):
        for k in range(TOP_K):
            row_copy(tok0 + j, k).wait()
        return c

    lax.fori_loop(0, td, start, 0)

    @pl.when(i == 0)
    def _():
        zrow[...] = jnp.zeros(zrow.shape, F32)

        def pad_copy(p):
            return pltpu.make_async_copy(zrow.at[pl.ds(0, 1), :], xp_hbm.at[pl.ds(pad_ref[p], 1), :], zsem)

        def pstart(p, c):
            @pl.when(pad_ref[p] >= 0)
            def _():
                pad_copy(p).start()
            return c

        def pwait(p, c):
            @pl.when(pad_ref[p] >= 0)
            def _():
                pad_copy(p).wait()
            return c

        lax.fori_loop(0, pad_ref.shape[0], pstart, 0)
        lax.fori_loop(0, pad_ref.shape[0], pwait, 0)

    lax.fori_loop(0, td, wait, 0)


def _dispatch(xt, dest, pad_dest, n_blocks):
    n_tok, d = xt.shape
    td = DISPATCH_TILE
    n_steps = n_tok // td
    return pl.pallas_call(
        _dispatch_kernel,
        out_shape=jax.ShapeDtypeStruct((n_blocks * EXPERT_ROWS, d), F32),
        grid=(n_steps,),
        in_specs=[
            pl.BlockSpec((1, 1, td * TOP_K), lambda i: (i, 0, 0), memory_space=pltpu.SMEM),
            pl.BlockSpec(memory_space=pltpu.SMEM),
            pl.BlockSpec(memory_space=pl.ANY),
        ],
        out_specs=pl.BlockSpec(memory_space=pl.ANY),
        scratch_shapes=[
            pltpu.VMEM((SUBLANES, d), F32),
            pltpu.SemaphoreType.DMA,
            pltpu.SemaphoreType.DMA,
        ],
        compiler_params=pltpu.CompilerParams(dimension_semantics=("arbitrary",)),
        name="dispatch",
    )(dest.reshape(n_steps, 1, td * TOP_K), pad_dest, xt)


def _ffn_kernel(be_ref, nu_ref, x_ref, wg_ref, wu_ref, wd_ref, o_ref):
    @pl.when(pl.program_id(0) < nu_ref[0])
    def _():
        xb = x_ref[...].astype(BF16)
        g = jnp.dot(xb, wg_ref[...], preferred_element_type=F32)
        u = jnp.dot(xb, wu_ref[...], preferred_element_type=F32)
        hb = (g * jax.nn.sigmoid(g) * u).astype(BF16)
        o_ref[...] = jnp.dot(hb, wd_ref[...], preferred_element_type=F32)


def _expert_ffn(xp, block_e, n_used, w_gate, w_up, w_down):
    n_rows, d = xp.shape
    d_e = w_gate.shape[2]
    n_blocks = n_rows // EXPERT_ROWS
    blk = lambda i, be, nu: (jnp.minimum(i, nu[0] - 1), 0)
    wsel = lambda i, be, nu: (be[jnp.minimum(i, nu[0] - 1)], 0, 0)
    return pl.pallas_call(
        _ffn_kernel,
        out_shape=jax.ShapeDtypeStruct((n_rows, d), F32),
        grid_spec=pltpu.PrefetchScalarGridSpec(
            num_scalar_prefetch=2,
            grid=(n_blocks,),
            in_specs=[
                pl.BlockSpec((EXPERT_ROWS, d), blk),
                pl.BlockSpec((None, d, d_e), wsel),
                pl.BlockSpec((None, d, d_e), wsel),
                pl.BlockSpec((None, d_e, d), wsel),
            ],
            out_specs=pl.BlockSpec((EXPERT_ROWS, d), blk),
        ),
        compiler_params=pltpu.CompilerParams(
            dimension_semantics=("arbitrary",), vmem_limit_bytes=VMEM_LIMIT),
        name="expert_ffn",
    )(block_e, n_used, xp, w_gate.astype(BF16), w_up.astype(BF16), w_down.astype(BF16))


def _combine_kernel(dest_ref, x_ref, rn_ref, yp_hbm, lng_ref, lnb_ref, o_ref, ybuf, sem):
    tc = x_ref.shape[0]

    def row_copy(j, k):
        return pltpu.make_async_copy(
            yp_hbm.at[pl.ds(dest_ref[0, 0, j * TOP_K + k], 1), :],
            ybuf.at[k, pl.ds(j, 1), :], sem)

    def start(j, c):
        for k in range(TOP_K):
            row_copy(j, k).start()
        return c

    def wait(j, c):
        for k in range(TOP_K):
            row_copy(j, k).wait()
        return c

    lax.fori_loop(0, tc, start, 0)
    lax.fori_loop(0, tc, wait, 0)

    w = rn_ref[...]
    ffn = w[:, 2:3] * ybuf[0] + w[:, 3:4] * ybuf[1]
    o_ref[...] = _layer_norm(ALPHA * x_ref[...] + ffn, lng_ref[...], lnb_ref[...])


def _combine(xt, rn, dest, yp, ln_g, ln_b):
    n_tok, d = xt.shape
    tc = COMBINE_TILE
    n_steps = n_tok // tc
    return pl.pallas_call(
        _combine_kernel,
        out_shape=jax.ShapeDtypeStruct((n_tok, d), F32),
        grid=(n_steps,),
        in_specs=[
            pl.BlockSpec((1, 1, tc * TOP_K), lambda i: (i, 0, 0), memory_space=pltpu.SMEM),
            pl.BlockSpec((tc, d), lambda i: (i, 0)),
            pl.BlockSpec((tc, LANES), lambda i: (i, 0)),
            pl.BlockSpec(memory_space=pl.ANY),
            _const_spec((1, d)),
            _const_spec((1, d)),
        ],
        out_specs=pl.BlockSpec((tc, d), lambda i: (i, 0)),
        scratch_shapes=[
            pltpu.VMEM((TOP_K, tc, d), F32),
            pltpu.SemaphoreType.DMA,
        ],
        compiler_params=pltpu.CompilerParams(
            dimension_semantics=("arbitrary",), vmem_limit_bytes=VMEM_LIMIT),
        name="combine",
    )(dest.reshape(n_steps, 1, tc * TOP_K), xt, rn, yp, ln_g.reshape(1, d), ln_b.reshape(1, d))


def _moe_layer(x, rg_w, rg_b, re_w, re_b, w_gate, w_up, w_down, ln_g, ln_b):
    bt, s_len, d = x.shape
    xt = x.reshape(bt * s_len, d)
    rt, rn, cnt = _router(xt, rg_w, rg_b, re_w, re_b)
    dest, block_e, n_used, pad_dest, n_blocks = _routing_tables(rt, cnt, bt * s_len)
    xp = _dispatch(xt, dest, pad_dest, n_blocks)
    yp = _expert_ffn(xp, block_e, n_used, w_gate, w_up, w_down)
    out = _combine(xt, rn, dest, yp, ln_g, ln_b)
    return out.reshape(bt, s_len, d)


def kernel(x, ab_w_in, ab_b_in, a_dw, a_dw_b, a_ln_g, a_ln_b, b_dw, ab_w_out, cd_w_in, cd_b_in, c_ln_g, c_ln_b, c_ws, c_ws_b, d_rel_bias, cd_w_out, mix_ln_g, mix_ln_b, moe_rg_w, moe_rg_b, moe_re_w, moe_re_b, moe_w_gate, moe_w_up, moe_w_down, ffn_ln_g, ffn_ln_b):
    for layer in range(DEPTH):
        i = layer // 2
        if layer % 2 == 0:
            x = _ab_mixer(x, ab_w_in[i], ab_b_in[i], a_dw[i], a_dw_b[i], a_ln_g[i], a_ln_b[i],
                          b_dw[i], ab_w_out[i], mix_ln_g[layer], mix_ln_b[layer])
        else:
            x = _cd_mixer(x, cd_w_in[i], cd_b_in[i], c_ln_g[i], c_ln_b[i], c_ws[i], c_ws_b[i],
                          d_rel_bias[i], cd_w_out[i], mix_ln_g[layer], mix_ln_b[layer])
        x = _moe_layer(x, moe_rg_w[layer], moe_rg_b[layer], moe_re_w[layer], moe_re_b[layer],
                       moe_w_gate[layer], moe_w_up[layer], moe_w_down[layer],
                       ffn_ln_g[layer], ffn_ln_b[layer])
    return x
```

```python
import functools

import jax
import jax.numpy as jnp
from jax import lax
from jax.experimental import pallas as pl
from jax.experimental.pallas import tpu as pltpu

F32 = jnp.float32
BF16 = jnp.bfloat16
I32 = jnp.int32

DEPTH = 2
CHUNK = 64
LEFT_CHUNKS = 8
HIST = LEFT_CHUNKS * CHUNK
CONV_A = 31
CONV_B = 3
HEADS_C = 8
SGU_BLOCK = 128
HEADS_D = 8
MAX_REL = 256
N_GROUPS = 4
EXPERTS_PER_GROUP = 8
N_EXPERTS = N_GROUPS * EXPERTS_PER_GROUP
TOP_K = 2
ALPHA = (2 * DEPTH) ** 0.25
LN_EPS = 1e-5
NEG_INF = -1e30

LANES = 128
SUBLANES = 8
VMEM_LIMIT = 56 * 1024 * 1024

SEQ_TILE = 512
CONV_ROWS = 32
HALO = 32
Q_ROWS = 256
K_ROWS = Q_ROWS + HIST
MOE_TILE = 512
ROUTER_TILE = MOE_TILE
EXPERT_ROWS = 256
SORT_ROWS = MOE_TILE * TOP_K + N_EXPERTS * SUBLANES
SORT_CHUNK = 256
ROUTER_LANE0 = N_GROUPS
_B_SHIFTS = sorted({(HALO - (CONV_B - 1) + j) % SUBLANES for j in range(CONV_B)} - {0})


def _layer_norm(x, g, b):
    mu = jnp.mean(x, axis=-1, keepdims=True)
    xc = x - mu
    var = jnp.mean(xc * xc, axis=-1, keepdims=True)
    return xc * lax.rsqrt(var + LN_EPS) * g + b


def _const_spec(shape):
    nd = len(shape)
    return pl.BlockSpec(shape, lambda *_: (0,) * nd, pipeline_mode=pl.Buffered(1))


def _ab_mixer_kernel(x_ref, win_ref, bin_ref, adw_ref, adwb_ref, alng_ref, alnb_ref,
                     bdw_ref, wout_ref, lng_ref, lnb_ref, o_ref,
                     abuf, cbuf, gbuf, mbuf, asht, csht, *, d_a, d_b):
    ts = x_ref.shape[0]

    @pl.when(pl.program_id(1) == 0)
    def _():
        abuf[0:HALO, :] = jnp.zeros((HALO, d_a), F32)
        cbuf[0:HALO, :] = jnp.zeros((HALO, d_b), F32)

    x = x_ref[...]
    h = jnp.dot(x.astype(BF16), win_ref[...], preferred_element_type=F32) + bin_ref[...]
    abuf[HALO:HALO + ts, :] = h[:, 0:d_a] * jax.nn.sigmoid(h[:, d_a:2 * d_a])
    gbuf[...] = h[:, 2 * d_a:2 * d_a + d_b]
    cbuf[HALO:HALO + ts, :] = h[:, 2 * d_a + d_b:2 * d_a + 2 * d_b] * h[:, 2 * d_a + 2 * d_b:]

    n_sh = HALO + ts - SUBLANES
    for sh in range(1, SUBLANES):
        asht[sh, 0:n_sh, :] = abuf[sh:sh + n_sh, :]
    for sh in _B_SHIFTS:
        csht[_B_SHIFTS.index(sh), 0:n_sh, :] = cbuf[sh:sh + n_sh, :]

    def tap(buf, sht, slot_of, r0, off):
        base = r0 + (off // SUBLANES) * SUBLANES
        if off % SUBLANES == 0:
            return buf[pl.ds(base, CONV_ROWS), :]
        return sht[slot_of(off % SUBLANES), pl.ds(base, CONV_ROWS), :]

    def conv_chunk(i, carry):
        r0 = pl.multiple_of(i * CONV_ROWS, CONV_ROWS)
        acc = jnp.broadcast_to(adwb_ref[...], (CONV_ROWS, d_a))
        for j in range(CONV_A):
            acc = acc + adw_ref[j:j + 1, :] * tap(abuf, asht, lambda sh: sh, r0, HALO - (CONV_A - 1) + j)
        a = _layer_norm(acc, alng_ref[...], alnb_ref[...])
        mbuf[pl.ds(r0, CONV_ROWS), 0:d_a] = (a * jax.nn.sigmoid(a)).astype(BF16)
        s = jnp.zeros((CONV_ROWS, d_b), F32)
        for j in range(CONV_B):
            s = s + bdw_ref[j:j + 1, :] * tap(cbuf, csht, _B_SHIFTS.index, r0, HALO - (CONV_B - 1) + j)
        mbuf[pl.ds(r0, CONV_ROWS), d_a:d_a + d_b] = (gbuf[pl.ds(r0, CONV_ROWS), :] * s).astype(BF16)
        return carry

    lax.fori_loop(0, ts // CONV_ROWS, conv_chunk, 0)

    abuf[0:HALO, :] = abuf[ts:ts + HALO, :]
    cbuf[0:HALO, :] = cbuf[ts:ts + HALO, :]

    mix = jnp.dot(mbuf[...], wout_ref[...], preferred_element_type=F32)
    o_ref[...] = _layer_norm(ALPHA * x + mix, lng_ref[...], lnb_ref[...])


def _ab_mixer(x, w_in, b_in, a_dw, a_dw_b, a_ln_g, a_ln_b, b_dw, w_out, ln_g, ln_b):
    bt, s_len, d = x.shape
    d_a = a_dw.shape[1]
    d_b = b_dw.shape[1]
    w_in_w = w_in.shape[1]
    ts = SEQ_TILE
    row = lambda v: v.reshape(1, -1)
    kern = functools.partial(_ab_mixer_kernel, d_a=d_a, d_b=d_b)
    return pl.pallas_call(
        kern,
        out_shape=jax.ShapeDtypeStruct((bt, s_len, d), F32),
        grid=(bt, s_len // ts),
        in_specs=[
            pl.BlockSpec((None, ts, d), lambda b, t: (b, t, 0)),
            _const_spec((d, w_in_w)),
            _const_spec((1, w_in_w)),
            _const_spec((CONV_A, d_a)),
            _const_spec((1, d_a)),
            _const_spec((1, d_a)),
            _const_spec((1, d_a)),
            _const_spec((CONV_B, d_b)),
            _const_spec((d_a + d_b, d)),
            _const_spec((1, d)),
            _const_spec((1, d)),
        ],
        out_specs=pl.BlockSpec((None, ts, d), lambda b, t: (b, t, 0)),
        scratch_shapes=[
            pltpu.VMEM((HALO + ts, d_a), F32),
            pltpu.VMEM((HALO + ts, d_b), F32),
            pltpu.VMEM((ts, d_b), F32),
            pltpu.VMEM((ts, d_a + d_b), BF16),
            pltpu.VMEM((SUBLANES, HALO + ts, d_a), F32),
            pltpu.VMEM((len(_B_SHIFTS), HALO + ts, d_b), F32),
        ],
        compiler_params=pltpu.CompilerParams(
            dimension_semantics=("arbitrary", "arbitrary"), vmem_limit_bytes=VMEM_LIMIT),
        name="ab_mixer",
    )(x, w_in.astype(BF16), row(b_in), a_dw, row(a_dw_b), row(a_ln_g), row(a_ln_b),
      b_dw, w_out.astype(BF16), row(ln_g), row(ln_b))


def _cd_mixer_kernel(x_ref, win_ref, bin_ref, clng_ref, clnb_ref, wsc_ref, wsb_ref,
                     bias_ref, wout_ref, lng_ref, lnb_ref, o_ref,
                     qbuf, kbuf, vbuf, wsm, vbd, mbuf, *, d_c, d_d):
    ts = x_ref.shape[0]
    t = pl.program_id(1)
    hc = d_c // HEADS_C
    hd = d_d // HEADS_D

    @pl.when(t == 0)
    def _():
        kbuf[0:HIST, :] = jnp.zeros((HIST, d_d), BF16)
        vbuf[0:HIST, :] = jnp.zeros((HIST, d_d), BF16)
        r = lax.broadcasted_iota(I32, (SGU_BLOCK, HEADS_C * SGU_BLOCK), 0)
        c = lax.broadcasted_iota(I32, (SGU_BLOCK, HEADS_C * SGU_BLOCK), 1) % SGU_BLOCK
        wsm[...] = jnp.where(c // CHUNK <= r // CHUNK, wsc_ref[...], 0.0).astype(BF16)

    x = x_ref[...]
    h = jnp.dot(x.astype(BF16), win_ref[...], preferred_element_type=F32) + bin_ref[...]
    u = h[:, 0:d_c]
    vn = _layer_norm(h[:, d_c:2 * d_c], clng_ref[...], clnb_ref[...])
    qbuf[...] = (h[:, 2 * d_c:2 * d_c + d_d] * (hd ** -0.5)).astype(BF16)
    kbuf[HIST:HIST + ts, :] = h[:, 2 * d_c + d_d:2 * d_c + 2 * d_d].astype(BF16)
    vbuf[HIST:HIST + ts, :] = h[:, 2 * d_c + 2 * d_d:].astype(BF16)

    lane_head = lax.broadcasted_iota(I32, (SGU_BLOCK, d_c), 1) // hc
    for nb in range(ts // SGU_BLOCK):
        v_blk = vn[nb * SGU_BLOCK:(nb + 1) * SGU_BLOCK, :]
        for hh in range(HEADS_C):
            vbd[hh * SGU_BLOCK:(hh + 1) * SGU_BLOCK, :] = jnp.where(
                lane_head == hh, v_blk, 0.0).astype(BF16)
        gate = jnp.dot(wsm[...], vbd[...], preferred_element_type=F32) + wsb_ref[...]
        mbuf[nb * SGU_BLOCK:(nb + 1) * SGU_BLOCK, 0:d_c] = (
            u[nb * SGU_BLOCK:(nb + 1) * SGU_BLOCK, :] * gate).astype(BF16)

    col = lax.broadcasted_iota(I32, (Q_ROWS, K_ROWS), 1)
    for qb in range(ts // Q_ROWS):
        q0 = qb * Q_ROWS
        key_ok = jnp.logical_or(t > 0, col + q0 >= HIST)
        for hh in range(HEADS_D):
            q = qbuf[q0:q0 + Q_ROWS, hh * hd:(hh + 1) * hd]
            k = kbuf[q0:q0 + K_ROWS, hh * hd:(hh + 1) * hd]
            v = vbuf[q0:q0 + K_ROWS, hh * hd:(hh + 1) * hd]
            s = lax.dot_general(q, k, (((1,), (1,)), ((), ())), preferred_element_type=F32)
            s = jnp.where(key_ok, s + bias_ref[hh], NEG_INF)
            p = jnp.exp(s - jnp.max(s, axis=-1, keepdims=True))
            l = jnp.sum(p, axis=-1, keepdims=True)
            o = jnp.dot(p.astype(BF16), v, preferred_element_type=F32)
            mbuf[q0:q0 + Q_ROWS, d_c + hh * hd:d_c + (hh + 1) * hd] = (o / l).astype(BF16)

    kbuf[0:HIST, :] = kbuf[ts:ts + HIST, :]
    vbuf[0:HIST, :] = vbuf[ts:ts + HIST, :]

    mix = jnp.dot(mbuf[...], wout_ref[...], preferred_element_type=F32)
    o_ref[...] = _layer_norm(ALPHA * x + mix, lng_ref[...], lnb_ref[...])


def _attention_bias(d_rel_bias):
    i = jnp.arange(Q_ROWS)[:, None]
    j = jnp.arange(K_ROWS)[None, :]
    jb = j - (i // CHUNK) * CHUNK
    in_band = jnp.logical_and(jb >= 0, jb < (LEFT_CHUNKS + 1) * CHUNK)
    rel_idx = jnp.clip(i + HIST - j, -MAX_REL, MAX_REL) + MAX_REL
    rel = d_rel_bias[:, rel_idx].astype(F32)
    return jnp.where(in_band[None], rel, NEG_INF)


def _cd_mixer(x, w_in, b_in, c_ln_g, c_ln_b, c_ws, c_ws_b, d_rel_bias, w_out, ln_g, ln_b):
    bt, s_len, d = x.shape
    d_c = c_ln_g.shape[0]
    d_d = w_out.shape[0] - d_c
    w_in_w = w_in.shape[1]
    ts = SEQ_TILE
    assert ts == HIST and ts % Q_ROWS == 0 and ts % SGU_BLOCK == 0
    row = lambda v: v.reshape(1, -1)
    ws_cat = jnp.transpose(c_ws, (1, 0, 2)).reshape(SGU_BLOCK, HEADS_C * SGU_BLOCK)
    bias_full = jnp.repeat(c_ws_b.T, d_c // HEADS_C, axis=1)
    att_bias = _attention_bias(d_rel_bias)
    kern = functools.partial(_cd_mixer_kernel, d_c=d_c, d_d=d_d)
    return pl.pallas_call(
        kern,
        out_shape=jax.ShapeDtypeStruct((bt, s_len, d), F32),
        grid=(bt, s_len // ts),
        in_specs=[
            pl.BlockSpec((None, ts, d), lambda b, t: (b, t, 0)),
            _const_spec((d, w_in_w)),
            _const_spec((1, w_in_w)),
            _const_spec((1, d_c)),
            _const_spec((1, d_c)),
            _const_spec((SGU_BLOCK, HEADS_C * SGU_BLOCK)),
            _const_spec((SGU_BLOCK, d_c)),
            _const_spec((HEADS_D, Q_ROWS, K_ROWS)),
            _const_spec((d_c + d_d, d)),
            _const_spec((1, d)),
            _const_spec((1, d)),
        ],
        out_specs=pl.BlockSpec((None, ts, d), lambda b, t: (b, t, 0)),
        scratch_shapes=[
            pltpu.VMEM((ts, d_d), BF16),
            pltpu.VMEM((HIST + ts, d_d), BF16),
            pltpu.VMEM((HIST + ts, d_d), BF16),
            pltpu.VMEM((SGU_BLOCK, HEADS_C * SGU_BLOCK), BF16),
            pltpu.VMEM((HEADS_C * SGU_BLOCK, d_c), BF16),
            pltpu.VMEM((ts, d_c + d_d), BF16),
        ],
        compiler_params=pltpu.CompilerParams(
            dimension_semantics=("arbitrary", "arbitrary"), vmem_limit_bytes=VMEM_LIMIT),
        name="cd_mixer",
    )(x, w_in.astype(BF16), row(b_in), row(c_ln_g), row(c_ln_b), ws_cat, bias_full,
      att_bias, w_out.astype(BF16), row(ln_g), row(ln_b))


def _router_kernel(x_ref, wr_ref, br_ref, rt_ref, rn_ref, cnt_ref):
    tr = x_ref.shape[0]
    logits = jnp.dot(x_ref[...], wr_ref[...], precision=lax.Precision.HIGHEST,
                     preferred_element_type=F32) + br_ref[...]
    lane = lax.broadcasted_iota(I32, (tr, LANES), 1).astype(F32)

    def top1(vals):
        m = jnp.max(vals, axis=-1, keepdims=True)
        idx = jnp.min(jnp.where(vals == m, lane, float(LANES)), axis=-1, keepdims=True)
        return m, idx

    g_mask = lane < N_GROUPS
    g_max, g_idx = top1(jnp.where(g_mask, logits, NEG_INF))
    g_top = 1.0 / jnp.sum(jnp.where(g_mask, jnp.exp(logits - g_max), 0.0), axis=-1, keepdims=True)

    e_lo = ROUTER_LANE0 + EXPERTS_PER_GROUP * g_idx
    e_vals = jnp.where(jnp.logical_and(lane >= e_lo, lane < e_lo + EXPERTS_PER_GROUP), logits, NEG_INF)
    m1, i1 = top1(e_vals)
    m2, i2 = top1(jnp.where(lane == i1, NEG_INF, e_vals))
    ratio = jnp.exp(m2 - m1)
    w1 = g_top / (1.0 + ratio)
    w2 = g_top * ratio / (1.0 + ratio)

    chosen = jnp.logical_or(lane == i1, lane == i2)
    onehot = jnp.where(chosen, 1.0, 0.0)
    rr = lax.broadcasted_iota(I32, (tr, tr), 0)
    cc = lax.broadcasted_iota(I32, (tr, tr), 1)
    before = jnp.where(cc < rr, 1.0, 0.0).astype(BF16)
    seen = jnp.dot(before, onehot.astype(BF16), preferred_element_type=F32)
    r1 = jnp.sum(jnp.where(lane == i1, seen, 0.0), axis=-1, keepdims=True)
    r2 = jnp.sum(jnp.where(lane == i2, seen, 0.0), axis=-1, keepdims=True)
    cnt_ref[...] = jnp.broadcast_to(jnp.sum(onehot, axis=0, keepdims=True), (SUBLANES, LANES))

    fields = (i1 - ROUTER_LANE0, i2 - ROUTER_LANE0, w1, w2, r1, r2)
    res = jnp.zeros((tr, LANES), F32)
    for f, val in enumerate(fields):
        res = jnp.where(lane == f, val, res)
    rn_ref[...] = res
    rt_ref[...] = res.T[0:SUBLANES, :]


def _router(xt, rg_w, rg_b, re_w, re_b):
    n_tok, d = xt.shape
    tr = ROUTER_TILE
    w_all = jnp.concatenate([rg_w, jnp.transpose(re_w, (1, 0, 2)).reshape(d, N_EXPERTS)], axis=1)
    b_all = jnp.concatenate([rg_b, re_b.reshape(N_EXPERTS)])
    pad = LANES - w_all.shape[1]
    w_all = jnp.pad(w_all, ((0, 0), (0, pad)))
    b_all = jnp.pad(b_all, (0, pad)).reshape(1, LANES)
    return pl.pallas_call(
        _router_kernel,
        out_shape=(
            jax.ShapeDtypeStruct((SUBLANES, n_tok), F32),
            jax.ShapeDtypeStruct((n_tok, LANES), F32),
            jax.ShapeDtypeStruct((n_tok // tr, SUBLANES, LANES), F32),
        ),
        grid=(n_tok // tr,),
        in_specs=[
            pl.BlockSpec((tr, d), lambda i: (i, 0)),
            _const_spec((d, LANES)),
            _const_spec((1, LANES)),
        ],
        out_specs=(
            pl.BlockSpec((SUBLANES, tr), lambda i: (0, i)),
            pl.BlockSpec((tr, LANES), lambda i: (i, 0)),
            pl.BlockSpec((None, SUBLANES, LANES), lambda i: (i, 0, 0)),
        ),
        compiler_params=pltpu.CompilerParams(
            dimension_semantics=("arbitrary",), vmem_limit_bytes=VMEM_LIMIT),
        name="router",
    )(xt, w_all, b_all)


def _round_up(v, m):
    return ((v + m - 1) // m) * m


def _num_expert_blocks(n_tok):
    n_tiles = n_tok // MOE_TILE
    rows = n_tok * TOP_K + n_tiles * N_EXPERTS * (SUBLANES - 1)
    return -(-rows // EXPERT_ROWS) + N_EXPERTS


def _routing_tables(cnt, n_tok):
    counts = cnt[:, 0, ROUTER_LANE0:ROUTER_LANE0 + N_EXPERTS].astype(I32)
    seg = _round_up(counts, SUBLANES)
    tile_off = jnp.cumsum(seg, axis=1) - seg
    tot = jnp.sum(seg, axis=0)
    reg = _round_up(tot, EXPERT_ROWS)
    reg_end = jnp.cumsum(reg)
    reg_start = reg_end - reg
    glob_off = reg_start[None, :] + jnp.cumsum(seg, axis=0) - seg
    n_blocks = _num_expert_blocks(n_tok)
    blk_start = jnp.arange(n_blocks, dtype=I32) * EXPERT_ROWS
    block_e = jnp.sum((reg_end[None, :] <= blk_start[:, None]).astype(I32), axis=1)
    block_e = jnp.minimum(block_e, N_EXPERTS - 1).astype(I32)
    n_used = (reg_end[-1] // EXPERT_ROWS).astype(I32).reshape(1)
    flat = lambda a: a.reshape(-1).astype(I32)
    return dict(tile_off=flat(tile_off), glob_off=flat(glob_off), seg=flat(seg),
                tail_start=flat(reg_start + tot), tail_len=flat(reg - tot),
                block_e=block_e, n_used=n_used)


def _piece_sizes(max_len):
    return [SUBLANES << b for b in range((max_len // SUBLANES).bit_length())]


def _for_each_piece(length, max_len, fn):
    off = 0
    for size in reversed(_piece_sizes(max_len)):
        take = (length & size) != 0

        @pl.when(take)
        def _(off=off, size=size):
            fn(off, size)

        off = off + jnp.where(take, size, 0)


def _slot_positions(e1, e2, r1, r2, toff_ref, tile):
    pos1, pos2 = r1, r2
    for e in range(N_EXPERTS):
        off = toff_ref[tile * N_EXPERTS + e]
        pos1 = pos1 + jnp.where(e1 == e, off, 0)
        pos2 = pos2 + jnp.where(e2 == e, off, 0)
    return pos1, pos2


def _dispatch_kernel(toff_ref, goff_ref, seg_ref, tstart_ref, tlen_ref, nu_ref, x_ref, rt_ref,
                     xp_hbm, obuf, zbuf, sem, zsem):
    i = pl.program_id(0)
    n = pl.num_programs(0)
    tile = x_ref.shape[0]
    slot = i % 2

    def segment_copies(t, sl, wait):
        def per_expert(e, c):
            idx = t * N_EXPERTS + e
            t0 = toff_ref[idx]
            g0 = goff_ref[idx]

            def piece(off, size):
                cp = pltpu.make_async_copy(
                    obuf.at[sl, pl.ds(pl.multiple_of(t0 + off, SUBLANES), size), :],
                    xp_hbm.at[pl.ds(pl.multiple_of(g0 + off, SUBLANES), size), :], sem.at[sl])
                cp.wait() if wait else cp.start()

            _for_each_piece(seg_ref[idx], tile, piece)
            return c

        lax.fori_loop(0, N_EXPERTS, per_expert, 0)

    @pl.when(i >= 2)
    def _():
        segment_copies(i - 2, slot, True)

    rt = rt_ref[...].astype(I32)
    pos1, pos2 = _slot_positions(rt[0:1], rt[1:2], rt[4:5], rt[5:6], toff_ref, i)
    xb = x_ref[...].astype(BF16)
    for c in range(SORT_ROWS // SORT_CHUNK):
        rows = lax.broadcasted_iota(I32, (SORT_CHUNK, tile), 0) + c * SORT_CHUNK
        hit = jnp.logical_or(rows == pos1, rows == pos2)
        obuf[slot, c * SORT_CHUNK:(c + 1) * SORT_CHUNK, :] = jnp.dot(
            jnp.where(hit, 1.0, 0.0).astype(BF16), xb, preferred_element_type=F32)
    segment_copies(i, slot, False)

    def zero_fill(wait):
        def per_expert(e, c):
            def piece(off, size):
                cp = pltpu.make_async_copy(
                    zbuf.at[pl.ds(0, size), :],
                    xp_hbm.at[pl.ds(pl.multiple_of(tstart_ref[e] + off, SUBLANES), size), :], zsem)
                cp.wait() if wait else cp.start()

            _for_each_piece(tlen_ref[e], EXPERT_ROWS - SUBLANES, piece)
            return c

        lax.fori_loop(0, N_EXPERTS, per_expert, 0)

        def per_half_block(h, c):
            cp = pltpu.make_async_copy(
                zbuf, xp_hbm.at[pl.ds(pl.multiple_of(h * zbuf.shape[0], zbuf.shape[0]), zbuf.shape[0]), :], zsem)
            cp.wait() if wait else cp.start()
            return c

        halves = EXPERT_ROWS // zbuf.shape[0]
        lax.fori_loop(nu_ref[0] * halves, (xp_hbm.shape[0] // EXPERT_ROWS) * halves, per_half_block, 0)

    @pl.when(i == 0)
    def _():
        zbuf[...] = jnp.zeros(zbuf.shape, F32)
        zero_fill(False)

    @pl.when(i == n - 1)
    def _():
        @pl.when(i >= 1)
        def _():
            segment_copies(i - 1, 1 - slot, True)

        segment_copies(i, slot, True)
        zero_fill(True)


def _dispatch(xt, rt, tabs):
    n_tok, d = xt.shape
    tile = MOE_TILE
    n_rows = _num_expert_blocks(n_tok) * EXPERT_ROWS
    return pl.pallas_call(
        _dispatch_kernel,
        out_shape=jax.ShapeDtypeStruct((n_rows, d), F32),
        grid_spec=pltpu.PrefetchScalarGridSpec(
            num_scalar_prefetch=6,
            grid=(n_tok // tile,),
            in_specs=[
                pl.BlockSpec((tile, d), lambda i, *_: (i, 0)),
                pl.BlockSpec((SUBLANES, tile), lambda i, *_: (0, i)),
            ],
            out_specs=pl.BlockSpec(memory_space=pl.ANY),
            scratch_shapes=[
                pltpu.VMEM((2, SORT_ROWS, d), F32),
                pltpu.VMEM((EXPERT_ROWS // 2, d), F32),
                pltpu.SemaphoreType.DMA((2,)),
                pltpu.SemaphoreType.DMA,
            ],
        ),
        compiler_params=pltpu.CompilerParams(
            dimension_semantics=("arbitrary",), vmem_limit_bytes=VMEM_LIMIT),
        name="dispatch",
    )(tabs["tile_off"], tabs["glob_off"], tabs["seg"], tabs["tail_start"], tabs["tail_len"],
      tabs["n_used"], xt, rt)


def _ffn_kernel(be_ref, nu_ref, x_ref, wg_ref, wu_ref, wd_ref, o_ref):
    used = pl.program_id(0) < nu_ref[0]

    @pl.when(used)
    def _():
        xb = x_ref[...].astype(BF16)
        g = jnp.dot(xb, wg_ref[...], preferred_element_type=F32)
        u = jnp.dot(xb, wu_ref[...], preferred_element_type=F32)
        hb = (g * jax.nn.sigmoid(g) * u).astype(BF16)
        o_ref[...] = jnp.dot(hb, wd_ref[...], preferred_element_type=F32)

    @pl.when(jnp.logical_not(used))
    def _():
        o_ref[...] = jnp.zeros(o_ref.shape, F32)


def _expert_ffn(xp, block_e, n_used, w_gate, w_up, w_down):
    n_rows, d = xp.shape
    d_e = w_gate.shape[2]
    n_blocks = n_rows // EXPERT_ROWS
    blk = lambda i, be, nu: (jnp.minimum(i, nu[0] - 1), 0)
    wsel = lambda i, be, nu: (be[jnp.minimum(i, nu[0] - 1)], 0, 0)
    return pl.pallas_call(
        _ffn_kernel,
        out_shape=jax.ShapeDtypeStruct((n_rows, d), F32),
        grid_spec=pltpu.PrefetchScalarGridSpec(
            num_scalar_prefetch=2,
            grid=(n_blocks,),
            in_specs=[
                pl.BlockSpec((EXPERT_ROWS, d), blk),
                pl.BlockSpec((None, d, d_e), wsel),
                pl.BlockSpec((None, d, d_e), wsel),
                pl.BlockSpec((None, d_e, d), wsel),
            ],
            out_specs=pl.BlockSpec((EXPERT_ROWS, d), lambda i, be, nu: (i, 0)),
        ),
        compiler_params=pltpu.CompilerParams(
            dimension_semantics=("arbitrary",), vmem_limit_bytes=VMEM_LIMIT),
        name="expert_ffn",
    )(block_e, n_used, xp, w_gate.astype(BF16), w_up.astype(BF16), w_down.astype(BF16))


def _combine_kernel(toff_ref, goff_ref, seg_ref, x_ref, rn_ref, yp_hbm, lng_ref, lnb_ref, o_ref,
                    ybuf, sem):
    i = pl.program_id(0)
    n = pl.num_programs(0)
    tile = x_ref.shape[0]
    slot = i % 2

    def segment_copies(t, sl, wait):
        def per_expert(e, c):
            idx = t * N_EXPERTS + e
            t0 = toff_ref[idx]
            g0 = goff_ref[idx]

            def piece(off, size):
                cp = pltpu.make_async_copy(
                    yp_hbm.at[pl.ds(pl.multiple_of(g0 + off, SUBLANES), size), :],
                    ybuf.at[sl, pl.ds(pl.multiple_of(t0 + off, SUBLANES), size), :], sem.at[sl])
                cp.wait() if wait else cp.start()

            _for_each_piece(seg_ref[idx], tile, piece)
            return c

        lax.fori_loop(0, N_EXPERTS, per_expert, 0)

    @pl.when(i == 0)
    def _():
        ybuf[...] = jnp.zeros(ybuf.shape, F32)
        segment_copies(0, 0, False)

    @pl.when(i + 1 < n)
    def _():
        segment_copies(i + 1, 1 - slot, False)

    segment_copies(i, slot, True)

    rn = rn_ref[...]
    ri = rn.astype(I32)
    pos1, pos2 = _slot_positions(ri[:, 0:1], ri[:, 1:2], ri[:, 4:5], ri[:, 5:6], toff_ref, i)
    w1, w2 = rn[:, 2:3], rn[:, 3:4]
    ffn = jnp.zeros(o_ref.shape, F32)
    for c in range(SORT_ROWS // SORT_CHUNK):
        cols = lax.broadcasted_iota(I32, (tile, SORT_CHUNK), 1) + c * SORT_CHUNK
        pick = jnp.where(cols == pos1, w1, 0.0) + jnp.where(cols == pos2, w2, 0.0)
        ffn = ffn + jnp.dot(pick.astype(BF16),
                            ybuf[slot, c * SORT_CHUNK:(c + 1) * SORT_CHUNK, :].astype(BF16),
                            preferred_element_type=F32)
    o_ref[...] = _layer_norm(ALPHA * x_ref[...] + ffn, lng_ref[...], lnb_ref[...])


def _combine(xt, rn, yp, tabs, ln_g, ln_b):
    n_tok, d = xt.shape
    tile = MOE_TILE
    return pl.pallas_call(
        _combine_kernel,
        out_shape=jax.ShapeDtypeStruct((n_tok, d), F32),
        grid_spec=pltpu.PrefetchScalarGridSpec(
            num_scalar_prefetch=3,
            grid=(n_tok // tile,),
            in_specs=[
                pl.BlockSpec((tile, d), lambda i, *_: (i, 0)),
                pl.BlockSpec((tile, LANES), lambda i, *_: (i, 0)),
                pl.BlockSpec(memory_space=pl.ANY),
                pl.BlockSpec((1, d), lambda i, *_: (0, 0)),
                pl.BlockSpec((1, d), lambda i, *_: (0, 0)),
            ],
            out_specs=pl.BlockSpec((tile, d), lambda i, *_: (i, 0)),
            scratch_shapes=[
                pltpu.VMEM((2, SORT_ROWS, d), F32),
                pltpu.SemaphoreType.DMA((2,)),
            ],
        ),
        compiler_params=pltpu.CompilerParams(
            dimension_semantics=("arbitrary",), vmem_limit_bytes=VMEM_LIMIT),
        name="combine",
    )(tabs["tile_off"], tabs["glob_off"], tabs["seg"], xt, rn, yp,
      ln_g.reshape(1, d), ln_b.reshape(1, d))


def _moe_layer(x, rg_w, rg_b, re_w, re_b, w_gate, w_up, w_down, ln_g, ln_b):
    bt, s_len, d = x.shape
    n_tok = bt * s_len
    xt = x.reshape(n_tok, d)
    rt, rn, cnt = _router(xt, rg_w, rg_b, re_w, re_b)
    tabs = _routing_tables(cnt, n_tok)
    xp = _dispatch(xt, rt, tabs)
    yp = _expert_ffn(xp, tabs["block_e"], tabs["n_used"], w_gate, w_up, w_down)
    out = _combine(xt, rn, yp, tabs, ln_g, ln_b)
    return out.reshape(bt, s_len, d)


def kernel(x, ab_w_in, ab_b_in, a_dw, a_dw_b, a_ln_g, a_ln_b, b_dw, ab_w_out, cd_w_in, cd_b_in, c_ln_g, c_ln_b, c_ws, c_ws_b, d_rel_bias, cd_w_out, mix_ln_g, mix_ln_b, moe_rg_w, moe_rg_b, moe_re_w, moe_re_b, moe_w_gate, moe_w_up, moe_w_down, ffn_ln_g, ffn_ln_b):
    for layer in range(DEPTH):
        i = layer // 2
        if layer % 2 == 0:
            x = _ab_mixer(x, ab_w_in[i], ab_b_in[i], a_dw[i], a_dw_b[i], a_ln_g[i], a_ln_b[i],
                          b_dw[i], ab_w_out[i], mix_ln_g[layer], mix_ln_b[layer])
        else:
            x = _cd_mixer(x, cd_w_in[i], cd_b_in[i], c_ln_g[i], c_ln_b[i], c_ws[i], c_ws_b[i],
                          d_rel_bias[i], cd_w_out[i], mix_ln_g[layer], mix_ln_b[layer])
        x = _moe_layer(x, moe_rg_w[layer], moe_rg_b[layer], moe_re_w[layer], moe_re_b[layer],
                       moe_w_gate[layer], moe_w_up[layer], moe_w_down[layer],
                       ffn_ln_g[layer], ffn_ln_b[layer])
    return x
```

```python
import functools

import jax
import jax.numpy as jnp
from jax import lax
from jax.experimental import pallas as pl
from jax.experimental.pallas import tpu as pltpu

F32 = jnp.float32
BF16 = jnp.bfloat16
I32 = jnp.int32

DEPTH = 2
CHUNK = 64
LEFT_CHUNKS = 8
HIST = LEFT_CHUNKS * CHUNK
CONV_A = 31
CONV_B = 3
HEADS_C = 8
SGU_BLOCK = 128
HEADS_D = 8
MAX_REL = 256
N_GROUPS = 4
EXPERTS_PER_GROUP = 8
N_EXPERTS = N_GROUPS * EXPERTS_PER_GROUP
TOP_K = 2
ALPHA = (2 * DEPTH) ** 0.25
LN_EPS = 1e-5
NEG_INF = -1e30

LANES = 128
SUBLANES = 8
VMEM_LIMIT = 56 * 1024 * 1024

SEQ_TILE = 512
CONV_ROWS = 32
HALO = 32
Q_ROWS = 256
K_ROWS = Q_ROWS + HIST
MOE_TILE = 512
ROUTER_TILE = MOE_TILE
EXPERT_ROWS = 256
SORT_ROWS = MOE_TILE * TOP_K + N_EXPERTS * SUBLANES
SORT_CHUNK = 256
ROUTER_LANE0 = N_GROUPS
_B_SHIFTS = sorted({(HALO - (CONV_B - 1) + j) % SUBLANES for j in range(CONV_B)} - {0})


def _layer_norm(x, g, b):
    mu = jnp.mean(x, axis=-1, keepdims=True)
    xc = x - mu
    var = jnp.mean(xc * xc, axis=-1, keepdims=True)
    return xc * lax.rsqrt(var + LN_EPS) * g + b


def _const_spec(shape):
    nd = len(shape)
    return pl.BlockSpec(shape, lambda *_: (0,) * nd, pipeline_mode=pl.Buffered(1))


def _ab_mixer_kernel(x_ref, win_ref, bin_ref, adw_ref, adwb_ref, alng_ref, alnb_ref,
                     bdw_ref, wout_ref, lng_ref, lnb_ref, o_ref,
                     abuf, cbuf, gbuf, mbuf, asht, csht, *, d_a, d_b):
    ts = x_ref.shape[0]

    @pl.when(pl.program_id(1) == 0)
    def _():
        abuf[0:HALO, :] = jnp.zeros((HALO, d_a), F32)
        cbuf[0:HALO, :] = jnp.zeros((HALO, d_b), F32)

    x = x_ref[...]
    h = jnp.dot(x.astype(BF16), win_ref[...], preferred_element_type=F32) + bin_ref[...]
    abuf[HALO:HALO + ts, :] = h[:, 0:d_a] * jax.nn.sigmoid(h[:, d_a:2 * d_a])
    gbuf[...] = h[:, 2 * d_a:2 * d_a + d_b]
    cbuf[HALO:HALO + ts, :] = h[:, 2 * d_a + d_b:2 * d_a + 2 * d_b] * h[:, 2 * d_a + 2 * d_b:]

    n_sh = HALO + ts - SUBLANES
    for sh in range(1, SUBLANES):
        asht[sh, 0:n_sh, :] = abuf[sh:sh + n_sh, :]
    for sh in _B_SHIFTS:
        csht[_B_SHIFTS.index(sh), 0:n_sh, :] = cbuf[sh:sh + n_sh, :]

    def tap(buf, sht, slot_of, r0, off):
        base = r0 + (off // SUBLANES) * SUBLANES
        if off % SUBLANES == 0:
            return buf[pl.ds(base, CONV_ROWS), :]
        return sht[slot_of(off % SUBLANES), pl.ds(base, CONV_ROWS), :]

    def conv_chunk(i, carry):
        r0 = pl.multiple_of(i * CONV_ROWS, CONV_ROWS)
        acc = jnp.broadcast_to(adwb_ref[...], (CONV_ROWS, d_a))
        for j in range(CONV_A):
            acc = acc + adw_ref[j:j + 1, :] * tap(abuf, asht, lambda sh: sh, r0, HALO - (CONV_A - 1) + j)
        a = _layer_norm(acc, alng_ref[...], alnb_ref[...])
        mbuf[pl.ds(r0, CONV_ROWS), 0:d_a] = (a * jax.nn.sigmoid(a)).astype(BF16)
        s = jnp.zeros((CONV_ROWS, d_b), F32)
        for j in range(CONV_B):
            s = s + bdw_ref[j:j + 1, :] * tap(cbuf, csht, _B_SHIFTS.index, r0, HALO - (CONV_B - 1) + j)
        mbuf[pl.ds(r0, CONV_ROWS), d_a:d_a + d_b] = (gbuf[pl.ds(r0, CONV_ROWS), :] * s).astype(BF16)
        return carry

    lax.fori_loop(0, ts // CONV_ROWS, conv_chunk, 0, unroll=2)

    abuf[0:HALO, :] = abuf[ts:ts + HALO, :]
    cbuf[0:HALO, :] = cbuf[ts:ts + HALO, :]

    mix = jnp.dot(mbuf[...], wout_ref[...], preferred_element_type=F32)
    o_ref[...] = _layer_norm(ALPHA * x + mix, lng_ref[...], lnb_ref[...])


def _ab_mixer(x, w_in, b_in, a_dw, a_dw_b, a_ln_g, a_ln_b, b_dw, w_out, ln_g, ln_b):
    bt, s_len, d = x.shape
    d_a = a_dw.shape[1]
    d_b = b_dw.shape[1]
    w_in_w = w_in.shape[1]
    ts = SEQ_TILE
    row = lambda v: v.reshape(1, -1)
    kern = functools.partial(_ab_mixer_kernel, d_a=d_a, d_b=d_b)
    return pl.pallas_call(
        kern,
        out_shape=jax.ShapeDtypeStruct((bt, s_len, d), F32),
        grid=(bt, s_len // ts),
        in_specs=[
            pl.BlockSpec((None, ts, d), lambda b, t: (b, t, 0)),
            _const_spec((d, w_in_w)),
            _const_spec((1, w_in_w)),
            _const_spec((CONV_A, d_a)),
            _const_spec((1, d_a)),
            _const_spec((1, d_a)),
            _const_spec((1, d_a)),
            _const_spec((CONV_B, d_b)),
            _const_spec((d_a + d_b, d)),
            _const_spec((1, d)),
            _const_spec((1, d)),
        ],
        out_specs=pl.BlockSpec((None, ts, d), lambda b, t: (b, t, 0)),
        scratch_shapes=[
            pltpu.VMEM((HALO + ts, d_a), F32),
            pltpu.VMEM((HALO + ts, d_b), F32),
            pltpu.VMEM((ts, d_b), F32),
            pltpu.VMEM((ts, d_a + d_b), BF16),
            pltpu.VMEM((SUBLANES, HALO + ts, d_a), F32),
            pltpu.VMEM((len(_B_SHIFTS), HALO + ts, d_b), F32),
        ],
        compiler_params=pltpu.CompilerParams(
            dimension_semantics=("arbitrary", "arbitrary"), vmem_limit_bytes=VMEM_LIMIT),
        name="ab_mixer",
    )(x, w_in.astype(BF16), row(b_in), a_dw, row(a_dw_b), row(a_ln_g), row(a_ln_b),
      b_dw, w_out.astype(BF16), row(ln_g), row(ln_b))


def _cd_mixer_kernel(x_ref, win_ref, bin_ref, clng_ref, clnb_ref, wsc_ref, wsb_ref,
                     bias_ref, wout_ref, lng_ref, lnb_ref, o_ref,
                     qbuf, kbuf, vbuf, wsm, vbd, mbuf, *, d_c, d_d):
    ts = x_ref.shape[0]
    t = pl.program_id(1)
    hc = d_c // HEADS_C
    hd = d_d // HEADS_D

    @pl.when(t == 0)
    def _():
        kbuf[0:HIST, :] = jnp.zeros((HIST, d_d), BF16)
        vbuf[0:HIST, :] = jnp.zeros((HIST, d_d), BF16)
        r = lax.broadcasted_iota(I32, (SGU_BLOCK, HEADS_C * SGU_BLOCK), 0)
        c = lax.broadcasted_iota(I32, (SGU_BLOCK, HEADS_C * SGU_BLOCK), 1) % SGU_BLOCK
        wsm[...] = jnp.where(c // CHUNK <= r // CHUNK, wsc_ref[...], 0.0).astype(BF16)

    x = x_ref[...]
    h = jnp.dot(x.astype(BF16), win_ref[...], preferred_element_type=F32) + bin_ref[...]
    u = h[:, 0:d_c]
    vn = _layer_norm(h[:, d_c:2 * d_c], clng_ref[...], clnb_ref[...])
    qbuf[...] = (h[:, 2 * d_c:2 * d_c + d_d] * (hd ** -0.5)).astype(BF16)
    kbuf[HIST:HIST + ts, :] = h[:, 2 * d_c + d_d:2 * d_c + 2 * d_d].astype(BF16)
    vbuf[HIST:HIST + ts, :] = h[:, 2 * d_c + 2 * d_d:].astype(BF16)

    lane_head = lax.broadcasted_iota(I32, (SGU_BLOCK, d_c), 1) // hc
    for nb in range(ts // SGU_BLOCK):
        v_blk = vn[nb * SGU_BLOCK:(nb + 1) * SGU_BLOCK, :]
        for hh in range(HEADS_C):
            vbd[hh * SGU_BLOCK:(hh + 1) * SGU_BLOCK, :] = jnp.where(
                lane_head == hh, v_blk, 0.0).astype(BF16)
        gate = jnp.dot(wsm[...], vbd[...], preferred_element_type=F32) + wsb_ref[...]
        mbuf[nb * SGU_BLOCK:(nb + 1) * SGU_BLOCK, 0:d_c] = (
            u[nb * SGU_BLOCK:(nb + 1) * SGU_BLOCK, :] * gate).astype(BF16)

    col = lax.broadcasted_iota(I32, (Q_ROWS, K_ROWS), 1)
    for qb in range(ts // Q_ROWS):
        q0 = qb * Q_ROWS
        key_ok = jnp.logical_or(t > 0, col + q0 >= HIST)
        for hh in range(HEADS_D):
            q = qbuf[q0:q0 + Q_ROWS, hh * hd:(hh + 1) * hd]
            k = kbuf[q0:q0 + K_ROWS, hh * hd:(hh + 1) * hd]
            v = vbuf[q0:q0 + K_ROWS, hh * hd:(hh + 1) * hd]
            s = lax.dot_general(q, k, (((1,), (1,)), ((), ())), preferred_element_type=F32)
            s = jnp.where(key_ok, s + bias_ref[hh], NEG_INF)
            p = jnp.exp(s - jnp.max(s, axis=-1, keepdims=True))
            l = jnp.sum(p, axis=-1, keepdims=True)
            o = jnp.dot(p.astype(BF16), v, preferred_element_type=F32)
            mbuf[q0:q0 + Q_ROWS, d_c + hh * hd:d_c + (hh + 1) * hd] = (o / l).astype(BF16)

    kbuf[0:HIST, :] = kbuf[ts:ts + HIST, :]
    vbuf[0:HIST, :] = vbuf[ts:ts + HIST, :]

    mix = jnp.dot(mbuf[...], wout_ref[...], preferred_element_type=F32)
    o_ref[...] = _layer_norm(ALPHA * x + mix, lng_ref[...], lnb_ref[...])


def _attention_bias(d_rel_bias):
    heads = d_rel_bias.shape[0]
    i = jnp.arange(Q_ROWS)[:, None]
    j = jnp.arange(K_ROWS)[None, :]
    jb = j - (i // CHUNK) * CHUNK
    in_band = jnp.logical_and(jb >= 0, jb < (LEFT_CHUNKS + 1) * CHUNK)
    n_diag = Q_ROWS + K_ROWS - 1
    k = jnp.arange(n_diag)
    diag = d_rel_bias[:, jnp.clip(k - (K_ROWS - 1) + HIST, -MAX_REL, MAX_REL) + MAX_REL].astype(F32)
    flat = jnp.tile(diag, (1, Q_ROWS + 1))[:, :Q_ROWS * (n_diag + 1)]
    rel = flat.reshape(heads, Q_ROWS, n_diag + 1)[:, :, :K_ROWS][:, :, ::-1]
    return jnp.where(in_band[None], rel, NEG_INF)


def _cd_mixer(x, w_in, b_in, c_ln_g, c_ln_b, c_ws, c_ws_b, d_rel_bias, w_out, ln_g, ln_b):
    bt, s_len, d = x.shape
    d_c = c_ln_g.shape[0]
    d_d = w_out.shape[0] - d_c
    w_in_w = w_in.shape[1]
    ts = SEQ_TILE
    assert ts == HIST and ts % Q_ROWS == 0 and ts % SGU_BLOCK == 0
    row = lambda v: v.reshape(1, -1)
    ws_cat = jnp.transpose(c_ws, (1, 0, 2)).reshape(SGU_BLOCK, HEADS_C * SGU_BLOCK)
    bias_full = jnp.repeat(c_ws_b.T, d_c // HEADS_C, axis=1)
    att_bias = _attention_bias(d_rel_bias)
    kern = functools.partial(_cd_mixer_kernel, d_c=d_c, d_d=d_d)
    return pl.pallas_call(
        kern,
        out_shape=jax.ShapeDtypeStruct((bt, s_len, d), F32),
        grid=(bt, s_len // ts),
        in_specs=[
            pl.BlockSpec((None, ts, d), lambda b, t: (b, t, 0)),
            _const_spec((d, w_in_w)),
            _const_spec((1, w_in_w)),
            _const_spec((1, d_c)),
            _const_spec((1, d_c)),
            _const_spec((SGU_BLOCK, HEADS_C * SGU_BLOCK)),
            _const_spec((SGU_BLOCK, d_c)),
            _const_spec((HEADS_D, Q_ROWS, K_ROWS)),
            _const_spec((d_c + d_d, d)),
            _const_spec((1, d)),
            _const_spec((1, d)),
        ],
        out_specs=pl.BlockSpec((None, ts, d), lambda b, t: (b, t, 0)),
        scratch_shapes=[
            pltpu.VMEM((ts, d_d), BF16),
            pltpu.VMEM((HIST + ts, d_d), BF16),
            pltpu.VMEM((HIST + ts, d_d), BF16),
            pltpu.VMEM((SGU_BLOCK, HEADS_C * SGU_BLOCK), BF16),
            pltpu.VMEM((HEADS_C * SGU_BLOCK, d_c), BF16),
            pltpu.VMEM((ts, d_c + d_d), BF16),
        ],
        compiler_params=pltpu.CompilerParams(
            dimension_semantics=("arbitrary", "arbitrary"), vmem_limit_bytes=VMEM_LIMIT),
        name="cd_mixer",
    )(x, w_in.astype(BF16), row(b_in), row(c_ln_g), row(c_ln_b), ws_cat, bias_full,
      att_bias, w_out.astype(BF16), row(ln_g), row(ln_b))


def _router_kernel(x_ref, wr_ref, br_ref, rt_ref, rn_ref, cnt_ref):
    tr = x_ref.shape[0]
    x = x_ref[...]
    x_hi = x.astype(BF16)
    x_lo = (x - x_hi.astype(F32)).astype(BF16)
    p_hi = jnp.dot(x_hi, wr_ref[...], preferred_element_type=F32)
    p_lo = jnp.dot(x_lo, wr_ref[...], preferred_element_type=F32)
    logits = p_hi + pltpu.roll(p_hi, LANES // 2, axis=1) + p_lo + br_ref[...]
    lane = lax.broadcasted_iota(I32, (tr, LANES), 1).astype(F32)

    def top1(vals):
        m = jnp.max(vals, axis=-1, keepdims=True)
        idx = jnp.min(jnp.where(vals == m, lane, float(LANES)), axis=-1, keepdims=True)
        return m, idx

    g_mask = lane < N_GROUPS
    g_max, g_idx = top1(jnp.where(g_mask, logits, NEG_INF))
    g_top = 1.0 / jnp.sum(jnp.where(g_mask, jnp.exp(logits - g_max), 0.0), axis=-1, keepdims=True)

    e_lo = ROUTER_LANE0 + EXPERTS_PER_GROUP * g_idx
    e_vals = jnp.where(jnp.logical_and(lane >= e_lo, lane < e_lo + EXPERTS_PER_GROUP), logits, NEG_INF)
    m1, i1 = top1(e_vals)
    m2, i2 = top1(jnp.where(lane == i1, NEG_INF, e_vals))
    ratio = jnp.exp(m2 - m1)
    w1 = g_top / (1.0 + ratio)
    w2 = g_top * ratio / (1.0 + ratio)

    chosen = jnp.logical_or(lane == i1, lane == i2)
    onehot = jnp.where(chosen, 1.0, 0.0)
    rr = lax.broadcasted_iota(I32, (tr, tr), 0)
    cc = lax.broadcasted_iota(I32, (tr, tr), 1)
    before = jnp.where(cc < rr, 1.0, 0.0).astype(BF16)
    seen = jnp.dot(before, onehot.astype(BF16), preferred_element_type=F32)
    counts = jnp.broadcast_to(jnp.sum(onehot, axis=0, keepdims=True), (SUBLANES, LANES))
    cnt_ref[...] = counts

    groups = (counts.astype(I32) + (SUBLANES - 1)) // SUBLANES
    lr = lax.broadcasted_iota(I32, (LANES, LANES), 0)
    lc = lax.broadcasted_iota(I32, (LANES, LANES), 1)
    lower = jnp.where(lr < lc, 1.0, 0.0).astype(BF16)
    seg_start = SUBLANES * jnp.dot(groups.astype(F32).astype(BF16), lower, preferred_element_type=F32)
    row = seen + seg_start[0:1, :]
    pos1 = jnp.sum(jnp.where(lane == i1, row, 0.0), axis=-1, keepdims=True)
    pos2 = jnp.sum(jnp.where(lane == i2, row, 0.0), axis=-1, keepdims=True)

    fields = (i1 - ROUTER_LANE0, i2 - ROUTER_LANE0, w1, w2, pos1, pos2)
    res = jnp.zeros((tr, LANES), F32)
    for f, val in enumerate(fields):
        res = jnp.where(lane == f, val, res)
    rn_ref[...] = res
    rt_ref[...] = res.T[0:SUBLANES, :]


def _router(xt, rg_w, rg_b, re_w, re_b):
    n_tok, d = xt.shape
    tr = ROUTER_TILE
    w_all = jnp.concatenate([rg_w, jnp.transpose(re_w, (1, 0, 2)).reshape(d, N_EXPERTS)], axis=1)
    b_all = jnp.concatenate([rg_b, re_b.reshape(N_EXPERTS)])
    half = LANES // 2
    pad = half - w_all.shape[1]
    w_hi = w_all.astype(BF16)
    w_lo = (w_all - w_hi.astype(F32)).astype(BF16)
    w_all = jnp.concatenate([jnp.pad(w_hi, ((0, 0), (0, pad))), jnp.pad(w_lo, ((0, 0), (0, pad)))], axis=1)
    b_all = jnp.pad(b_all, (0, LANES - b_all.shape[0])).reshape(1, LANES)
    return pl.pallas_call(
        _router_kernel,
        out_shape=(
            jax.ShapeDtypeStruct((SUBLANES, n_tok), F32),
            jax.ShapeDtypeStruct((n_tok, LANES), F32),
            jax.ShapeDtypeStruct((n_tok // tr, SUBLANES, LANES), F32),
        ),
        grid=(n_tok // tr,),
        in_specs=[
            pl.BlockSpec((tr, d), lambda i: (i, 0)),
            _const_spec((d, LANES)),
            _const_spec((1, LANES)),
        ],
        out_specs=(
            pl.BlockSpec((SUBLANES, tr), lambda i: (0, i)),
            pl.BlockSpec((tr, LANES), lambda i: (i, 0)),
            pl.BlockSpec((None, SUBLANES, LANES), lambda i: (i, 0, 0)),
        ),
        compiler_params=pltpu.CompilerParams(
            dimension_semantics=("arbitrary",), vmem_limit_bytes=VMEM_LIMIT),
        name="router",
    )(xt, w_all, b_all)


def _round_up(v, m):
    return ((v + m - 1) // m) * m


def _num_expert_blocks(n_tok):
    n_tiles = n_tok // MOE_TILE
    rows = n_tok * TOP_K + n_tiles * N_EXPERTS * (SUBLANES - 1)
    return -(-rows // EXPERT_ROWS) + N_EXPERTS


def _routing_tables(cnt, n_tok):
    counts = cnt[:, 0, ROUTER_LANE0:ROUTER_LANE0 + N_EXPERTS].astype(I32)
    seg = _round_up(counts, SUBLANES)
    tile_off = jnp.cumsum(seg, axis=1) - seg
    tot = jnp.sum(seg, axis=0)
    reg = _round_up(tot, EXPERT_ROWS)
    reg_end = jnp.cumsum(reg)
    reg_start = reg_end - reg
    glob_off = reg_start[None, :] + jnp.cumsum(seg, axis=0) - seg
    n_blocks = _num_expert_blocks(n_tok)
    blk_start = jnp.arange(n_blocks, dtype=I32) * EXPERT_ROWS
    block_e = jnp.sum((reg_end[None, :] <= blk_start[:, None]).astype(I32), axis=1)
    block_e = jnp.minimum(block_e, N_EXPERTS - 1).astype(I32)
    n_used = (reg_end[-1] // EXPERT_ROWS).astype(I32).reshape(1)
    flat = lambda a: a.reshape(-1).astype(I32)
    return dict(tile_off=flat(tile_off), glob_off=flat(glob_off), seg=flat(seg),
                tail_start=flat(reg_start + tot), tail_len=flat(reg - tot),
                block_e=block_e, n_used=n_used)


def _piece_sizes(max_len):
    return [SUBLANES << b for b in range((max_len // SUBLANES).bit_length())]


def _for_each_piece(length, max_len, fn):
    off = 0
    for size in reversed(_piece_sizes(max_len)):
        take = (length & size) != 0

        @pl.when(take)
        def _(off=off, size=size):
            fn(off, size)

        off = off + jnp.where(take, size, 0)


def _dispatch_kernel(toff_ref, goff_ref, seg_ref, tstart_ref, tlen_ref, nu_ref, x_ref, rt_ref,
                     xp_hbm, obuf, zbuf, sem, zsem):
    i = pl.program_id(0)
    n = pl.num_programs(0)
    tile = x_ref.shape[0]
    slot = i % 2

    def segment_copies(t, sl, wait):
        def per_expert(e, c):
            idx = t * N_EXPERTS + e
            t0 = toff_ref[idx]
            g0 = goff_ref[idx]

            def piece(off, size):
                cp = pltpu.make_async_copy(
                    obuf.at[sl, pl.ds(pl.multiple_of(t0 + off, SUBLANES), size), :],
                    xp_hbm.at[pl.ds(pl.multiple_of(g0 + off, SUBLANES), size), :], sem.at[sl])
                cp.wait() if wait else cp.start()

            _for_each_piece(seg_ref[idx], tile, piece)
            return c

        lax.fori_loop(0, N_EXPERTS, per_expert, 0)

    @pl.when(i >= 2)
    def _():
        segment_copies(i - 2, slot, True)

    pos1 = rt_ref[4:5, :].astype(I32)
    pos2 = rt_ref[5:6, :].astype(I32)
    xb = x_ref[...].astype(BF16)
    for c in range(SORT_ROWS // SORT_CHUNK):
        rows = lax.broadcasted_iota(I32, (SORT_CHUNK, tile), 0) + c * SORT_CHUNK
        hit = jnp.logical_or(rows == pos1, rows == pos2)
        obuf[slot, c * SORT_CHUNK:(c + 1) * SORT_CHUNK, :] = jnp.dot(
            jnp.where(hit, 1.0, 0.0).astype(BF16), xb, preferred_element_type=F32)
    segment_copies(i, slot, False)

    def zero_fill(wait):
        def per_expert(e, c):
            def piece(off, size):
                cp = pltpu.make_async_copy(
                    zbuf.at[pl.ds(0, size), :],
                    xp_hbm.at[pl.ds(pl.multiple_of(tstart_ref[e] + off, SUBLANES), size), :], zsem)
                cp.wait() if wait else cp.start()

            _for_each_piece(tlen_ref[e], EXPERT_ROWS - SUBLANES, piece)
            return c

        lax.fori_loop(0, N_EXPERTS, per_expert, 0)

        def per_half_block(h, c):
            cp = pltpu.make_async_copy(
                zbuf, xp_hbm.at[pl.ds(pl.multiple_of(h * zbuf.shape[0], zbuf.shape[0]), zbuf.shape[0]), :], zsem)
            cp.wait() if wait else cp.start()
            return c

        halves = EXPERT_ROWS // zbuf.shape[0]
        lax.fori_loop(nu_ref[0] * halves, (xp_hbm.shape[0] // EXPERT_ROWS) * halves, per_half_block, 0)

    @pl.when(i == 0)
    def _():
        zbuf[...] = jnp.zeros(zbuf.shape, F32)
        zero_fill(False)

    @pl.when(i == n - 1)
    def _():
        @pl.when(i >= 1)
        def _():
            segment_copies(i - 1, 1 - slot, True)

        segment_copies(i, slot, True)
        zero_fill(True)


def _dispatch(xt, rt, tabs):
    n_tok, d = xt.shape
    tile = MOE_TILE
    n_rows = _num_expert_blocks(n_tok) * EXPERT_ROWS
    return pl.pallas_call(
        _dispatch_kernel,
        out_shape=jax.ShapeDtypeStruct((n_rows, d), F32),
        grid_spec=pltpu.PrefetchScalarGridSpec(
            num_scalar_prefetch=6,
            grid=(n_tok // tile,),
            in_specs=[
                pl.BlockSpec((tile, d), lambda i, *_: (i, 0)),
                pl.BlockSpec((SUBLANES, tile), lambda i, *_: (0, i)),
            ],
            out_specs=pl.BlockSpec(memory_space=pl.ANY),
            scratch_shapes=[
                pltpu.VMEM((2, SORT_ROWS, d), F32),
                pltpu.VMEM((EXPERT_ROWS // 2, d), F32),
                pltpu.SemaphoreType.DMA((2,)),
                pltpu.SemaphoreType.DMA,
            ],
        ),
        compiler_params=pltpu.CompilerParams(
            dimension_semantics=("arbitrary",), vmem_limit_bytes=VMEM_LIMIT),
        name="dispatch",
    )(tabs["tile_off"], tabs["glob_off"], tabs["seg"], tabs["tail_start"], tabs["tail_len"],
      tabs["n_used"], xt, rt)


def _ffn_kernel(be_ref, nu_ref, x_ref, wg_ref, wu_ref, wd_ref, o_ref, wg_b, wu_b, wd_b):
    i = pl.program_id(0)
    used = i < nu_ref[0]

    @pl.when(jnp.logical_and(used, jnp.logical_or(i == 0, be_ref[i] != be_ref[jnp.maximum(i - 1, 0)])))
    def _():
        wg_b[...] = wg_ref[...].astype(BF16)
        wu_b[...] = wu_ref[...].astype(BF16)
        wd_b[...] = wd_ref[...].astype(BF16)

    @pl.when(used)
    def _():
        xb = x_ref[...].astype(BF16)
        g = jnp.dot(xb, wg_b[...], preferred_element_type=F32)
        u = jnp.dot(xb, wu_b[...], preferred_element_type=F32)
        hb = (g * jax.nn.sigmoid(g) * u).astype(BF16)
        o_ref[...] = jnp.dot(hb, wd_b[...], preferred_element_type=F32)

    @pl.when(jnp.logical_not(used))
    def _():
        o_ref[...] = jnp.zeros(o_ref.shape, F32)


def _expert_ffn(xp, block_e, n_used, w_gate, w_up, w_down, layer):
    n_rows, d = xp.shape
    d_e = w_gate.shape[3]
    n_blocks = n_rows // EXPERT_ROWS
    blk = lambda i, be, nu: (jnp.minimum(i, nu[0] - 1), 0)
    wsel = lambda i, be, nu: (layer, be[jnp.minimum(i, nu[0] - 1)], 0, 0)
    return pl.pallas_call(
        _ffn_kernel,
        out_shape=jax.ShapeDtypeStruct((n_rows, d), F32),
        grid_spec=pltpu.PrefetchScalarGridSpec(
            num_scalar_prefetch=2,
            grid=(n_blocks,),
            in_specs=[
                pl.BlockSpec((EXPERT_ROWS, d), blk),
                pl.BlockSpec((None, None, d, d_e), wsel),
                pl.BlockSpec((None, None, d, d_e), wsel),
                pl.BlockSpec((None, None, d_e, d), wsel),
            ],
            out_specs=pl.BlockSpec((EXPERT_ROWS, d), lambda i, be, nu: (i, 0)),
            scratch_shapes=[
                pltpu.VMEM((d, d_e), BF16),
                pltpu.VMEM((d, d_e), BF16),
                pltpu.VMEM((d_e, d), BF16),
            ],
        ),
        compiler_params=pltpu.CompilerParams(
            dimension_semantics=("arbitrary",), vmem_limit_bytes=VMEM_LIMIT),
        name="expert_ffn",
    )(block_e, n_used, xp, w_gate, w_up, w_down)


def _combine_kernel(toff_ref, goff_ref, seg_ref, x_ref, rn_ref, yp_hbm, lng_ref, lnb_ref, o_ref,
                    ybuf, sem):
    i = pl.program_id(0)
    n = pl.num_programs(0)
    tile = x_ref.shape[0]
    slot = i % 2

    def segment_copies(t, sl, wait):
        def per_expert(e, c):
            idx = t * N_EXPERTS + e
            t0 = toff_ref[idx]
            g0 = goff_ref[idx]

            def piece(off, size):
                cp = pltpu.make_async_copy(
                    yp_hbm.at[pl.ds(pl.multiple_of(g0 + off, SUBLANES), size), :],
                    ybuf.at[sl, pl.ds(pl.multiple_of(t0 + off, SUBLANES), size), :], sem.at[sl])
                cp.wait() if wait else cp.start()

            _for_each_piece(seg_ref[idx], tile, piece)
            return c

        lax.fori_loop(0, N_EXPERTS, per_expert, 0)

    @pl.when(i == 0)
    def _():
        ybuf[...] = jnp.zeros(ybuf.shape, F32)
        segment_copies(0, 0, False)

    @pl.when(i + 1 < n)
    def _():
        segment_copies(i + 1, 1 - slot, False)

    segment_copies(i, slot, True)

    rn = rn_ref[...]
    full = lambda col: jnp.broadcast_to(col, (tile, LANES))
    w1, w2 = full(rn[:, 2:3]), full(rn[:, 3:4])
    pos1, pos2 = full(rn[:, 4:5]), full(rn[:, 5:6])
    lane = lax.broadcasted_iota(I32, (tile, LANES), 1).astype(F32)

    def pick_lanes(col0):
        cols = lane + float(col0)
        return jnp.where(cols == pos1, w1, 0.0) + jnp.where(cols == pos2, w2, 0.0)

    ffn = jnp.zeros(o_ref.shape, F32)
    for c in range(SORT_ROWS // SORT_CHUNK):
        pick = jnp.concatenate([pick_lanes(c * SORT_CHUNK + l0) for l0 in range(0, SORT_CHUNK, LANES)],
                               axis=1)
        ffn = ffn + jnp.dot(pick.astype(BF16),
                            ybuf[slot, c * SORT_CHUNK:(c + 1) * SORT_CHUNK, :].astype(BF16),
                            preferred_element_type=F32)
    o_ref[...] = _layer_norm(ALPHA * x_ref[...] + ffn, lng_ref[...], lnb_ref[...])


def _combine(xt, rn, yp, tabs, ln_g, ln_b):
    n_tok, d = xt.shape
    tile = MOE_TILE
    return pl.pallas_call(
        _combine_kernel,
        out_shape=jax.ShapeDtypeStruct((n_tok, d), F32),
        grid_spec=pltpu.PrefetchScalarGridSpec(
            num_scalar_prefetch=3,
            grid=(n_tok // tile,),
            in_specs=[
                pl.BlockSpec((tile, d), lambda i, *_: (i, 0)),
                pl.BlockSpec((tile, LANES), lambda i, *_: (i, 0)),
                pl.BlockSpec(memory_space=pl.ANY),
                pl.BlockSpec((1, d), lambda i, *_: (0, 0)),
                pl.BlockSpec((1, d), lambda i, *_: (0, 0)),
            ],
            out_specs=pl.BlockSpec((tile, d), lambda i, *_: (i, 0)),
            scratch_shapes=[
                pltpu.VMEM((2, SORT_ROWS, d), F32),
                pltpu.SemaphoreType.DMA((2,)),
            ],
        ),
        compiler_params=pltpu.CompilerParams(
            dimension_semantics=("arbitrary",), vmem_limit_bytes=VMEM_LIMIT),
        name="combine",
    )(tabs["tile_off"], tabs["glob_off"], tabs["seg"], xt, rn, yp,
      ln_g.reshape(1, d), ln_b.reshape(1, d))


def _moe_layer(x, rg_w, rg_b, re_w, re_b, w_gate, w_up, w_down, layer, ln_g, ln_b):
    bt, s_len, d = x.shape
    n_tok = bt * s_len
    xt = x.reshape(n_tok, d)
    rt, rn, cnt = _router(xt, rg_w, rg_b, re_w, re_b)
    tabs = _routing_tables(cnt, n_tok)
    xp = _dispatch(xt, rt, tabs)
    yp = _expert_ffn(xp, tabs["block_e"], tabs["n_used"], w_gate, w_up, w_down, layer)
    out = _combine(xt, rn, yp, tabs, ln_g, ln_b)
    return out.reshape(bt, s_len, d)


def kernel(x, ab_w_in, ab_b_in, a_dw, a_dw_b, a_ln_g, a_ln_b, b_dw, ab_w_out, cd_w_in, cd_b_in, c_ln_g, c_ln_b, c_ws, c_ws_b, d_rel_bias, cd_w_out, mix_ln_g, mix_ln_b, moe_rg_w, moe_rg_b, moe_re_w, moe_re_b, moe_w_gate, moe_w_up, moe_w_down, ffn_ln_g, ffn_ln_b):
    for layer in range(DEPTH):
        i = layer // 2
        if layer % 2 == 0:
            x = _ab_mixer(x, ab_w_in[i], ab_b_in[i], a_dw[i], a_dw_b[i], a_ln_g[i], a_ln_b[i],
                          b_dw[i], ab_w_out[i], mix_ln_g[layer], mix_ln_b[layer])
        else:
            x = _cd_mixer(x, cd_w_in[i], cd_b_in[i], c_ln_g[i], c_ln_b[i], c_ws[i], c_ws_b[i],
                          d_rel_bias[i], cd_w_out[i], mix_ln_g[layer], mix_ln_b[layer])
        x = _moe_layer(x, moe_rg_w[layer], moe_rg_b[layer], moe_re_w[layer], moe_re_b[layer],
                       moe_w_gate, moe_w_up, moe_w_down, layer,
                       ffn_ln_g[layer], ffn_ln_b[layer])
    return x
```

```python
import functools

import jax
import jax.numpy as jnp
from jax import lax
from jax.experimental import pallas as pl
from jax.experimental.pallas import tpu as pltpu

F32 = jnp.float32
BF16 = jnp.bfloat16
I32 = jnp.int32

DEPTH = 2
CHUNK = 64
LEFT_CHUNKS = 8
HIST = LEFT_CHUNKS * CHUNK
CONV_A = 31
CONV_B = 3
HEADS_C = 8
SGU_BLOCK = 128
HEADS_D = 8
MAX_REL = 256
N_GROUPS = 4
EXPERTS_PER_GROUP = 8
N_EXPERTS = N_GROUPS * EXPERTS_PER_GROUP
TOP_K = 2
ALPHA = (2 * DEPTH) ** 0.25
LN_EPS = 1e-5
NEG_INF = -1e30
LOG2_E = 1.4426950408889634

LANES = 128
SUBLANES = 8
VMEM_LIMIT = 56 * 1024 * 1024

SEQ_TILE = 512
CONV_ROWS = 32
HALO = 32
Q_ROWS = 256
K_ROWS = Q_ROWS + HIST
MOE_TILE = 512
ROUTER_TILE = MOE_TILE
EXPERT_ROWS = 512
SORT_ROWS = MOE_TILE * TOP_K + N_EXPERTS * SUBLANES
SORT_CHUNK = 256
ROUTER_LANE0 = N_GROUPS
_B_SHIFTS = sorted({(HALO - (CONV_B - 1) + j) % SUBLANES for j in range(CONV_B)} - {0})


def _layer_norm(x, g, b):
    mu = jnp.mean(x, axis=-1, keepdims=True)
    xc = x - mu
    var = jnp.mean(xc * xc, axis=-1, keepdims=True)
    return xc * lax.rsqrt(var + LN_EPS) * g + b


def _const_spec(shape):
    nd = len(shape)
    return pl.BlockSpec(shape, lambda *_: (0,) * nd, pipeline_mode=pl.Buffered(1))


def _ab_mixer_kernel(x_ref, win_ref, bin_ref, adw_ref, adwb_ref, alng_ref, alnb_ref,
                     bdw_ref, wout_ref, lng_ref, lnb_ref, o_ref,
                     abuf, cbuf, gbuf, mbuf, asht, csht, *, d_a, d_b):
    ts = x_ref.shape[0]

    @pl.when(pl.program_id(1) == 0)
    def _():
        abuf[0:HALO, :] = jnp.zeros((HALO, d_a), F32)
        cbuf[0:HALO, :] = jnp.zeros((HALO, d_b), F32)

    x = x_ref[...]
    h = jnp.dot(x.astype(BF16), win_ref[...], preferred_element_type=F32) + bin_ref[...]
    abuf[HALO:HALO + ts, :] = h[:, 0:d_a] * jax.nn.sigmoid(h[:, d_a:2 * d_a])
    gbuf[...] = h[:, 2 * d_a:2 * d_a + d_b]
    cbuf[HALO:HALO + ts, :] = h[:, 2 * d_a + d_b:2 * d_a + 2 * d_b] * h[:, 2 * d_a + 2 * d_b:]

    n_sh = HALO + ts - SUBLANES
    for sh in range(1, SUBLANES):
        asht[sh, 0:n_sh, :] = abuf[sh:sh + n_sh, :]
    for sh in _B_SHIFTS:
        csht[_B_SHIFTS.index(sh), 0:n_sh, :] = cbuf[sh:sh + n_sh, :]

    def tap(buf, sht, slot_of, r0, off):
        base = r0 + (off // SUBLANES) * SUBLANES
        if off % SUBLANES == 0:
            return buf[pl.ds(base, CONV_ROWS), :]
        return sht[slot_of(off % SUBLANES), pl.ds(base, CONV_ROWS), :]

    def conv_chunk(i, carry):
        r0 = pl.multiple_of(i * CONV_ROWS, CONV_ROWS)
        acc = jnp.broadcast_to(adwb_ref[...], (CONV_ROWS, d_a))
        for j in range(CONV_A):
            acc = acc + adw_ref[j:j + 1, :] * tap(abuf, asht, lambda sh: sh, r0, HALO - (CONV_A - 1) + j)
        a = _layer_norm(acc, alng_ref[...], alnb_ref[...])
        mbuf[pl.ds(r0, CONV_ROWS), 0:d_a] = (a * jax.nn.sigmoid(a)).astype(BF16)
        s = jnp.zeros((CONV_ROWS, d_b), F32)
        for j in range(CONV_B):
            s = s + bdw_ref[j:j + 1, :] * tap(cbuf, csht, _B_SHIFTS.index, r0, HALO - (CONV_B - 1) + j)
        mbuf[pl.ds(r0, CONV_ROWS), d_a:d_a + d_b] = (gbuf[pl.ds(r0, CONV_ROWS), :] * s).astype(BF16)
        return carry

    lax.fori_loop(0, ts // CONV_ROWS, conv_chunk, 0, unroll=2)

    abuf[0:HALO, :] = abuf[ts:ts + HALO, :]
    cbuf[0:HALO, :] = cbuf[ts:ts + HALO, :]

    mix = jnp.dot(mbuf[...], wout_ref[...], preferred_element_type=F32)
    o_ref[...] = _layer_norm(ALPHA * x + mix, lng_ref[...], lnb_ref[...])


def _ab_mixer(x, w_in, b_in, a_dw, a_dw_b, a_ln_g, a_ln_b, b_dw, w_out, ln_g, ln_b):
    bt, s_len, d = x.shape
    d_a = a_dw.shape[1]
    d_b = b_dw.shape[1]
    w_in_w = w_in.shape[1]
    ts = SEQ_TILE
    row = lambda v: v.reshape(1, -1)
    kern = functools.partial(_ab_mixer_kernel, d_a=d_a, d_b=d_b)
    return pl.pallas_call(
        kern,
        out_shape=jax.ShapeDtypeStruct((bt, s_len, d), F32),
        grid=(bt, s_len // ts),
        in_specs=[
            pl.BlockSpec((None, ts, d), lambda b, t: (b, t, 0)),
            _const_spec((d, w_in_w)),
            _const_spec((1, w_in_w)),
            _const_spec((CONV_A, d_a)),
            _const_spec((1, d_a)),
            _const_spec((1, d_a)),
            _const_spec((1, d_a)),
            _const_spec((CONV_B, d_b)),
            _const_spec((d_a + d_b, d)),
            _const_spec((1, d)),
            _const_spec((1, d)),
        ],
        out_specs=pl.BlockSpec((None, ts, d), lambda b, t: (b, t, 0)),
        scratch_shapes=[
            pltpu.VMEM((HALO + ts, d_a), F32),
            pltpu.VMEM((HALO + ts, d_b), F32),
            pltpu.VMEM((ts, d_b), F32),
            pltpu.VMEM((ts, d_a + d_b), BF16),
            pltpu.VMEM((SUBLANES, HALO + ts, d_a), F32),
            pltpu.VMEM((len(_B_SHIFTS), HALO + ts, d_b), F32),
        ],
        compiler_params=pltpu.CompilerParams(
            dimension_semantics=("arbitrary", "arbitrary"), vmem_limit_bytes=VMEM_LIMIT),
        name="ab_mixer",
    )(x, w_in.astype(BF16), row(b_in), a_dw, row(a_dw_b), row(a_ln_g), row(a_ln_b),
      b_dw, w_out.astype(BF16), row(ln_g), row(ln_b))


def _cd_mixer_kernel(x_ref, win_ref, bin_ref, clng_ref, clnb_ref, wsc_ref, wsb_ref,
                     bias_ref, wout_ref, lng_ref, lnb_ref, o_ref,
                     qbuf, kbuf, vbuf, wsm, vbd, mbuf, *, d_c, d_d):
    ts = x_ref.shape[0]
    t = pl.program_id(1)
    hc = d_c // HEADS_C
    hd = d_d // HEADS_D

    @pl.when(t == 0)
    def _():
        kbuf[0:HIST, :] = jnp.zeros((HIST, d_d), BF16)
        vbuf[0:HIST, :] = jnp.zeros((HIST, d_d), BF16)
        r = lax.broadcasted_iota(I32, (SGU_BLOCK, HEADS_C * SGU_BLOCK), 0)
        c = lax.broadcasted_iota(I32, (SGU_BLOCK, HEADS_C * SGU_BLOCK), 1) % SGU_BLOCK
        wsm[...] = jnp.where(c // CHUNK <= r // CHUNK, wsc_ref[...], 0.0).astype(BF16)

    x = x_ref[...]
    h = jnp.dot(x.astype(BF16), win_ref[...], preferred_element_type=F32) + bin_ref[...]
    u = h[:, 0:d_c]
    vn = _layer_norm(h[:, d_c:2 * d_c], clng_ref[...], clnb_ref[...])
    qbuf[...] = (h[:, 2 * d_c:2 * d_c + d_d] * (hd ** -0.5 * LOG2_E)).astype(BF16)
    kbuf[HIST:HIST + ts, :] = h[:, 2 * d_c + d_d:2 * d_c + 2 * d_d].astype(BF16)
    vbuf[HIST:HIST + ts, :] = h[:, 2 * d_c + 2 * d_d:].astype(BF16)

    lane_head = lax.broadcasted_iota(I32, (SGU_BLOCK, d_c), 1) // hc
    for nb in range(ts // SGU_BLOCK):
        v_blk = vn[nb * SGU_BLOCK:(nb + 1) * SGU_BLOCK, :]
        for hh in range(HEADS_C):
            vbd[hh * SGU_BLOCK:(hh + 1) * SGU_BLOCK, :] = jnp.where(
                lane_head == hh, v_blk, 0.0).astype(BF16)
        gate = jnp.dot(wsm[...], vbd[...], preferred_element_type=F32) + wsb_ref[...]
        mbuf[nb * SGU_BLOCK:(nb + 1) * SGU_BLOCK, 0:d_c] = (
            u[nb * SGU_BLOCK:(nb + 1) * SGU_BLOCK, :] * gate).astype(BF16)

    for qb in range(ts // Q_ROWS):
        q0 = qb * Q_ROWS
        variant = jnp.where(t == 0, 1 + qb, 0)
        for hh in range(HEADS_D):
            q = qbuf[q0:q0 + Q_ROWS, hh * hd:(hh + 1) * hd]
            k = kbuf[q0:q0 + K_ROWS, hh * hd:(hh + 1) * hd]
            v = vbuf[q0:q0 + K_ROWS, hh * hd:(hh + 1) * hd]
            s = lax.dot_general(q, k, (((1,), (1,)), ((), ())), preferred_element_type=F32)
            s = s + bias_ref[variant, hh]
            p = jnp.exp2(s - jnp.max(s, axis=-1, keepdims=True))
            l = jnp.sum(p, axis=-1, keepdims=True)
            o = jnp.dot(p.astype(BF16), v, preferred_element_type=F32)
            mbuf[q0:q0 + Q_ROWS, d_c + hh * hd:d_c + (hh + 1) * hd] = (o / l).astype(BF16)

    kbuf[0:HIST, :] = kbuf[ts:ts + HIST, :]
    vbuf[0:HIST, :] = vbuf[ts:ts + HIST, :]

    mix = jnp.dot(mbuf[...], wout_ref[...], preferred_element_type=F32)
    o_ref[...] = _layer_norm(ALPHA * x + mix, lng_ref[...], lnb_ref[...])


def _attention_bias(d_rel_bias):
    heads = d_rel_bias.shape[0]
    i = jnp.arange(Q_ROWS)[:, None]
    j = jnp.arange(K_ROWS)[None, :]
    jb = j - (i // CHUNK) * CHUNK
    in_band = jnp.logical_and(jb >= 0, jb < (LEFT_CHUNKS + 1) * CHUNK)
    n_diag = Q_ROWS + K_ROWS - 1
    m = jnp.arange(n_diag)
    diag = d_rel_bias[:, jnp.clip(Q_ROWS - 1 + HIST - m, -MAX_REL, MAX_REL) + MAX_REL].astype(F32)
    period = jnp.pad(diag, ((0, 0), (0, 1)))
    flat = jnp.tile(period, (1, Q_ROWS))[:, :Q_ROWS * n_diag]
    rel = flat.reshape(heads, Q_ROWS, n_diag)[:, :, Q_ROWS - 1:Q_ROWS - 1 + K_ROWS]
    rel = jnp.where(in_band[None], rel * LOG2_E, NEG_INF)
    variants = [rel] + [jnp.where(j[None] + qb * Q_ROWS >= HIST, rel, NEG_INF)
                        for qb in range(SEQ_TILE // Q_ROWS)]
    return jnp.stack(variants)


def _cd_mixer(x, w_in, b_in, c_ln_g, c_ln_b, c_ws, c_ws_b, d_rel_bias, w_out, ln_g, ln_b):
    bt, s_len, d = x.shape
    d_c = c_ln_g.shape[0]
    d_d = w_out.shape[0] - d_c
    w_in_w = w_in.shape[1]
    ts = SEQ_TILE
    assert ts == HIST and ts % Q_ROWS == 0 and ts % SGU_BLOCK == 0
    row = lambda v: v.reshape(1, -1)
    ws_cat = jnp.transpose(c_ws, (1, 0, 2)).reshape(SGU_BLOCK, HEADS_C * SGU_BLOCK)
    bias_full = jnp.repeat(c_ws_b.T, d_c // HEADS_C, axis=1)
    att_bias = _attention_bias(d_rel_bias)
    kern = functools.partial(_cd_mixer_kernel, d_c=d_c, d_d=d_d)
    return pl.pallas_call(
        kern,
        out_shape=jax.ShapeDtypeStruct((bt, s_len, d), F32),
        grid=(bt, s_len // ts),
        in_specs=[
            pl.BlockSpec((None, ts, d), lambda b, t: (b, t, 0)),
            _const_spec((d, w_in_w)),
            _const_spec((1, w_in_w)),
            _const_spec((1, d_c)),
            _const_spec((1, d_c)),
            _const_spec((SGU_BLOCK, HEADS_C * SGU_BLOCK)),
            _const_spec((SGU_BLOCK, d_c)),
            _const_spec((1 + ts // Q_ROWS, HEADS_D, Q_ROWS, K_ROWS)),
            _const_spec((d_c + d_d, d)),
            _const_spec((1, d)),
            _const_spec((1, d)),
        ],
        out_specs=pl.BlockSpec((None, ts, d), lambda b, t: (b, t, 0)),
        scratch_shapes=[
            pltpu.VMEM((ts, d_d), BF16),
            pltpu.VMEM((HIST + ts, d_d), BF16),
            pltpu.VMEM((HIST + ts, d_d), BF16),
            pltpu.VMEM((SGU_BLOCK, HEADS_C * SGU_BLOCK), BF16),
            pltpu.VMEM((HEADS_C * SGU_BLOCK, d_c), BF16),
            pltpu.VMEM((ts, d_c + d_d), BF16),
        ],
        compiler_params=pltpu.CompilerParams(
            dimension_semantics=("arbitrary", "arbitrary"), vmem_limit_bytes=VMEM_LIMIT),
        name="cd_mixer",
    )(x, w_in.astype(BF16), row(b_in), row(c_ln_g), row(c_ln_b), ws_cat, bias_full,
      att_bias, w_out.astype(BF16), row(ln_g), row(ln_b))


def _router_kernel(x_ref, wr_ref, br_ref, rt_ref, rn_ref, cnt_ref):
    tr = x_ref.shape[0]
    x = x_ref[...]
    x_hi = x.astype(BF16)
    x_lo = (x - x_hi.astype(F32)).astype(BF16)
    p_hi = jnp.dot(x_hi, wr_ref[...], preferred_element_type=F32)
    p_lo = jnp.dot(x_lo, wr_ref[...], preferred_element_type=F32)
    logits = p_hi + pltpu.roll(p_hi, LANES // 2, axis=1) + p_lo + br_ref[...]
    lane = lax.broadcasted_iota(I32, (tr, LANES), 1).astype(F32)

    def top1(vals):
        m = jnp.max(vals, axis=-1, keepdims=True)
        idx = jnp.min(jnp.where(vals == m, lane, float(LANES)), axis=-1, keepdims=True)
        return m, idx

    g_mask = lane < N_GROUPS
    g_max, g_idx = top1(jnp.where(g_mask, logits, NEG_INF))
    g_top = 1.0 / jnp.sum(jnp.where(g_mask, jnp.exp(logits - g_max), 0.0), axis=-1, keepdims=True)

    e_lo = ROUTER_LANE0 + EXPERTS_PER_GROUP * g_idx
    e_vals = jnp.where(jnp.logical_and(lane >= e_lo, lane < e_lo + EXPERTS_PER_GROUP), logits, NEG_INF)
    m1, i1 = top1(e_vals)
    m2, i2 = top1(jnp.where(lane == i1, NEG_INF, e_vals))
    ratio = jnp.exp(m2 - m1)
    w1 = g_top / (1.0 + ratio)
    w2 = g_top * ratio / (1.0 + ratio)

    chosen = jnp.logical_or(lane == i1, lane == i2)
    onehot = jnp.where(chosen, 1.0, 0.0)
    rr = lax.broadcasted_iota(I32, (tr, tr), 0)
    cc = lax.broadcasted_iota(I32, (tr, tr), 1)
    before = jnp.where(cc < rr, 1.0, 0.0).astype(BF16)
    seen = jnp.dot(before, onehot.astype(BF16), preferred_element_type=F32)
    counts = jnp.broadcast_to(jnp.sum(onehot, axis=0, keepdims=True), (SUBLANES, LANES))
    cnt_ref[...] = counts

    groups = (counts.astype(I32) + (SUBLANES - 1)) // SUBLANES
    lr = lax.broadcasted_iota(I32, (LANES, LANES), 0)
    lc = lax.broadcasted_iota(I32, (LANES, LANES), 1)
    lower = jnp.where(lr < lc, 1.0, 0.0).astype(BF16)
    seg_start = SUBLANES * jnp.dot(groups.astype(F32).astype(BF16), lower, preferred_element_type=F32)
    row = seen + seg_start[0:1, :]
    pos1 = jnp.sum(jnp.where(lane == i1, row, 0.0), axis=-1, keepdims=True)
    pos2 = jnp.sum(jnp.where(lane == i2, row, 0.0), axis=-1, keepdims=True)

    fields = (i1 - ROUTER_LANE0, i2 - ROUTER_LANE0, w1, w2, pos1, pos2)
    res = jnp.zeros((tr, LANES), F32)
    for f, val in enumerate(fields):
        res = jnp.where(lane == f, val, res)
    rn_ref[...] = res
    rt_ref[...] = res.T[0:SUBLANES, :]


def _router(xt, rg_w, rg_b, re_w, re_b):
    n_tok, d = xt.shape
    tr = ROUTER_TILE
    w_all = jnp.concatenate([rg_w, jnp.transpose(re_w, (1, 0, 2)).reshape(d, N_EXPERTS)], axis=1)
    b_all = jnp.concatenate([rg_b, re_b.reshape(N_EXPERTS)])
    half = LANES // 2
    pad = half - w_all.shape[1]
    w_hi = w_all.astype(BF16)
    w_lo = (w_all - w_hi.astype(F32)).astype(BF16)
    w_all = jnp.concatenate([jnp.pad(w_hi, ((0, 0), (0, pad))), jnp.pad(w_lo, ((0, 0), (0, pad)))], axis=1)
    b_all = jnp.pad(b_all, (0, LANES - b_all.shape[0])).reshape(1, LANES)
    return pl.pallas_call(
        _router_kernel,
        out_shape=(
            jax.ShapeDtypeStruct((SUBLANES, n_tok), F32),
            jax.ShapeDtypeStruct((n_tok, LANES), F32),
            jax.ShapeDtypeStruct((n_tok // tr, SUBLANES, LANES), F32),
        ),
        grid=(n_tok // tr,),
        in_specs=[
            pl.BlockSpec((tr, d), lambda i: (i, 0)),
            _const_spec((d, LANES)),
            _const_spec((1, LANES)),
        ],
        out_specs=(
            pl.BlockSpec((SUBLANES, tr), lambda i: (0, i)),
            pl.BlockSpec((tr, LANES), lambda i: (i, 0)),
            pl.BlockSpec((None, SUBLANES, LANES), lambda i: (i, 0, 0)),
        ),
        compiler_params=pltpu.CompilerParams(
            dimension_semantics=("arbitrary",), vmem_limit_bytes=VMEM_LIMIT),
        name="router",
    )(xt, w_all, b_all)


def _round_up(v, m):
    return ((v + m - 1) // m) * m


def _num_expert_blocks(n_tok):
    n_tiles = n_tok // MOE_TILE
    rows = n_tok * TOP_K + n_tiles * N_EXPERTS * (SUBLANES - 1)
    return -(-rows // EXPERT_ROWS) + N_EXPERTS


def _routing_tables(cnt, n_tok):
    counts = cnt[:, 0, ROUTER_LANE0:ROUTER_LANE0 + N_EXPERTS].astype(I32)
    seg = _round_up(counts, SUBLANES)
    tile_off = jnp.cumsum(seg, axis=1) - seg
    tot = jnp.sum(seg, axis=0)
    reg = _round_up(tot, EXPERT_ROWS)
    reg_end = jnp.cumsum(reg)
    reg_start = reg_end - reg
    glob_off = reg_start[None, :] + jnp.cumsum(seg, axis=0) - seg
    n_blocks = _num_expert_blocks(n_tok)
    blk_start = jnp.arange(n_blocks, dtype=I32) * EXPERT_ROWS
    block_e = jnp.sum((reg_end[None, :] <= blk_start[:, None]).astype(I32), axis=1)
    block_e = jnp.minimum(block_e, N_EXPERTS - 1).astype(I32)
    n_used = (reg_end[-1] // EXPERT_ROWS).astype(I32).reshape(1)
    flat = lambda a: a.reshape(-1).astype(I32)
    return dict(tile_off=flat(tile_off), glob_off=flat(glob_off), seg=flat(seg),
                tail_start=flat(reg_start + tot), tail_len=flat(reg - tot),
                block_e=block_e, n_used=n_used)


def _piece_sizes(max_len):
    return [SUBLANES << b for b in range((max_len // SUBLANES).bit_length())]


def _for_each_piece(length, max_len, fn):
    off = 0
    for size in reversed(_piece_sizes(max_len)):
        take = (length & size) != 0

        @pl.when(take)
        def _(off=off, size=size):
            fn(off, size)

        off = off + jnp.where(take, size, 0)


def _dispatch_kernel(toff_ref, goff_ref, seg_ref, tstart_ref, tlen_ref, nu_ref, x_ref, rt_ref,
                     xp_hbm, obuf, zbuf, sem, zsem):
    i = pl.program_id(0)
    n = pl.num_programs(0)
    tile = x_ref.shape[0]
    slot = i % 2

    def wait_segments(t, sl):
        last = t * N_EXPERTS + N_EXPERTS - 1

        def piece(off, size):
            pltpu.make_async_copy(obuf.at[sl, pl.ds(0, size), :], xp_hbm.at[pl.ds(0, size), :],
                                  sem.at[sl]).wait()

        _for_each_piece(toff_ref[last] + seg_ref[last], SORT_ROWS, piece)

    def start_segments(t, sl, live, inline):
        def per_expert(e, c):
            idx = t * N_EXPERTS + e
            t0 = toff_ref[idx]
            g0 = goff_ref[idx]

            def piece(off, size):
                pltpu.make_async_copy(
                    obuf.at[sl, pl.ds(pl.multiple_of(t0 + off, SUBLANES), size), :],
                    xp_hbm.at[pl.ds(pl.multiple_of(g0 + off, SUBLANES), size), :], sem.at[sl]).start()

            _for_each_piece(jnp.where(live, seg_ref[idx], 0), tile, piece)
            return c

        if inline:
            for e in range(N_EXPERTS):
                per_expert(e, 0)
        else:
            lax.fori_loop(0, N_EXPERTS, per_expert, 0)

    @pl.when(i >= 2)
    def _():
        wait_segments(i - 2, slot)

    start_segments(jnp.maximum(i - 1, 0), 1 - slot, i >= 1, inline=True)

    pos1 = rt_ref[4:5, :].astype(I32)
    pos2 = rt_ref[5:6, :].astype(I32)
    xb = x_ref[...].astype(BF16)
    for c in range(SORT_ROWS // SORT_CHUNK):
        rows = lax.broadcasted_iota(I32, (SORT_CHUNK, tile), 0) + c * SORT_CHUNK
        hit = jnp.logical_or(rows == pos1, rows == pos2)
        obuf[slot, c * SORT_CHUNK:(c + 1) * SORT_CHUNK, :] = jnp.dot(
            jnp.where(hit, 1.0, 0.0).astype(BF16), xb, preferred_element_type=F32)

    def zero_fill(wait):
        def per_expert(e, c):
            def piece(off, size):
                cp = pltpu.make_async_copy(
                    zbuf.at[pl.ds(0, size), :],
                    xp_hbm.at[pl.ds(pl.multiple_of(tstart_ref[e] + off, SUBLANES), size), :], zsem)
                cp.wait() if wait else cp.start()

            _for_each_piece(tlen_ref[e], EXPERT_ROWS - SUBLANES, piece)
            return c

        lax.fori_loop(0, N_EXPERTS, per_expert, 0)

        def per_half_block(h, c):
            cp = pltpu.make_async_copy(
                zbuf, xp_hbm.at[pl.ds(pl.multiple_of(h * zbuf.shape[0], zbuf.shape[0]), zbuf.shape[0]), :], zsem)
            cp.wait() if wait else cp.start()
            return c

        halves = EXPERT_ROWS // zbuf.shape[0]
        lax.fori_loop(nu_ref[0] * halves, (xp_hbm.shape[0] // EXPERT_ROWS) * halves, per_half_block, 0)

    @pl.when(i == 0)
    def _():
        zbuf[...] = jnp.zeros(zbuf.shape, zbuf.dtype)
        zero_fill(False)

    @pl.when(i == n - 1)
    def _():
        start_segments(i, slot, True, inline=False)

        @pl.when(i >= 1)
        def _():
            wait_segments(i - 1, 1 - slot)

        wait_segments(i, slot)
        zero_fill(True)


def _dispatch(xt, rt, tabs):
    n_tok, d = xt.shape
    tile = MOE_TILE
    n_rows = _num_expert_blocks(n_tok) * EXPERT_ROWS
    return pl.pallas_call(
        _dispatch_kernel,
        out_shape=jax.ShapeDtypeStruct((n_rows, d), F32),
        grid_spec=pltpu.PrefetchScalarGridSpec(
            num_scalar_prefetch=6,
            grid=(n_tok // tile,),
            in_specs=[
                pl.BlockSpec((tile, d), lambda i, *_: (i, 0)),
                pl.BlockSpec((SUBLANES, tile), lambda i, *_: (0, i)),
            ],
            out_specs=pl.BlockSpec(memory_space=pl.ANY),
            scratch_shapes=[
                pltpu.VMEM((2, SORT_ROWS, d), F32),
                pltpu.VMEM((EXPERT_ROWS // 2, d), F32),
                pltpu.SemaphoreType.DMA((2,)),
                pltpu.SemaphoreType.DMA,
            ],
        ),
        compiler_params=pltpu.CompilerParams(
            dimension_semantics=("arbitrary",), vmem_limit_bytes=VMEM_LIMIT),
        name="dispatch",
    )(tabs["tile_off"], tabs["glob_off"], tabs["seg"], tabs["tail_start"], tabs["tail_len"],
      tabs["n_used"], xt, rt)


def _ffn_kernel(be_ref, nu_ref, x_ref, wg_ref, wu_ref, wd_ref, o_ref, wg_b, wu_b, wd_b):
    i = pl.program_id(0)
    used = i < nu_ref[0]

    @pl.when(jnp.logical_and(used, jnp.logical_or(i == 0, be_ref[i] != be_ref[jnp.maximum(i - 1, 0)])))
    def _():
        wg_b[...] = wg_ref[...].astype(BF16)
        wu_b[...] = wu_ref[...].astype(BF16)
        wd_b[...] = wd_ref[...].astype(BF16)

    @pl.when(used)
    def _():
        xb = x_ref[...].astype(BF16)
        g = jnp.dot(xb, wg_b[...], preferred_element_type=F32)
        u = jnp.dot(xb, wu_b[...], preferred_element_type=F32)
        hb = (g * jax.nn.sigmoid(g) * u).astype(BF16)
        o_ref[...] = jnp.dot(hb, wd_b[...], preferred_element_type=F32)

    @pl.when(jnp.logical_not(used))
    def _():
        o_ref[...] = jnp.zeros(o_ref.shape, o_ref.dtype)


def _expert_ffn(xp, block_e, n_used, w_gate, w_up, w_down, layer):
    n_rows, d = xp.shape
    d_e = w_gate.shape[3]
    n_blocks = n_rows // EXPERT_ROWS
    blk = lambda i, be, nu: (jnp.minimum(i, nu[0] - 1), 0)
    wsel = lambda i, be, nu: (layer, be[jnp.minimum(i, nu[0] - 1)], 0, 0)
    return pl.pallas_call(
        _ffn_kernel,
        out_shape=jax.ShapeDtypeStruct((n_rows, d), F32),
        grid_spec=pltpu.PrefetchScalarGridSpec(
            num_scalar_prefetch=2,
            grid=(n_blocks,),
            in_specs=[
                pl.BlockSpec((EXPERT_ROWS, d), blk),
                pl.BlockSpec((None, None, d, d_e), wsel),
                pl.BlockSpec((None, None, d, d_e), wsel),
                pl.BlockSpec((None, None, d_e, d), wsel),
            ],
            out_specs=pl.BlockSpec((EXPERT_ROWS, d), lambda i, be, nu: (i, 0)),
            scratch_shapes=[
                pltpu.VMEM((d, d_e), BF16),
                pltpu.VMEM((d, d_e), BF16),
                pltpu.VMEM((d_e, d), BF16),
            ],
        ),
        compiler_params=pltpu.CompilerParams(
            dimension_semantics=("arbitrary",), vmem_limit_bytes=VMEM_LIMIT),
        name="expert_ffn",
    )(block_e, n_used, xp, w_gate, w_up, w_down)


def _combine_kernel(toff_ref, goff_ref, seg_ref, x_ref, rn_ref, yp_hbm, lng_ref, lnb_ref, o_ref,
                    ybuf, sem):
    i = pl.program_id(0)
    n = pl.num_programs(0)
    tile = x_ref.shape[0]
    slot = i % 2

    def wait_segments(t, sl):
        last = t * N_EXPERTS + N_EXPERTS - 1

        def piece(off, size):
            pltpu.make_async_copy(yp_hbm.at[pl.ds(0, size), :], ybuf.at[sl, pl.ds(0, size), :],
                                  sem.at[sl]).wait()

        _for_each_piece(toff_ref[last] + seg_ref[last], SORT_ROWS, piece)

    def start_segments(t, sl, live, inline):
        def per_expert(e, c):
            idx = t * N_EXPERTS + e
            t0 = toff_ref[idx]
            g0 = goff_ref[idx]

            def piece(off, size):
                pltpu.make_async_copy(
                    yp_hbm.at[pl.ds(pl.multiple_of(g0 + off, SUBLANES), size), :],
                    ybuf.at[sl, pl.ds(pl.multiple_of(t0 + off, SUBLANES), size), :], sem.at[sl]).start()

            _for_each_piece(jnp.where(live, seg_ref[idx], 0), tile, piece)
            return c

        if inline:
            for e in range(N_EXPERTS):
                per_expert(e, 0)
        else:
            lax.fori_loop(0, N_EXPERTS, per_expert, 0)

    @pl.when(i == 0)
    def _():
        ybuf[...] = jnp.zeros(ybuf.shape, ybuf.dtype)
        start_segments(0, 0, True, inline=False)

    wait_segments(i, slot)
    start_segments(jnp.minimum(i + 1, n - 1), 1 - slot, i + 1 < n, inline=True)

    rn = rn_ref[...]
    full = lambda col: jnp.broadcast_to(col, (tile, LANES))
    w1, w2 = full(rn[:, 2:3]), full(rn[:, 3:4])
    pos1, pos2 = full(rn[:, 4:5]), full(rn[:, 5:6])
    lane = lax.broadcasted_iota(I32, (tile, LANES), 1).astype(F32)

    def pick_lanes(col0):
        cols = lane + float(col0)
        return jnp.where(cols == pos1, w1, 0.0) + jnp.where(cols == pos2, w2, 0.0)

    ffn = jnp.zeros(o_ref.shape, F32)
    for c in range(SORT_ROWS // SORT_CHUNK):
        pick = jnp.concatenate([pick_lanes(c * SORT_CHUNK + l0) for l0 in range(0, SORT_CHUNK, LANES)],
                               axis=1).astype(BF16)
        ffn = ffn + jnp.dot(pick, ybuf[slot, c * SORT_CHUNK:(c + 1) * SORT_CHUNK, :].astype(BF16),
                            preferred_element_type=F32)
    o_ref[...] = _layer_norm(ALPHA * x_ref[...] + ffn, lng_ref[...], lnb_ref[...])


def _combine(xt, rn, yp, tabs, ln_g, ln_b):
    n_tok, d = xt.shape
    tile = MOE_TILE
    return pl.pallas_call(
        _combine_kernel,
        out_shape=jax.ShapeDtypeStruct((n_tok, d), F32),
        grid_spec=pltpu.PrefetchScalarGridSpec(
            num_scalar_prefetch=3,
            grid=(n_tok // tile,),
            in_specs=[
                pl.BlockSpec((tile, d), lambda i, *_: (i, 0)),
                pl.BlockSpec((tile, LANES), lambda i, *_: (i, 0)),
                pl.BlockSpec(memory_space=pl.ANY),
                pl.BlockSpec((1, d), lambda i, *_: (0, 0)),
                pl.BlockSpec((1, d), lambda i, *_: (0, 0)),
            ],
            out_specs=pl.BlockSpec((tile, d), lambda i, *_: (i, 0)),
            scratch_shapes=[
                pltpu.VMEM((2, SORT_ROWS, d), F32),
                pltpu.SemaphoreType.DMA((2,)),
            ],
        ),
        compiler_params=pltpu.CompilerParams(
            dimension_semantics=("arbitrary",), vmem_limit_bytes=VMEM_LIMIT),
        name="combine",
    )(tabs["tile_off"], tabs["glob_off"], tabs["seg"], xt, rn, yp,
      ln_g.reshape(1, d), ln_b.reshape(1, d))


def _moe_layer(x, rg_w, rg_b, re_w, re_b, w_gate, w_up, w_down, layer, ln_g, ln_b):
    bt, s_len, d = x.shape
    n_tok = bt * s_len
    xt = x.reshape(n_tok, d)
    rt, rn, cnt = _router(xt, rg_w, rg_b, re_w, re_b)
    tabs = _routing_tables(cnt, n_tok)
    xp = _dispatch(xt, rt, tabs)
    yp = _expert_ffn(xp, tabs["block_e"], tabs["n_used"], w_gate, w_up, w_down, layer)
    out = _combine(xt, rn, yp, tabs, ln_g, ln_b)
    return out.reshape(bt, s_len, d)


def kernel(x, ab_w_in, ab_b_in, a_dw, a_dw_b, a_ln_g, a_ln_b, b_dw, ab_w_out, cd_w_in, cd_b_in, c_ln_g, c_ln_b, c_ws, c_ws_b, d_rel_bias, cd_w_out, mix_ln_g, mix_ln_b, moe_rg_w, moe_rg_b, moe_re_w, moe_re_b, moe_w_gate, moe_w_up, moe_w_down, ffn_ln_g, ffn_ln_b):
    for layer in range(DEPTH):
        i = layer // 2
        if layer % 2 == 0:
            x = _ab_mixer(x, ab_w_in[i], ab_b_in[i], a_dw[i], a_dw_b[i], a_ln_g[i], a_ln_b[i],
                          b_dw[i], ab_w_out[i], mix_ln_g[layer], mix_ln_b[layer])
        else:
            x = _cd_mixer(x, cd_w_in[i], cd_b_in[i], c_ln_g[i], c_ln_b[i], c_ws[i], c_ws_b[i],
                          d_rel_bias[i], cd_w_out[i], mix_ln_g[layer], mix_ln_b[layer])
        x = _moe_layer(x, moe_rg_w[layer], moe_rg_b[layer], moe_re_w[layer], moe_re_b[layer],
                       moe_w_gate, moe_w_up, moe_w_down, layer,
                       ffn_ln_g[layer], ffn_ln_b[layer])
    return x
```

```python
import functools

import jax
import jax.numpy as jnp
from jax import lax
from jax.experimental import pallas as pl
from jax.experimental.pallas import tpu as pltpu

F32 = jnp.float32
BF16 = jnp.bfloat16
I32 = jnp.int32

DEPTH = 2
CHUNK = 64
LEFT_CHUNKS = 8
HIST = LEFT_CHUNKS * CHUNK
CONV_A = 31
CONV_B = 3
HEADS_C = 8
SGU_BLOCK = 128
HEADS_D = 8
MAX_REL = 256
N_GROUPS = 4
EXPERTS_PER_GROUP = 8
N_EXPERTS = N_GROUPS * EXPERTS_PER_GROUP
TOP_K = 2
ALPHA = (2 * DEPTH) ** 0.25
LN_EPS = 1e-5
NEG_INF = -1e30
LOG2_E = 1.4426950408889634

LANES = 128
SUBLANES = 8
VMEM_LIMIT = 56 * 1024 * 1024

SEQ_TILE = 512
CONV_ROWS = 32
HALO = 32
Q_ROWS = 256
K_ROWS = Q_ROWS + HIST
MOE_TILE = 512
ROUTER_TILE = MOE_TILE
EXPERT_ROWS = 512
SORT_ROWS = MOE_TILE * TOP_K + N_EXPERTS * SUBLANES
SORT_CHUNK = 256
ROUTER_LANE0 = N_GROUPS
_B_SHIFTS = sorted({(HALO - (CONV_B - 1) + j) % SUBLANES for j in range(CONV_B)} - {0})


def _layer_norm(x, g, b):
    mu = jnp.mean(x, axis=-1, keepdims=True)
    xc = x - mu
    var = jnp.mean(xc * xc, axis=-1, keepdims=True)
    return xc * lax.rsqrt(var + LN_EPS) * g + b


def _const_spec(shape):
    nd = len(shape)
    return pl.BlockSpec(shape, lambda *_: (0,) * nd, pipeline_mode=pl.Buffered(1))


def _ab_mixer_kernel(x_ref, win_ref, bin_ref, adw_ref, adwb_ref, alng_ref, alnb_ref,
                     bdw_ref, wout_ref, lng_ref, lnb_ref, o_ref,
                     abuf, cbuf, gbuf, mbuf, asht, csht, *, d_a, d_b):
    ts = x_ref.shape[0]

    @pl.when(pl.program_id(1) == 0)
    def _():
        abuf[0:HALO, :] = jnp.zeros((HALO, d_a), F32)
        cbuf[0:HALO, :] = jnp.zeros((HALO, d_b), F32)

    x = x_ref[...]
    h = jnp.dot(x.astype(BF16), win_ref[...], preferred_element_type=F32) + bin_ref[...]
    abuf[HALO:HALO + ts, :] = h[:, 0:d_a] * jax.nn.sigmoid(h[:, d_a:2 * d_a])
    gbuf[...] = h[:, 2 * d_a:2 * d_a + d_b]
    cbuf[HALO:HALO + ts, :] = h[:, 2 * d_a + d_b:2 * d_a + 2 * d_b] * h[:, 2 * d_a + 2 * d_b:]

    n_sh = HALO + ts - SUBLANES
    for sh in range(1, SUBLANES):
        asht[sh, 0:n_sh, :] = abuf[sh:sh + n_sh, :]
    for sh in _B_SHIFTS:
        csht[_B_SHIFTS.index(sh), 0:n_sh, :] = cbuf[sh:sh + n_sh, :]

    def tap(buf, sht, slot_of, r0, off):
        base = r0 + (off // SUBLANES) * SUBLANES
        if off % SUBLANES == 0:
            return buf[pl.ds(base, CONV_ROWS), :]
        return sht[slot_of(off % SUBLANES), pl.ds(base, CONV_ROWS), :]

    def conv_chunk(i, carry):
        r0 = i * CONV_ROWS
        acc = jnp.broadcast_to(adwb_ref[...], (CONV_ROWS, d_a))
        for j in range(CONV_A):
            acc = acc + adw_ref[j:j + 1, :] * tap(abuf, asht, lambda sh: sh, r0, HALO - (CONV_A - 1) + j)
        a = _layer_norm(acc, alng_ref[...], alnb_ref[...])
        mbuf[pl.ds(r0, CONV_ROWS), 0:d_a] = (a * jax.nn.sigmoid(a)).astype(BF16)
        s = jnp.zeros((CONV_ROWS, d_b), F32)
        for j in range(CONV_B):
            s = s + bdw_ref[j:j + 1, :] * tap(cbuf, csht, _B_SHIFTS.index, r0, HALO - (CONV_B - 1) + j)
        mbuf[pl.ds(r0, CONV_ROWS), d_a:d_a + d_b] = (gbuf[pl.ds(r0, CONV_ROWS), :] * s).astype(BF16)
        return carry

    for i in range(ts // CONV_ROWS):
        conv_chunk(i, 0)

    abuf[0:HALO, :] = abuf[ts:ts + HALO, :]
    cbuf[0:HALO, :] = cbuf[ts:ts + HALO, :]

    mix = jnp.dot(mbuf[...], wout_ref[...], preferred_element_type=F32)
    o_ref[...] = _layer_norm(ALPHA * x + mix, lng_ref[...], lnb_ref[...])


def _ab_mixer(x, w_in, b_in, a_dw, a_dw_b, a_ln_g, a_ln_b, b_dw, w_out, ln_g, ln_b):
    bt, s_len, d = x.shape
    d_a = a_dw.shape[1]
    d_b = b_dw.shape[1]
    w_in_w = w_in.shape[1]
    ts = SEQ_TILE
    row = lambda v: v.reshape(1, -1)
    kern = functools.partial(_ab_mixer_kernel, d_a=d_a, d_b=d_b)
    return pl.pallas_call(
        kern,
        out_shape=jax.ShapeDtypeStruct((bt, s_len, d), F32),
        grid=(bt, s_len // ts),
        in_specs=[
            pl.BlockSpec((None, ts, d), lambda b, t: (b, t, 0)),
            _const_spec((d, w_in_w)),
            _const_spec((1, w_in_w)),
            _const_spec((CONV_A, d_a)),
            _const_spec((1, d_a)),
            _const_spec((1, d_a)),
            _const_spec((1, d_a)),
            _const_spec((CONV_B, d_b)),
            _const_spec((d_a + d_b, d)),
            _const_spec((1, d)),
            _const_spec((1, d)),
        ],
        out_specs=pl.BlockSpec((None, ts, d), lambda b, t: (b, t, 0)),
        scratch_shapes=[
            pltpu.VMEM((HALO + ts, d_a), F32),
            pltpu.VMEM((HALO + ts, d_b), F32),
            pltpu.VMEM((ts, d_b), F32),
            pltpu.VMEM((ts, d_a + d_b), BF16),
            pltpu.VMEM((SUBLANES, HALO + ts, d_a), F32),
            pltpu.VMEM((len(_B_SHIFTS), HALO + ts, d_b), F32),
        ],
        compiler_params=pltpu.CompilerParams(
            dimension_semantics=("arbitrary", "arbitrary"), vmem_limit_bytes=VMEM_LIMIT),
        name="ab_mixer",
    )(x, w_in.astype(BF16), row(b_in), a_dw, row(a_dw_b), row(a_ln_g), row(a_ln_b),
      b_dw, w_out.astype(BF16), row(ln_g), row(ln_b))


def _cd_mixer_kernel(x_ref, win_ref, bin_ref, clng_ref, clnb_ref, wsc_ref, wsb_ref,
                     bias_ref, wout_ref, lng_ref, lnb_ref, o_ref,
                     qbuf, kbuf, vbuf, wsm, vbd, mbuf, *, d_c, d_d):
    ts = x_ref.shape[0]
    t = pl.program_id(1)
    hc = d_c // HEADS_C
    hd = d_d // HEADS_D

    @pl.when(t == 0)
    def _():
        kbuf[0:HIST, :] = jnp.zeros((HIST, d_d), BF16)
        vbuf[0:HIST, :] = jnp.zeros((HIST, d_d), BF16)
        r = lax.broadcasted_iota(I32, (SGU_BLOCK, HEADS_C * SGU_BLOCK), 0)
        c = lax.broadcasted_iota(I32, (SGU_BLOCK, HEADS_C * SGU_BLOCK), 1) % SGU_BLOCK
        wsm[...] = jnp.where(c // CHUNK <= r // CHUNK, wsc_ref[...], 0.0).astype(BF16)

    x = x_ref[...]
    h = jnp.dot(x.astype(BF16), win_ref[...], preferred_element_type=F32) + bin_ref[...]
    u = h[:, 0:d_c]
    vn = _layer_norm(h[:, d_c:2 * d_c], clng_ref[...], clnb_ref[...])
    qbuf[...] = (h[:, 2 * d_c:2 * d_c + d_d] * (hd ** -0.5 * LOG2_E)).astype(BF16)
    kbuf[HIST:HIST + ts, :] = h[:, 2 * d_c + d_d:2 * d_c + 2 * d_d].astype(BF16)
    vbuf[HIST:HIST + ts, :] = h[:, 2 * d_c + 2 * d_d:].astype(BF16)

    lane_head = lax.broadcasted_iota(I32, (SGU_BLOCK, d_c), 1) // hc
    for nb in range(ts // SGU_BLOCK):
        v_blk = vn[nb * SGU_BLOCK:(nb + 1) * SGU_BLOCK, :]
        for hh in range(HEADS_C):
            vbd[hh * SGU_BLOCK:(hh + 1) * SGU_BLOCK, :] = jnp.where(
                lane_head == hh, v_blk, 0.0).astype(BF16)
        gate = jnp.dot(wsm[...], vbd[...], preferred_element_type=F32) + wsb_ref[...]
        mbuf[nb * SGU_BLOCK:(nb + 1) * SGU_BLOCK, 0:d_c] = (
            u[nb * SGU_BLOCK:(nb + 1) * SGU_BLOCK, :] * gate).astype(BF16)

    for qb in range(ts // Q_ROWS):
        q0 = qb * Q_ROWS
        variant = jnp.where(t == 0, 1 + qb, 0)
        for hh in range(HEADS_D):
            q = qbuf[q0:q0 + Q_ROWS, hh * hd:(hh + 1) * hd]
            k = kbuf[q0:q0 + K_ROWS, hh * hd:(hh + 1) * hd]
            v = vbuf[q0:q0 + K_ROWS, hh * hd:(hh + 1) * hd]
            s = lax.dot_general(q, k, (((1,), (1,)), ((), ())), preferred_element_type=F32)
            s = s + bias_ref[variant, hh]
            p = jnp.exp2(s - jnp.max(s, axis=-1, keepdims=True))
            l = jnp.sum(p, axis=-1, keepdims=True)
            o = jnp.dot(p.astype(BF16), v, preferred_element_type=F32)
            mbuf[q0:q0 + Q_ROWS, d_c + hh * hd:d_c + (hh + 1) * hd] = (o / l).astype(BF16)

    kbuf[0:HIST, :] = kbuf[ts:ts + HIST, :]
    vbuf[0:HIST, :] = vbuf[ts:ts + HIST, :]

    mix = jnp.dot(mbuf[...], wout_ref[...], preferred_element_type=F32)
    o_ref[...] = _layer_norm(ALPHA * x + mix, lng_ref[...], lnb_ref[...])


def _attention_bias(d_rel_bias):
    heads = d_rel_bias.shape[0]
    i = jnp.arange(Q_ROWS)[:, None]
    j = jnp.arange(K_ROWS)[None, :]
    jb = j - (i // CHUNK) * CHUNK
    in_band = jnp.logical_and(jb >= 0, jb < (LEFT_CHUNKS + 1) * CHUNK)
    n_diag = Q_ROWS + K_ROWS - 1
    m = jnp.arange(n_diag)
    diag = d_rel_bias[:, jnp.clip(Q_ROWS - 1 + HIST - m, -MAX_REL, MAX_REL) + MAX_REL].astype(F32)
    period = jnp.pad(diag, ((0, 0), (0, 1)))
    flat = jnp.tile(period, (1, Q_ROWS))[:, :Q_ROWS * n_diag]
    rel = flat.reshape(heads, Q_ROWS, n_diag)[:, :, Q_ROWS - 1:Q_ROWS - 1 + K_ROWS]
    rel = jnp.where(in_band[None], rel * LOG2_E, NEG_INF)
    variants = [rel] + [jnp.where(j[None] + qb * Q_ROWS >= HIST, rel, NEG_INF)
                        for qb in range(SEQ_TILE // Q_ROWS)]
    return jnp.stack(variants)


def _cd_mixer(x, w_in, b_in, c_ln_g, c_ln_b, c_ws, c_ws_b, d_rel_bias, w_out, ln_g, ln_b):
    bt, s_len, d = x.shape
    d_c = c_ln_g.shape[0]
    d_d = w_out.shape[0] - d_c
    w_in_w = w_in.shape[1]
    ts = SEQ_TILE
    assert ts == HIST and ts % Q_ROWS == 0 and ts % SGU_BLOCK == 0
    row = lambda v: v.reshape(1, -1)
    ws_cat = jnp.transpose(c_ws, (1, 0, 2)).reshape(SGU_BLOCK, HEADS_C * SGU_BLOCK)
    bias_full = jnp.repeat(c_ws_b.T, d_c // HEADS_C, axis=1)
    att_bias = _attention_bias(d_rel_bias)
    kern = functools.partial(_cd_mixer_kernel, d_c=d_c, d_d=d_d)
    return pl.pallas_call(
        kern,
        out_shape=jax.ShapeDtypeStruct((bt, s_len, d), F32),
        grid=(bt, s_len // ts),
        in_specs=[
            pl.BlockSpec((None, ts, d), lambda b, t: (b, t, 0)),
            _const_spec((d, w_in_w)),
            _const_spec((1, w_in_w)),
            _const_spec((1, d_c)),
            _const_spec((1, d_c)),
            _const_spec((SGU_BLOCK, HEADS_C * SGU_BLOCK)),
            _const_spec((SGU_BLOCK, d_c)),
            _const_spec((1 + ts // Q_ROWS, HEADS_D, Q_ROWS, K_ROWS)),
            _const_spec((d_c + d_d, d)),
            _const_spec((1, d)),
            _const_spec((1, d)),
        ],
        out_specs=pl.BlockSpec((None, ts, d), lambda b, t: (b, t, 0)),
        scratch_shapes=[
            pltpu.VMEM((ts, d_d), BF16),
            pltpu.VMEM((HIST + ts, d_d), BF16),
            pltpu.VMEM((HIST + ts, d_d), BF16),
            pltpu.VMEM((SGU_BLOCK, HEADS_C * SGU_BLOCK), BF16),
            pltpu.VMEM((HEADS_C * SGU_BLOCK, d_c), BF16),
            pltpu.VMEM((ts, d_c + d_d), BF16),
        ],
        compiler_params=pltpu.CompilerParams(
            dimension_semantics=("arbitrary", "arbitrary"), vmem_limit_bytes=VMEM_LIMIT),
        name="cd_mixer",
    )(x, w_in.astype(BF16), row(b_in), row(c_ln_g), row(c_ln_b), ws_cat, bias_full,
      att_bias, w_out.astype(BF16), row(ln_g), row(ln_b))


def _router_kernel(x_ref, wr_ref, br_ref, rt_ref, rn_ref, cnt_ref, before_ref, lower_ref):
    tr = x_ref.shape[0]

    @pl.when(pl.program_id(0) == 0)
    def _():
        rr = lax.broadcasted_iota(I32, (tr, tr), 0)
        cc = lax.broadcasted_iota(I32, (tr, tr), 1)
        before_ref[...] = jnp.where(cc < rr, 1.0, 0.0).astype(BF16)
        lr = lax.broadcasted_iota(I32, (LANES, LANES), 0)
        lc = lax.broadcasted_iota(I32, (LANES, LANES), 1)
        lower_ref[...] = jnp.where(lr < lc, 1.0, 0.0).astype(BF16)

    x = x_ref[...]
    x_hi = x.astype(BF16)
    x_lo = (x - x_hi.astype(F32)).astype(BF16)
    p_hi = jnp.dot(x_hi, wr_ref[...], preferred_element_type=F32)
    p_lo = jnp.dot(x_lo, wr_ref[...], preferred_element_type=F32)
    logits = p_hi + pltpu.roll(p_hi, LANES // 2, axis=1) + p_lo + br_ref[...]
    lane = lax.broadcasted_iota(I32, (tr, LANES), 1).astype(F32)

    def top1(vals):
        m = jnp.max(vals, axis=-1, keepdims=True)
        idx = jnp.min(jnp.where(vals == m, lane, float(LANES)), axis=-1, keepdims=True)
        return m, idx

    g_mask = lane < N_GROUPS
    g_max, g_idx = top1(jnp.where(g_mask, logits, NEG_INF))
    g_top = 1.0 / jnp.sum(jnp.where(g_mask, jnp.exp(logits - g_max), 0.0), axis=-1, keepdims=True)

    e_lo = ROUTER_LANE0 + EXPERTS_PER_GROUP * g_idx
    e_vals = jnp.where(jnp.logical_and(lane >= e_lo, lane < e_lo + EXPERTS_PER_GROUP), logits, NEG_INF)
    m1, i1 = top1(e_vals)
    m2, i2 = top1(jnp.where(lane == i1, NEG_INF, e_vals))
    ratio = jnp.exp(m2 - m1)
    w1 = g_top / (1.0 + ratio)
    w2 = g_top * ratio / (1.0 + ratio)

    chosen = jnp.logical_or(lane == i1, lane == i2)
    onehot = jnp.where(chosen, 1.0, 0.0)
    seen = jnp.dot(before_ref[...], onehot.astype(BF16), preferred_element_type=F32)
    counts = jnp.broadcast_to(jnp.sum(onehot, axis=0, keepdims=True), (SUBLANES, LANES))
    cnt_ref[...] = counts

    groups = (counts.astype(I32) + (SUBLANES - 1)) // SUBLANES
    seg_start = SUBLANES * jnp.dot(groups.astype(F32).astype(BF16), lower_ref[...],
                                   preferred_element_type=F32)
    row = seen + seg_start[0:1, :]
    pos1 = jnp.sum(jnp.where(lane == i1, row, 0.0), axis=-1, keepdims=True)
    pos2 = jnp.sum(jnp.where(lane == i2, row, 0.0), axis=-1, keepdims=True)

    fields = (i1 - ROUTER_LANE0, i2 - ROUTER_LANE0, w1, w2, pos1, pos2)
    res = jnp.zeros((tr, LANES), F32)
    for f, val in enumerate(fields):
        res = jnp.where(lane == f, val, res)
    rn_ref[...] = res
    rt_ref[...] = res.T[0:SUBLANES, :]


def _router(xt, rg_w, rg_b, re_w, re_b):
    n_tok, d = xt.shape
    tr = ROUTER_TILE
    w_all = jnp.concatenate([rg_w, jnp.transpose(re_w, (1, 0, 2)).reshape(d, N_EXPERTS)], axis=1)
    b_all = jnp.concatenate([rg_b, re_b.reshape(N_EXPERTS)])
    half = LANES // 2
    pad = half - w_all.shape[1]
    w_hi = w_all.astype(BF16)
    w_lo = (w_all - w_hi.astype(F32)).astype(BF16)
    w_all = jnp.concatenate([jnp.pad(w_hi, ((0, 0), (0, pad))), jnp.pad(w_lo, ((0, 0), (0, pad)))], axis=1)
    b_all = jnp.pad(b_all, (0, LANES - b_all.shape[0])).reshape(1, LANES)
    return pl.pallas_call(
        _router_kernel,
        out_shape=(
            jax.ShapeDtypeStruct((SUBLANES, n_tok), F32),
            jax.ShapeDtypeStruct((n_tok, LANES), F32),
            jax.ShapeDtypeStruct((n_tok // tr, SUBLANES, LANES), F32),
        ),
        grid=(n_tok // tr,),
        in_specs=[
            pl.BlockSpec((tr, d), lambda i: (i, 0)),
            _const_spec((d, LANES)),
            _const_spec((1, LANES)),
        ],
        out_specs=(
            pl.BlockSpec((SUBLANES, tr), lambda i: (0, i)),
            pl.BlockSpec((tr, LANES), lambda i: (i, 0)),
            pl.BlockSpec((None, SUBLANES, LANES), lambda i: (i, 0, 0)),
        ),
        scratch_shapes=[pltpu.VMEM((tr, tr), BF16), pltpu.VMEM((LANES, LANES), BF16)],
        compiler_params=pltpu.CompilerParams(
            dimension_semantics=("arbitrary",), vmem_limit_bytes=VMEM_LIMIT),
        name="router",
    )(xt, w_all, b_all)


def _round_up(v, m):
    return ((v + m - 1) // m) * m


def _num_expert_blocks(n_tok):
    n_tiles = n_tok // MOE_TILE
    rows = n_tok * TOP_K + n_tiles * N_EXPERTS * (SUBLANES - 1)
    return -(-rows // EXPERT_ROWS) + N_EXPERTS


def _routing_tables(cnt, n_tok):
    counts = cnt[:, 0, ROUTER_LANE0:ROUTER_LANE0 + N_EXPERTS].astype(I32)
    seg = _round_up(counts, SUBLANES)
    tile_off = jnp.cumsum(seg, axis=1) - seg
    tot = jnp.sum(seg, axis=0)
    reg = _round_up(tot, EXPERT_ROWS)
    reg_end = jnp.cumsum(reg)
    reg_start = reg_end - reg
    glob_off = reg_start[None, :] + jnp.cumsum(seg, axis=0) - seg
    n_blocks = _num_expert_blocks(n_tok)
    blk_start = jnp.arange(n_blocks, dtype=I32) * EXPERT_ROWS
    block_e = jnp.sum((reg_end[None, :] <= blk_start[:, None]).astype(I32), axis=1)
    block_e = jnp.minimum(block_e, N_EXPERTS - 1).astype(I32)
    n_used = (reg_end[-1] // EXPERT_ROWS).astype(I32).reshape(1)
    flat = lambda a: a.reshape(-1).astype(I32)
    return dict(tile_off=flat(tile_off), glob_off=flat(glob_off), seg=flat(seg),
                tail_start=flat(reg_start + tot), tail_len=flat(reg - tot),
                block_e=block_e, n_used=n_used)


def _piece_sizes(max_len):
    return [SUBLANES << b for b in range((max_len // SUBLANES).bit_length())]


def _for_each_piece(length, max_len, fn):
    off = 0
    for size in reversed(_piece_sizes(max_len)):
        take = (length & size) != 0

        @pl.when(take)
        def _(off=off, size=size):
            fn(off, size)

        off = off + jnp.where(take, size, 0)


def _dispatch_kernel(toff_ref, goff_ref, seg_ref, tstart_ref, tlen_ref, nu_ref, x_ref, rt_ref,
                     xp_hbm, obuf, zbuf, sem, zsem):
    i = pl.program_id(0)
    n = pl.num_programs(0)
    tile = x_ref.shape[0]
    slot = i % 2

    def wait_segments(t, sl):
        last = t * N_EXPERTS + N_EXPERTS - 1

        def piece(off, size):
            pltpu.make_async_copy(obuf.at[sl, pl.ds(0, size), :], xp_hbm.at[pl.ds(0, size), :],
                                  sem.at[sl]).wait()

        _for_each_piece(toff_ref[last] + seg_ref[last], SORT_ROWS, piece)

    def start_segments(t, sl, live, inline):
        def per_expert(e, c):
            idx = t * N_EXPERTS + e
            t0 = toff_ref[idx]
            g0 = goff_ref[idx]

            def piece(off, size):
                pltpu.make_async_copy(
                    obuf.at[sl, pl.ds(pl.multiple_of(t0 + off, SUBLANES), size), :],
                    xp_hbm.at[pl.ds(pl.multiple_of(g0 + off, SUBLANES), size), :], sem.at[sl]).start()

            _for_each_piece(jnp.where(live, seg_ref[idx], 0), tile, piece)
            return c

        if inline:
            for e in range(N_EXPERTS):
                per_expert(e, 0)
        else:
            lax.fori_loop(0, N_EXPERTS, per_expert, 0)

    @pl.when(i >= 2)
    def _():
        wait_segments(i - 2, slot)

    start_segments(jnp.maximum(i - 1, 0), 1 - slot, i >= 1, inline=True)

    pos1 = rt_ref[4:5, :].astype(I32)
    pos2 = rt_ref[5:6, :].astype(I32)
    xb = x_ref[...].astype(BF16)
    for c in range(SORT_ROWS // SORT_CHUNK):
        rows = lax.broadcasted_iota(I32, (SORT_CHUNK, tile), 0) + c * SORT_CHUNK
        hit = jnp.logical_or(rows == pos1, rows == pos2)
        obuf[slot, c * SORT_CHUNK:(c + 1) * SORT_CHUNK, :] = jnp.dot(
            jnp.where(hit, 1.0, 0.0).astype(BF16), xb, preferred_element_type=F32)

    def zero_fill(wait):
        def per_expert(e, c):
            def piece(off, size):
                cp = pltpu.make_async_copy(
                    zbuf.at[pl.ds(0, size), :],
                    xp_hbm.at[pl.ds(pl.multiple_of(tstart_ref[e] + off, SUBLANES), size), :], zsem)
                cp.wait() if wait else cp.start()

            _for_each_piece(tlen_ref[e], EXPERT_ROWS - SUBLANES, piece)
            return c

        lax.fori_loop(0, N_EXPERTS, per_expert, 0)

        def per_half_block(h, c):
            cp = pltpu.make_async_copy(
                zbuf, xp_hbm.at[pl.ds(pl.multiple_of(h * zbuf.shape[0], zbuf.shape[0]), zbuf.shape[0]), :], zsem)
            cp.wait() if wait else cp.start()
            return c

        halves = EXPERT_ROWS // zbuf.shape[0]
        lax.fori_loop(nu_ref[0] * halves, (xp_hbm.shape[0] // EXPERT_ROWS) * halves, per_half_block, 0)

    @pl.when(i == 0)
    def _():
        zbuf[...] = jnp.zeros(zbuf.shape, zbuf.dtype)
        zero_fill(False)

    @pl.when(i == n - 1)
    def _():
        start_segments(i, slot, True, inline=False)

        @pl.when(i >= 1)
        def _():
            wait_segments(i - 1, 1 - slot)

        wait_segments(i, slot)
        zero_fill(True)


def _dispatch(xt, rt, tabs):
    n_tok, d = xt.shape
    tile = MOE_TILE
    n_rows = _num_expert_blocks(n_tok) * EXPERT_ROWS
    return pl.pallas_call(
        _dispatch_kernel,
        out_shape=jax.ShapeDtypeStruct((n_rows, d), F32),
        grid_spec=pltpu.PrefetchScalarGridSpec(
            num_scalar_prefetch=6,
            grid=(n_tok // tile,),
            in_specs=[
                pl.BlockSpec((tile, d), lambda i, *_: (i, 0)),
                pl.BlockSpec((SUBLANES, tile), lambda i, *_: (0, i)),
            ],
            out_specs=pl.BlockSpec(memory_space=pl.ANY),
            scratch_shapes=[
                pltpu.VMEM((2, SORT_ROWS, d), F32),
                pltpu.VMEM((EXPERT_ROWS // 2, d), F32),
                pltpu.SemaphoreType.DMA((2,)),
                pltpu.SemaphoreType.DMA,
            ],
        ),
        compiler_params=pltpu.CompilerParams(
            dimension_semantics=("arbitrary",), vmem_limit_bytes=VMEM_LIMIT),
        name="dispatch",
    )(tabs["tile_off"], tabs["glob_off"], tabs["seg"], tabs["tail_start"], tabs["tail_len"],
      tabs["n_used"], xt, rt)


def _ffn_kernel(be_ref, nu_ref, x_ref, wg_ref, wu_ref, wd_ref, o_ref, wg_b, wu_b, wd_b):
    i = pl.program_id(0)
    used = i < nu_ref[0]

    @pl.when(jnp.logical_and(used, jnp.logical_or(i == 0, be_ref[i] != be_ref[jnp.maximum(i - 1, 0)])))
    def _():
        wg_b[...] = wg_ref[...].astype(BF16)
        wu_b[...] = wu_ref[...].astype(BF16)
        wd_b[...] = wd_ref[...].astype(BF16)

    @pl.when(used)
    def _():
        xb = x_ref[...].astype(BF16)
        g = jnp.dot(xb, wg_b[...], preferred_element_type=F32)
        u = jnp.dot(xb, wu_b[...], preferred_element_type=F32)
        hb = (g * jax.nn.sigmoid(g) * u).astype(BF16)
        o_ref[...] = jnp.dot(hb, wd_b[...], preferred_element_type=F32)

    @pl.when(jnp.logical_not(used))
    def _():
        o_ref[...] = jnp.zeros(o_ref.shape, o_ref.dtype)


def _expert_ffn(xp, block_e, n_used, w_gate, w_up, w_down, layer):
    n_rows, d = xp.shape
    d_e = w_gate.shape[3]
    n_blocks = n_rows // EXPERT_ROWS
    blk = lambda i, be, nu: (jnp.minimum(i, nu[0] - 1), 0)
    wsel = lambda i, be, nu: (layer, be[jnp.minimum(i, nu[0] - 1)], 0, 0)
    return pl.pallas_call(
        _ffn_kernel,
        out_shape=jax.ShapeDtypeStruct((n_rows, d), F32),
        grid_spec=pltpu.PrefetchScalarGridSpec(
            num_scalar_prefetch=2,
            grid=(n_blocks,),
            in_specs=[
                pl.BlockSpec((EXPERT_ROWS, d), blk),
                pl.BlockSpec((None, None, d, d_e), wsel),
                pl.BlockSpec((None, None, d, d_e), wsel),
                pl.BlockSpec((None, None, d_e, d), wsel),
            ],
            out_specs=pl.BlockSpec((EXPERT_ROWS, d), lambda i, be, nu: (i, 0)),
            scratch_shapes=[
                pltpu.VMEM((d, d_e), BF16),
                pltpu.VMEM((d, d_e), BF16),
                pltpu.VMEM((d_e, d), BF16),
            ],
        ),
        compiler_params=pltpu.CompilerParams(
            dimension_semantics=("arbitrary",), vmem_limit_bytes=VMEM_LIMIT),
        name="expert_ffn",
    )(block_e, n_used, xp, w_gate, w_up, w_down)


def _combine_kernel(toff_ref, goff_ref, seg_ref, x_ref, rn_ref, yp_hbm, lng_ref, lnb_ref, o_ref,
                    ybuf, sem):
    i = pl.program_id(0)
    n = pl.num_programs(0)
    tile = x_ref.shape[0]
    slot = i % 2

    def wait_segments(t, sl):
        last = t * N_EXPERTS + N_EXPERTS - 1

        def piece(off, size):
            pltpu.make_async_copy(yp_hbm.at[pl.ds(0, size), :], ybuf.at[sl, pl.ds(0, size), :],
                                  sem.at[sl]).wait()

        _for_each_piece(toff_ref[last] + seg_ref[last], SORT_ROWS, piece)

    def start_segments(t, sl, live, inline):
        def per_expert(e, c):
            idx = t * N_EXPERTS + e
            t0 = toff_ref[idx]
            g0 = goff_ref[idx]

            def piece(off, size):
                pltpu.make_async_copy(
                    yp_hbm.at[pl.ds(pl.multiple_of(g0 + off, SUBLANES), size), :],
                    ybuf.at[sl, pl.ds(pl.multiple_of(t0 + off, SUBLANES), size), :], sem.at[sl]).start()

            _for_each_piece(jnp.where(live, seg_ref[idx], 0), tile, piece)
            return c

        if inline:
            for e in range(N_EXPERTS):
                per_expert(e, 0)
        else:
            lax.fori_loop(0, N_EXPERTS, per_expert, 0)

    @pl.when(i == 0)
    def _():
        ybuf[...] = jnp.zeros(ybuf.shape, ybuf.dtype)
        start_segments(0, 0, True, inline=False)

    wait_segments(i, slot)
    start_segments(jnp.minimum(i + 1, n - 1), 1 - slot, i + 1 < n, inline=True)

    rn = rn_ref[...]
    full = lambda col: jnp.broadcast_to(col, (tile, LANES))
    w1, w2 = full(rn[:, 2:3]), full(rn[:, 3:4])
    pos1, pos2 = full(rn[:, 4:5]), full(rn[:, 5:6])
    lane = lax.broadcasted_iota(I32, (tile, LANES), 1).astype(F32)

    def pick_lanes(col0):
        cols = lane + float(col0)
        return jnp.where(cols == pos1, w1, 0.0) + jnp.where(cols == pos2, w2, 0.0)

    ffn = jnp.zeros(o_ref.shape, F32)
    for c in range(SORT_ROWS // SORT_CHUNK):
        pick = jnp.concatenate([pick_lanes(c * SORT_CHUNK + l0) for l0 in range(0, SORT_CHUNK, LANES)],
                               axis=1).astype(BF16)
        ffn = ffn + jnp.dot(pick, ybuf[slot, c * SORT_CHUNK:(c + 1) * SORT_CHUNK, :].astype(BF16),
                            preferred_element_type=F32)
    o_ref[...] = _layer_norm(ALPHA * x_ref[...] + ffn, lng_ref[...], lnb_ref[...])


def _combine(xt, rn, yp, tabs, ln_g, ln_b):
    n_tok, d = xt.shape
    tile = MOE_TILE
    return pl.pallas_call(
        _combine_kernel,
        out_shape=jax.ShapeDtypeStruct((n_tok, d), F32),
        grid_spec=pltpu.PrefetchScalarGridSpec(
            num_scalar_prefetch=3,
            grid=(n_tok // tile,),
            in_specs=[
                pl.BlockSpec((tile, d), lambda i, *_: (i, 0)),
                pl.BlockSpec((tile, LANES), lambda i, *_: (i, 0)),
                pl.BlockSpec(memory_space=pl.ANY),
                pl.BlockSpec((1, d), lambda i, *_: (0, 0)),
                pl.BlockSpec((1, d), lambda i, *_: (0, 0)),
            ],
            out_specs=pl.BlockSpec((tile, d), lambda i, *_: (i, 0)),
            scratch_shapes=[
                pltpu.VMEM((2, SORT_ROWS, d), F32),
                pltpu.SemaphoreType.DMA((2,)),
            ],
        ),
        compiler_params=pltpu.CompilerParams(
            dimension_semantics=("arbitrary",), vmem_limit_bytes=VMEM_LIMIT),
        name="combine",
    )(tabs["tile_off"], tabs["glob_off"], tabs["seg"], xt, rn, yp,
      ln_g.reshape(1, d), ln_b.reshape(1, d))


def _moe_layer(x, rg_w, rg_b, re_w, re_b, w_gate, w_up, w_down, layer, ln_g, ln_b):
    bt, s_len, d = x.shape
    n_tok = bt * s_len
    xt = x.reshape(n_tok, d)
    rt, rn, cnt = _router(xt, rg_w, rg_b, re_w, re_b)
    tabs = _routing_tables(cnt, n_tok)
    xp = _dispatch(xt, rt, tabs)
    yp = _expert_ffn(xp, tabs["block_e"], tabs["n_used"], w_gate, w_up, w_down, layer)
    out = _combine(xt, rn, yp, tabs, ln_g, ln_b)
    return out.reshape(bt, s_len, d)


def kernel(x, ab_w_in, ab_b_in, a_dw, a_dw_b, a_ln_g, a_ln_b, b_dw, ab_w_out, cd_w_in, cd_b_in, c_ln_g, c_ln_b, c_ws, c_ws_b, d_rel_bias, cd_w_out, mix_ln_g, mix_ln_b, moe_rg_w, moe_rg_b, moe_re_w, moe_re_b, moe_w_gate, moe_w_up, moe_w_down, ffn_ln_g, ffn_ln_b):
    for layer in range(DEPTH):
        i = layer // 2
        if layer % 2 == 0:
            x = _ab_mixer(x, ab_w_in[i], ab_b_in[i], a_dw[i], a_dw_b[i], a_ln_g[i], a_ln_b[i],
                          b_dw[i], ab_w_out[i], mix_ln_g[layer], mix_ln_b[layer])
        else:
            x = _cd_mixer(x, cd_w_in[i], cd_b_in[i], c_ln_g[i], c_ln_b[i], c_ws[i], c_ws_b[i],
                          d_rel_bias[i], cd_w_out[i], mix_ln_g[layer], mix_ln_b[layer])
        x = _moe_layer(x, moe_rg_w[layer], moe_rg_b[layer], moe_re_w[layer], moe_re_b[layer],
                       moe_w_gate, moe_w_up, moe_w_down, layer,
                       ffn_ln_g[layer], ffn_ln_b[layer])
    return x
```

```python
import functools

import jax
import jax.numpy as jnp
from jax import lax
from jax.experimental import pallas as pl
from jax.experimental.pallas import tpu as pltpu

F32 = jnp.float32
BF16 = jnp.bfloat16
I32 = jnp.int32

DEPTH = 2
CHUNK = 64
LEFT_CHUNKS = 8
HIST = LEFT_CHUNKS * CHUNK
CONV_A = 31
CONV_B = 3
HEADS_C = 8
SGU_BLOCK = 128
HEADS_D = 8
MAX_REL = 256
N_GROUPS = 4
EXPERTS_PER_GROUP = 8
N_EXPERTS = N_GROUPS * EXPERTS_PER_GROUP
TOP_K = 2
ALPHA = (2 * DEPTH) ** 0.25
LN_EPS = 1e-5
NEG_INF = -1e30
LOG2_E = 1.4426950408889634

LANES = 128
SUBLANES = 8
VMEM_LIMIT = 56 * 1024 * 1024

SEQ_TILE = 512
CONV_ROWS = 32
HALO = 32
Q_ROWS = 256
K_ROWS = Q_ROWS + HIST
MOE_TILE = 512
ROUTER_TILE = MOE_TILE
EXPERT_ROWS = 512
SEG_ALIGN = 2 * SUBLANES
SORT_ROWS = MOE_TILE * TOP_K + N_EXPERTS * SEG_ALIGN
SORT_CHUNK = 256
INLINE_GROUPS = 8
ROUTER_LANE0 = N_GROUPS
_B_SHIFTS = sorted({(HALO - (CONV_B - 1) + j) % SUBLANES for j in range(CONV_B)} - {0})


def _layer_norm(x, g, b):
    mu = jnp.mean(x, axis=-1, keepdims=True)
    xc = x - mu
    var = jnp.mean(xc * xc, axis=-1, keepdims=True)
    return xc * lax.rsqrt(var + LN_EPS) * g + b


def _const_spec(shape):
    nd = len(shape)
    return pl.BlockSpec(shape, lambda *_: (0,) * nd, pipeline_mode=pl.Buffered(1))


def _ab_mixer_kernel(x_ref, win_ref, bin_ref, adw_ref, adwb_ref, alng_ref, alnb_ref,
                     bdw_ref, wout_ref, lng_ref, lnb_ref, o_ref,
                     abuf, cbuf, gbuf, mbuf, asht, csht, *, d_a, d_b):
    ts = x_ref.shape[0]

    @pl.when(pl.program_id(1) == 0)
    def _():
        abuf[0:HALO, :] = jnp.zeros((HALO, d_a), F32)
        cbuf[0:HALO, :] = jnp.zeros((HALO, d_b), F32)

    x = x_ref[...]
    h = jnp.dot(x.astype(BF16), win_ref[...], preferred_element_type=F32) + bin_ref[...]
    abuf[HALO:HALO + ts, :] = h[:, 0:d_a] * jax.nn.sigmoid(h[:, d_a:2 * d_a])
    gbuf[...] = h[:, 2 * d_a:2 * d_a + d_b]
    cbuf[HALO:HALO + ts, :] = h[:, 2 * d_a + d_b:2 * d_a + 2 * d_b] * h[:, 2 * d_a + 2 * d_b:]

    n_sh = HALO + ts - SUBLANES
    for sh in range(1, SUBLANES):
        asht[sh, 0:n_sh, :] = abuf[sh:sh + n_sh, :]
    for sh in _B_SHIFTS:
        csht[_B_SHIFTS.index(sh), 0:n_sh, :] = cbuf[sh:sh + n_sh, :]

    def tap(buf, sht, slot_of, r0, off):
        base = r0 + (off // SUBLANES) * SUBLANES
        if off % SUBLANES == 0:
            return buf[pl.ds(base, CONV_ROWS), :]
        return sht[slot_of(off % SUBLANES), pl.ds(base, CONV_ROWS), :]

    def conv_chunk(i, carry):
        r0 = i * CONV_ROWS
        acc = jnp.broadcast_to(adwb_ref[...], (CONV_ROWS, d_a))
        for j in range(CONV_A):
            acc = acc + adw_ref[j:j + 1, :] * tap(abuf, asht, lambda sh: sh, r0, HALO - (CONV_A - 1) + j)
        a = _layer_norm(acc, alng_ref[...], alnb_ref[...])
        mbuf[pl.ds(r0, CONV_ROWS), 0:d_a] = (a * jax.nn.sigmoid(a)).astype(BF16)
        s = jnp.zeros((CONV_ROWS, d_b), F32)
        for j in range(CONV_B):
            s = s + bdw_ref[j:j + 1, :] * tap(cbuf, csht, _B_SHIFTS.index, r0, HALO - (CONV_B - 1) + j)
        mbuf[pl.ds(r0, CONV_ROWS), d_a:d_a + d_b] = (gbuf[pl.ds(r0, CONV_ROWS), :] * s).astype(BF16)
        return carry

    for i in range(ts // CONV_ROWS):
        conv_chunk(i, 0)

    abuf[0:HALO, :] = abuf[ts:ts + HALO, :]
    cbuf[0:HALO, :] = cbuf[ts:ts + HALO, :]

    mix = jnp.dot(mbuf[...], wout_ref[...], preferred_element_type=F32)
    o_ref[...] = _layer_norm(ALPHA * x + mix, lng_ref[...], lnb_ref[...])


def _ab_mixer(x, w_in, b_in, a_dw, a_dw_b, a_ln_g, a_ln_b, b_dw, w_out, ln_g, ln_b):
    bt, s_len, d = x.shape
    d_a = a_dw.shape[1]
    d_b = b_dw.shape[1]
    w_in_w = w_in.shape[1]
    ts = SEQ_TILE
    row = lambda v: v.reshape(1, -1)
    kern = functools.partial(_ab_mixer_kernel, d_a=d_a, d_b=d_b)
    return pl.pallas_call(
        kern,
        out_shape=jax.ShapeDtypeStruct((bt, s_len, d), F32),
        grid=(bt, s_len // ts),
        in_specs=[
            pl.BlockSpec((None, ts, d), lambda b, t: (b, t, 0)),
            _const_spec((d, w_in_w)),
            _const_spec((1, w_in_w)),
            _const_spec((CONV_A, d_a)),
            _const_spec((1, d_a)),
            _const_spec((1, d_a)),
            _const_spec((1, d_a)),
            _const_spec((CONV_B, d_b)),
            _const_spec((d_a + d_b, d)),
            _const_spec((1, d)),
            _const_spec((1, d)),
        ],
        out_specs=pl.BlockSpec((None, ts, d), lambda b, t: (b, t, 0)),
        scratch_shapes=[
            pltpu.VMEM((HALO + ts, d_a), F32),
            pltpu.VMEM((HALO + ts, d_b), F32),
            pltpu.VMEM((ts, d_b), F32),
            pltpu.VMEM((ts, d_a + d_b), BF16),
            pltpu.VMEM((SUBLANES, HALO + ts, d_a), F32),
            pltpu.VMEM((len(_B_SHIFTS), HALO + ts, d_b), F32),
        ],
        compiler_params=pltpu.CompilerParams(
            dimension_semantics=("arbitrary", "arbitrary"), vmem_limit_bytes=VMEM_LIMIT),
        name="ab_mixer",
    )(x, w_in.astype(BF16), row(b_in), a_dw, row(a_dw_b), row(a_ln_g), row(a_ln_b),
      b_dw, w_out.astype(BF16), row(ln_g), row(ln_b))


def _cd_mixer_kernel(x_ref, win_ref, bin_ref, clng_ref, clnb_ref, wsc_ref, wsb_ref,
                     bias_ref, wout_ref, lng_ref, lnb_ref, o_ref,
                     qbuf, kbuf, vbuf, wsm, vbd, mbuf, *, d_c, d_d):
    ts = x_ref.shape[0]
    t = pl.program_id(1)
    hc = d_c // HEADS_C
    hd = d_d // HEADS_D

    @pl.when(t == 0)
    def _():
        kbuf[0:HIST, :] = jnp.zeros((HIST, d_d), BF16)
        vbuf[0:HIST, :] = jnp.zeros((HIST, d_d), BF16)
        r = lax.broadcasted_iota(I32, (SGU_BLOCK, HEADS_C * SGU_BLOCK), 0)
        c = lax.broadcasted_iota(I32, (SGU_BLOCK, HEADS_C * SGU_BLOCK), 1) % SGU_BLOCK
        wsm[...] = jnp.where(c // CHUNK <= r // CHUNK, wsc_ref[...], 0.0).astype(BF16)

    x = x_ref[...]
    h = jnp.dot(x.astype(BF16), win_ref[...], preferred_element_type=F32) + bin_ref[...]
    u = h[:, 0:d_c]
    vn = _layer_norm(h[:, d_c:2 * d_c], clng_ref[...], clnb_ref[...])
    qbuf[...] = (h[:, 2 * d_c:2 * d_c + d_d] * (hd ** -0.5 * LOG2_E)).astype(BF16)
    kbuf[HIST:HIST + ts, :] = h[:, 2 * d_c + d_d:2 * d_c + 2 * d_d].astype(BF16)
    vbuf[HIST:HIST + ts, :] = h[:, 2 * d_c + 2 * d_d:].astype(BF16)

    lane_head = lax.broadcasted_iota(I32, (SGU_BLOCK, d_c), 1) // hc
    for nb in range(ts // SGU_BLOCK):
        v_blk = vn[nb * SGU_BLOCK:(nb + 1) * SGU_BLOCK, :]
        for hh in range(HEADS_C):
            vbd[hh * SGU_BLOCK:(hh + 1) * SGU_BLOCK, :] = jnp.where(
                lane_head == hh, v_blk, 0.0).astype(BF16)
        gate = jnp.dot(wsm[...], vbd[...], preferred_element_type=F32) + wsb_ref[...]
        mbuf[nb * SGU_BLOCK:(nb + 1) * SGU_BLOCK, 0:d_c] = (
            u[nb * SGU_BLOCK:(nb + 1) * SGU_BLOCK, :] * gate).astype(BF16)

    for qb in range(ts // Q_ROWS):
        q0 = qb * Q_ROWS
        variant = jnp.where(t == 0, 1 + qb, 0)
        for hh in range(HEADS_D):
            q = qbuf[q0:q0 + Q_ROWS, hh * hd:(hh + 1) * hd]
            k = kbuf[q0:q0 + K_ROWS, hh * hd:(hh + 1) * hd]
            v = vbuf[q0:q0 + K_ROWS, hh * hd:(hh + 1) * hd]
            s = lax.dot_general(q, k, (((1,), (1,)), ((), ())), preferred_element_type=F32)
            s = s + bias_ref[variant, hh]
            p = jnp.exp2(s - jnp.max(s, axis=-1, keepdims=True))
            l = jnp.sum(p, axis=-1, keepdims=True)
            o = jnp.dot(p.astype(BF16), v, preferred_element_type=F32)
            mbuf[q0:q0 + Q_ROWS, d_c + hh * hd:d_c + (hh + 1) * hd] = (o / l).astype(BF16)

    kbuf[0:HIST, :] = kbuf[ts:ts + HIST, :]
    vbuf[0:HIST, :] = vbuf[ts:ts + HIST, :]

    mix = jnp.dot(mbuf[...], wout_ref[...], preferred_element_type=F32)
    o_ref[...] = _layer_norm(ALPHA * x + mix, lng_ref[...], lnb_ref[...])


def _attention_bias(d_rel_bias):
    heads = d_rel_bias.shape[0]
    i = jnp.arange(Q_ROWS)[:, None]
    j = jnp.arange(K_ROWS)[None, :]
    jb = j - (i // CHUNK) * CHUNK
    in_band = jnp.logical_and(jb >= 0, jb < (LEFT_CHUNKS + 1) * CHUNK)
    n_diag = Q_ROWS + K_ROWS - 1
    m = jnp.arange(n_diag)
    diag = d_rel_bias[:, jnp.clip(Q_ROWS - 1 + HIST - m, -MAX_REL, MAX_REL) + MAX_REL].astype(F32)
    period = jnp.pad(diag, ((0, 0), (0, 1)))
    flat = jnp.tile(period, (1, Q_ROWS))[:, :Q_ROWS * n_diag]
    rel = flat.reshape(heads, Q_ROWS, n_diag)[:, :, Q_ROWS - 1:Q_ROWS - 1 + K_ROWS]
    rel = jnp.where(in_band[None], rel * LOG2_E, NEG_INF)
    variants = [rel] + [jnp.where(j[None] + qb * Q_ROWS >= HIST, rel, NEG_INF)
                        for qb in range(SEQ_TILE // Q_ROWS)]
    return jnp.stack(variants)


def _cd_mixer(x, w_in, b_in, c_ln_g, c_ln_b, c_ws, c_ws_b, d_rel_bias, w_out, ln_g, ln_b):
    bt, s_len, d = x.shape
    d_c = c_ln_g.shape[0]
    d_d = w_out.shape[0] - d_c
    w_in_w = w_in.shape[1]
    ts = SEQ_TILE
    assert ts == HIST and ts % Q_ROWS == 0 and ts % SGU_BLOCK == 0
    row = lambda v: v.reshape(1, -1)
    ws_cat = jnp.transpose(c_ws, (1, 0, 2)).reshape(SGU_BLOCK, HEADS_C * SGU_BLOCK)
    bias_full = jnp.repeat(c_ws_b.T, d_c // HEADS_C, axis=1)
    att_bias = _attention_bias(d_rel_bias)
    kern = functools.partial(_cd_mixer_kernel, d_c=d_c, d_d=d_d)
    return pl.pallas_call(
        kern,
        out_shape=jax.ShapeDtypeStruct((bt, s_len, d), F32),
        grid=(bt, s_len // ts),
        in_specs=[
            pl.BlockSpec((None, ts, d), lambda b, t: (b, t, 0)),
            _const_spec((d, w_in_w)),
            _const_spec((1, w_in_w)),
            _const_spec((1, d_c)),
            _const_spec((1, d_c)),
            _const_spec((SGU_BLOCK, HEADS_C * SGU_BLOCK)),
            _const_spec((SGU_BLOCK, d_c)),
            _const_spec((1 + ts // Q_ROWS, HEADS_D, Q_ROWS, K_ROWS)),
            _const_spec((d_c + d_d, d)),
            _const_spec((1, d)),
            _const_spec((1, d)),
        ],
        out_specs=pl.BlockSpec((None, ts, d), lambda b, t: (b, t, 0)),
        scratch_shapes=[
            pltpu.VMEM((ts, d_d), BF16),
            pltpu.VMEM((HIST + ts, d_d), BF16),
            pltpu.VMEM((HIST + ts, d_d), BF16),
            pltpu.VMEM((SGU_BLOCK, HEADS_C * SGU_BLOCK), BF16),
            pltpu.VMEM((HEADS_C * SGU_BLOCK, d_c), BF16),
            pltpu.VMEM((ts, d_c + d_d), BF16),
        ],
        compiler_params=pltpu.CompilerParams(
            dimension_semantics=("arbitrary", "arbitrary"), vmem_limit_bytes=VMEM_LIMIT),
        name="cd_mixer",
    )(x, w_in.astype(BF16), row(b_in), row(c_ln_g), row(c_ln_b), ws_cat, bias_full,
      att_bias, w_out.astype(BF16), row(ln_g), row(ln_b))


def _router_kernel(x_ref, wr_ref, br_ref, rt_ref, rn_ref, cnt_ref, before_ref, lower_ref):
    tr = x_ref.shape[0]

    @pl.when(pl.program_id(0) == 0)
    def _():
        rr = lax.broadcasted_iota(I32, (tr, tr), 0)
        cc = lax.broadcasted_iota(I32, (tr, tr), 1)
        before_ref[...] = jnp.where(cc < rr, 1.0, 0.0).astype(BF16)
        lr = lax.broadcasted_iota(I32, (LANES, LANES), 0)
        lc = lax.broadcasted_iota(I32, (LANES, LANES), 1)
        lower_ref[...] = jnp.where(lr < lc, 1.0, 0.0).astype(BF16)

    x = x_ref[...]
    x_hi = x.astype(BF16)
    x_lo = (x - x_hi.astype(F32)).astype(BF16)
    p_hi = jnp.dot(x_hi, wr_ref[...], preferred_element_type=F32)
    p_lo = jnp.dot(x_lo, wr_ref[...], preferred_element_type=F32)
    logits = p_hi + pltpu.roll(p_hi, LANES // 2, axis=1) + p_lo + br_ref[...]
    lane = lax.broadcasted_iota(I32, (tr, LANES), 1).astype(F32)

    def top1(vals):
        m = jnp.max(vals, axis=-1, keepdims=True)
        idx = jnp.min(jnp.where(vals == m, lane, float(LANES)), axis=-1, keepdims=True)
        return m, idx

    g_mask = lane < N_GROUPS
    g_max, g_idx = top1(jnp.where(g_mask, logits, NEG_INF))
    g_top = 1.0 / jnp.sum(jnp.where(g_mask, jnp.exp(logits - g_max), 0.0), axis=-1, keepdims=True)

    e_lo = ROUTER_LANE0 + EXPERTS_PER_GROUP * g_idx
    e_vals = jnp.where(jnp.logical_and(lane >= e_lo, lane < e_lo + EXPERTS_PER_GROUP), logits, NEG_INF)
    m1, i1 = top1(e_vals)
    m2, i2 = top1(jnp.where(lane == i1, NEG_INF, e_vals))
    ratio = jnp.exp(m2 - m1)
    w1 = g_top / (1.0 + ratio)
    w2 = g_top * ratio / (1.0 + ratio)

    chosen = jnp.logical_or(lane == i1, lane == i2)
    onehot = jnp.where(chosen, 1.0, 0.0)
    seen = jnp.dot(before_ref[...], onehot.astype(BF16), preferred_element_type=F32)
    counts = jnp.broadcast_to(jnp.sum(onehot, axis=0, keepdims=True), (SUBLANES, LANES))
    cnt_ref[...] = counts

    groups = (counts.astype(I32) + (SEG_ALIGN - 1)) // SEG_ALIGN
    seg_start = SEG_ALIGN * jnp.dot(groups.astype(F32).astype(BF16), lower_ref[...],
                                    preferred_element_type=F32)
    row = seen + seg_start[0:1, :]
    pos1 = jnp.sum(jnp.where(lane == i1, row, 0.0), axis=-1, keepdims=True)
    pos2 = jnp.sum(jnp.where(lane == i2, row, 0.0), axis=-1, keepdims=True)

    fields = (i1 - ROUTER_LANE0, i2 - ROUTER_LANE0, w1, w2, pos1, pos2)
    res = jnp.zeros((tr, LANES), F32)
    for f, val in enumerate(fields):
        res = jnp.where(lane == f, val, res)
    rn_ref[...] = res
    rt_ref[...] = res.T[0:SUBLANES, :]


def _router(xt, rg_w, rg_b, re_w, re_b):
    n_tok, d = xt.shape
    tr = ROUTER_TILE
    w_all = jnp.concatenate([rg_w, jnp.transpose(re_w, (1, 0, 2)).reshape(d, N_EXPERTS)], axis=1)
    b_all = jnp.concatenate([rg_b, re_b.reshape(N_EXPERTS)])
    half = LANES // 2
    pad = half - w_all.shape[1]
    w_hi = w_all.astype(BF16)
    w_lo = (w_all - w_hi.astype(F32)).astype(BF16)
    w_all = jnp.concatenate([jnp.pad(w_hi, ((0, 0), (0, pad))), jnp.pad(w_lo, ((0, 0), (0, pad)))], axis=1)
    b_all = jnp.pad(b_all, (0, LANES - b_all.shape[0])).reshape(1, LANES)
    return pl.pallas_call(
        _router_kernel,
        out_shape=(
            jax.ShapeDtypeStruct((SUBLANES, n_tok), F32),
            jax.ShapeDtypeStruct((n_tok, LANES), F32),
            jax.ShapeDtypeStruct((n_tok // tr, SUBLANES, LANES), F32),
        ),
        grid=(n_tok // tr,),
        in_specs=[
            pl.BlockSpec((tr, d), lambda i: (i, 0)),
            _const_spec((d, LANES)),
            _const_spec((1, LANES)),
        ],
        out_specs=(
            pl.BlockSpec((SUBLANES, tr), lambda i: (0, i)),
            pl.BlockSpec((tr, LANES), lambda i: (i, 0)),
            pl.BlockSpec((None, SUBLANES, LANES), lambda i: (i, 0, 0)),
        ),
        scratch_shapes=[pltpu.VMEM((tr, tr), BF16), pltpu.VMEM((LANES, LANES), BF16)],
        compiler_params=pltpu.CompilerParams(
            dimension_semantics=("arbitrary",), vmem_limit_bytes=VMEM_LIMIT),
        name="router",
    )(xt, w_all, b_all)


def _round_up(v, m):
    return ((v + m - 1) // m) * m


def _num_expert_blocks(n_tok):
    n_tiles = n_tok // MOE_TILE
    rows = n_tok * TOP_K + n_tiles * N_EXPERTS * (SEG_ALIGN - 1)
    return -(-rows // EXPERT_ROWS) + N_EXPERTS


def _routing_tables(cnt, n_tok):
    counts = cnt[:, 0, ROUTER_LANE0:ROUTER_LANE0 + N_EXPERTS].astype(I32)
    seg = _round_up(counts, SEG_ALIGN)
    tile_off = jnp.cumsum(seg, axis=1) - seg
    tot = jnp.sum(seg, axis=0)
    reg = _round_up(tot, EXPERT_ROWS)
    reg_end = jnp.cumsum(reg)
    reg_start = reg_end - reg
    glob_off = reg_start[None, :] + jnp.cumsum(seg, axis=0) - seg
    n_blocks = _num_expert_blocks(n_tok)
    blk_start = jnp.arange(n_blocks, dtype=I32) * EXPERT_ROWS
    block_e = jnp.sum((reg_end[None, :] <= blk_start[:, None]).astype(I32), axis=1)
    block_e = jnp.minimum(block_e, N_EXPERTS - 1).astype(I32)
    n_used = (reg_end[-1] // EXPERT_ROWS).astype(I32).reshape(1)
    groups = lambda a: (a.reshape(-1) // SEG_ALIGN).astype(I32)
    big = (jnp.max(seg, axis=1) >= INLINE_GROUPS * SEG_ALIGN).astype(I32)
    return dict(tile_off=groups(tile_off), glob_off=groups(glob_off), seg=groups(seg), big=big,
                tail_start=groups(reg_start + tot), tail_len=groups(reg - tot),
                block_e=block_e, n_used=n_used)


def _grouped(rows):
    return rows.reshape(rows.shape[0] // SEG_ALIGN, SEG_ALIGN, rows.shape[1])


def _ungrouped(groups):
    return groups.reshape(groups.shape[0] * SEG_ALIGN, groups.shape[2])


def _for_each_piece(length, max_len, fn):
    off = 0
    for size in reversed([1 << b for b in range(max_len.bit_length())]):
        take = (length & size) != 0

        @pl.when(take)
        def _(off=off, size=size):
            fn(off, size)

        off = off + jnp.where(take, size, 0)


def _start_segment_copies(copy, toff_ref, goff_ref, seg_ref, big_ref, t, live, inline):
    def per_expert(e, limit, enabled):
        idx = t * N_EXPERTS + e
        t0 = toff_ref[idx]
        g0 = goff_ref[idx]
        _for_each_piece(jnp.where(enabled, seg_ref[idx], 0), limit,
                        lambda off, size: copy(t0 + off, g0 + off, size))

    def looped(enabled):
        def body(e, c):
            per_expert(e, MOE_TILE // SEG_ALIGN, enabled)
            return c

        lax.fori_loop(0, N_EXPERTS, body, 0)

    if not inline:
        looped(live)
        return
    big = big_ref[t] != 0
    for e in range(N_EXPERTS):
        per_expert(e, INLINE_GROUPS - 1, jnp.logical_and(live, jnp.logical_not(big)))

    @pl.when(jnp.logical_and(live, big))
    def _():
        looped(True)


def _wait_segment_copies(wait, toff_ref, seg_ref, t):
    last = t * N_EXPERTS + N_EXPERTS - 1
    _for_each_piece(toff_ref[last] + seg_ref[last], SORT_ROWS // SEG_ALIGN, lambda off, size: wait(size))


def _dispatch_kernel(toff_ref, goff_ref, seg_ref, big_ref, tstart_ref, tlen_ref, nu_ref,
                     x_ref, rt_ref, xp_hbm, obuf, zbuf, sem, zsem):
    i = pl.program_id(0)
    n = pl.num_programs(0)
    tile = x_ref.shape[0]
    slot = i % 2

    def wait_segments(t, sl):
        _wait_segment_copies(
            lambda size: pltpu.make_async_copy(obuf.at[sl, pl.ds(0, size)], xp_hbm.at[pl.ds(0, size)],
                                               sem.at[sl]).wait(),
            toff_ref, seg_ref, t)

    def start_segments(t, sl, live, inline):
        _start_segment_copies(
            lambda tg, sg, size: pltpu.make_async_copy(
                obuf.at[sl, pl.ds(tg, size)], xp_hbm.at[pl.ds(sg, size)], sem.at[sl]).start(),
            toff_ref, goff_ref, seg_ref, big_ref, t, live, inline)

    @pl.when(i >= 2)
    def _():
        wait_segments(i - 2, slot)

    start_segments(jnp.maximum(i - 1, 0), 1 - slot, i >= 1, inline=True)

    pos1 = rt_ref[4:5, :].astype(I32)
    pos2 = rt_ref[5:6, :].astype(I32)
    xb = x_ref[...].astype(BF16)
    for c in range(SORT_ROWS // SORT_CHUNK):
        rows = lax.broadcasted_iota(I32, (SORT_CHUNK, tile), 0) + c * SORT_CHUNK
        hit = jnp.logical_or(rows == pos1, rows == pos2)
        g0, g1 = c * SORT_CHUNK // SEG_ALIGN, (c + 1) * SORT_CHUNK // SEG_ALIGN
        obuf[slot, g0:g1] = _grouped(jnp.dot(
            jnp.where(hit, 1.0, 0.0).astype(BF16), xb, preferred_element_type=F32).astype(BF16))

    def zero_fill(wait):
        def per_expert(e, c):
            def piece(off, size):
                cp = pltpu.make_async_copy(zbuf.at[pl.ds(0, size)],
                                           xp_hbm.at[pl.ds(tstart_ref[e] + off, size)], zsem)
                cp.wait() if wait else cp.start()

            _for_each_piece(tlen_ref[e], EXPERT_ROWS // SEG_ALIGN - 1, piece)
            return c

        lax.fori_loop(0, N_EXPERTS, per_expert, 0)

        zg = zbuf.shape[0]

        def per_zero_copy(h, c):
            cp = pltpu.make_async_copy(zbuf, xp_hbm.at[pl.ds(h * zg, zg)], zsem)
            cp.wait() if wait else cp.start()
            return c

        per_block = EXPERT_ROWS // SEG_ALIGN // zg
        lax.fori_loop(nu_ref[0] * per_block, xp_hbm.shape[0] // zg, per_zero_copy, 0)

    @pl.when(i == 0)
    def _():
        zbuf[...] = jnp.zeros(zbuf.shape, zbuf.dtype)
        zero_fill(False)

    @pl.when(i == n - 1)
    def _():
        start_segments(i, slot, True, inline=False)

        @pl.when(i >= 1)
        def _():
            wait_segments(i - 1, 1 - slot)

        wait_segments(i, slot)
        zero_fill(True)


def _dispatch(xt, rt, tabs):
    n_tok, d = xt.shape
    tile = MOE_TILE
    n_groups = _num_expert_blocks(n_tok) * EXPERT_ROWS // SEG_ALIGN
    return pl.pallas_call(
        _dispatch_kernel,
        out_shape=jax.ShapeDtypeStruct((n_groups, SEG_ALIGN, d), BF16),
        grid_spec=pltpu.PrefetchScalarGridSpec(
            num_scalar_prefetch=7,
            grid=(n_tok // tile,),
            in_specs=[
                pl.BlockSpec((tile, d), lambda i, *_: (i, 0)),
                pl.BlockSpec((SUBLANES, tile), lambda i, *_: (0, i)),
            ],
            out_specs=pl.BlockSpec(memory_space=pl.ANY),
            scratch_shapes=[
                pltpu.VMEM((2, SORT_ROWS // SEG_ALIGN, SEG_ALIGN, d), BF16),
                pltpu.VMEM((EXPERT_ROWS // SEG_ALIGN // 2, SEG_ALIGN, d), BF16),
                pltpu.SemaphoreType.DMA((2,)),
                pltpu.SemaphoreType.DMA,
            ],
        ),
        compiler_params=pltpu.CompilerParams(
            dimension_semantics=("arbitrary",), vmem_limit_bytes=VMEM_LIMIT),
        name="dispatch",
    )(tabs["tile_off"], tabs["glob_off"], tabs["seg"], tabs["big"], tabs["tail_start"],
      tabs["tail_len"], tabs["n_used"], xt, rt)


def _ffn_kernel(be_ref, nu_ref, x_ref, wg_ref, wu_ref, wd_ref, o_ref, wg_b, wu_b, wd_b):
    i = pl.program_id(0)
    used = i < nu_ref[0]

    @pl.when(jnp.logical_and(used, jnp.logical_or(i == 0, be_ref[i] != be_ref[jnp.maximum(i - 1, 0)])))
    def _():
        wg_b[...] = wg_ref[...].astype(BF16)
        wu_b[...] = wu_ref[...].astype(BF16)
        wd_b[...] = wd_ref[...].astype(BF16)

    @pl.when(used)
    def _():
        xb = _ungrouped(x_ref[...])
        g = jnp.dot(xb, wg_b[...], preferred_element_type=F32)
        u = jnp.dot(xb, wu_b[...], preferred_element_type=F32)
        hb = (g * jax.nn.sigmoid(g) * u).astype(BF16)
        o_ref[...] = _grouped(jnp.dot(hb, wd_b[...], preferred_element_type=F32).astype(BF16))

    @pl.when(jnp.logical_not(used))
    def _():
        o_ref[...] = jnp.zeros(o_ref.shape, o_ref.dtype)


def _expert_ffn(xp, block_e, n_used, w_gate, w_up, w_down, layer):
    n_groups, _, d = xp.shape
    d_e = w_gate.shape[3]
    blk_groups = EXPERT_ROWS // SEG_ALIGN
    n_blocks = n_groups // blk_groups
    blk = lambda i, be, nu: (jnp.minimum(i, nu[0] - 1), 0, 0)
    wsel = lambda i, be, nu: (layer, be[jnp.minimum(i, nu[0] - 1)], 0, 0)
    return pl.pallas_call(
        _ffn_kernel,
        out_shape=jax.ShapeDtypeStruct((n_groups, SEG_ALIGN, d), BF16),
        grid_spec=pltpu.PrefetchScalarGridSpec(
            num_scalar_prefetch=2,
            grid=(n_blocks,),
            in_specs=[
                pl.BlockSpec((blk_groups, SEG_ALIGN, d), blk),
                pl.BlockSpec((None, None, d, d_e), wsel),
                pl.BlockSpec((None, None, d, d_e), wsel),
                pl.BlockSpec((None, None, d_e, d), wsel),
            ],
            out_specs=pl.BlockSpec((blk_groups, SEG_ALIGN, d), lambda i, be, nu: (i, 0, 0)),
            scratch_shapes=[
                pltpu.VMEM((d, d_e), BF16),
                pltpu.VMEM((d, d_e), BF16),
                pltpu.VMEM((d_e, d), BF16),
            ],
        ),
        compiler_params=pltpu.CompilerParams(
            dimension_semantics=("arbitrary",), vmem_limit_bytes=VMEM_LIMIT),
        name="expert_ffn",
    )(block_e, n_used, xp, w_gate, w_up, w_down)


def _combine_kernel(toff_ref, goff_ref, seg_ref, big_ref, x_ref, rn_ref, yp_hbm, lng_ref, lnb_ref,
                    o_ref, ybuf, sem):
    i = pl.program_id(0)
    n = pl.num_programs(0)
    tile = x_ref.shape[0]
    slot = i % 2

    def wait_segments(t, sl):
        _wait_segment_copies(
            lambda size: pltpu.make_async_copy(yp_hbm.at[pl.ds(0, size)], ybuf.at[sl, pl.ds(0, size)],
                                               sem.at[sl]).wait(),
            toff_ref, seg_ref, t)

    def start_segments(t, sl, live, inline):
        _start_segment_copies(
            lambda tg, sg, size: pltpu.make_async_copy(
                yp_hbm.at[pl.ds(sg, size)], ybuf.at[sl, pl.ds(tg, size)], sem.at[sl]).start(),
            toff_ref, goff_ref, seg_ref, big_ref, t, live, inline)

    @pl.when(i == 0)
    def _():
        ybuf[...] = jnp.zeros(ybuf.shape, ybuf.dtype)
        start_segments(0, 0, True, inline=False)

    wait_segments(i, slot)
    start_segments(jnp.minimum(i + 1, n - 1), 1 - slot, i + 1 < n, inline=True)

    rn = rn_ref[...]
    full = lambda col: jnp.broadcast_to(col, (tile, LANES))
    w1, w2 = full(rn[:, 2:3]), full(rn[:, 3:4])
    pos1, pos2 = full(rn[:, 4:5]), full(rn[:, 5:6])
    lane = lax.broadcasted_iota(I32, (tile, LANES), 1).astype(F32)

    def pick_lanes(col0):
        cols = lane + float(col0)
        return jnp.where(cols == pos1, w1, 0.0) + jnp.where(cols == pos2, w2, 0.0)

    ffn = jnp.zeros(o_ref.shape, F32)
    for c in range(SORT_ROWS // SORT_CHUNK):
        pick = jnp.concatenate([pick_lanes(c * SORT_CHUNK + l0) for l0 in range(0, SORT_CHUNK, LANES)],
                               axis=1).astype(BF16)
        g0, g1 = c * SORT_CHUNK // SEG_ALIGN, (c + 1) * SORT_CHUNK // SEG_ALIGN
        ffn = ffn + jnp.dot(pick, _ungrouped(ybuf[slot, g0:g1]), preferred_element_type=F32)
    o_ref[...] = _layer_norm(ALPHA * x_ref[...] + ffn, lng_ref[...], lnb_ref[...])


def _combine(xt, rn, yp, tabs, ln_g, ln_b):
    n_tok, d = xt.shape
    tile = MOE_TILE
    return pl.pallas_call(
        _combine_kernel,
        out_shape=jax.ShapeDtypeStruct((n_tok, d), F32),
        grid_spec=pltpu.PrefetchScalarGridSpec(
            num_scalar_prefetch=4,
            grid=(n_tok // tile,),
            in_specs=[
                pl.BlockSpec((tile, d), lambda i, *_: (i, 0)),
                pl.BlockSpec((tile, LANES), lambda i, *_: (i, 0)),
                pl.BlockSpec(memory_space=pl.ANY),
                pl.BlockSpec((1, d), lambda i, *_: (0, 0)),
                pl.BlockSpec((1, d), lambda i, *_: (0, 0)),
            ],
            out_specs=pl.BlockSpec((tile, d), lambda i, *_: (i, 0)),
            scratch_shapes=[
                pltpu.VMEM((2, SORT_ROWS // SEG_ALIGN, SEG_ALIGN, d), BF16),
                pltpu.SemaphoreType.DMA((2,)),
            ],
        ),
        compiler_params=pltpu.CompilerParams(
            dimension_semantics=("arbitrary",), vmem_limit_bytes=VMEM_LIMIT),
        name="combine",
    )(tabs["tile_off"], tabs["glob_off"], tabs["seg"], tabs["big"], xt, rn, yp,
      ln_g.reshape(1, d), ln_b.reshape(1, d))


def _moe_layer(x, rg_w, rg_b, re_w, re_b, w_gate, w_up, w_down, layer, ln_g, ln_b):
    bt, s_len, d = x.shape
    n_tok = bt * s_len
    xt = x.reshape(n_tok, d)
    rt, rn, cnt = _router(xt, rg_w, rg_b, re_w, re_b)
    tabs = _routing_tables(cnt, n_tok)
    xp = _dispatch(xt, rt, tabs)
    yp = _expert_ffn(xp, tabs["block_e"], tabs["n_used"], w_gate, w_up, w_down, layer)
    out = _combine(xt, rn, yp, tabs, ln_g, ln_b)
    return out.reshape(bt, s_len, d)


def kernel(x, ab_w_in, ab_b_in, a_dw, a_dw_b, a_ln_g, a_ln_b, b_dw, ab_w_out, cd_w_in, cd_b_in, c_ln_g, c_ln_b, c_ws, c_ws_b, d_rel_bias, cd_w_out, mix_ln_g, mix_ln_b, moe_rg_w, moe_rg_b, moe_re_w, moe_re_b, moe_w_gate, moe_w_up, moe_w_down, ffn_ln_g, ffn_ln_b):
    for layer in range(DEPTH):
        i = layer // 2
        if layer % 2 == 0:
            x = _ab_mixer(x, ab_w_in[i], ab_b_in[i], a_dw[i], a_dw_b[i], a_ln_g[i], a_ln_b[i],
                          b_dw[i], ab_w_out[i], mix_ln_g[layer], mix_ln_b[layer])
        else:
            x = _cd_mixer(x, cd_w_in[i], cd_b_in[i], c_ln_g[i], c_ln_b[i], c_ws[i], c_ws_b[i],
                          d_rel_bias[i], cd_w_out[i], mix_ln_g[layer], mix_ln_b[layer])
        x = _moe_layer(x, moe_rg_w[layer], moe_rg_b[layer], moe_re_w[layer], moe_re_b[layer],
                       moe_w_gate, moe_w_up, moe_w_down, layer,
                       ffn_ln_g[layer], ffn_ln_b[layer])
    return x
```

```python
import functools

import jax
import jax.numpy as jnp
from jax import lax
from jax.experimental import pallas as pl
from jax.experimental.pallas import tpu as pltpu

F32 = jnp.float32
BF16 = jnp.bfloat16
I32 = jnp.int32

DEPTH = 2
CHUNK = 64
LEFT_CHUNKS = 8
HIST = LEFT_CHUNKS * CHUNK
CONV_A = 31
CONV_B = 3
HEADS_C = 8
SGU_BLOCK = 128
HEADS_D = 8
MAX_REL = 256
N_GROUPS = 4
EXPERTS_PER_GROUP = 8
N_EXPERTS = N_GROUPS * EXPERTS_PER_GROUP
TOP_K = 2
ALPHA = (2 * DEPTH) ** 0.25
LN_EPS = 1e-5
NEG_INF = -1e30
LOG2_E = 1.4426950408889634

LANES = 128
SUBLANES = 8
VMEM_LIMIT = 56 * 1024 * 1024

SEQ_TILE = 512
CONV_ROWS = 64
HALO = 32
Q_ROWS = 256
K_ROWS = Q_ROWS + HIST
MOE_TILE = 512
ROUTER_TILE = MOE_TILE
EXPERT_ROWS = 512
SEG_ALIGN = 2 * SUBLANES
SORT_ROWS = MOE_TILE * TOP_K + N_EXPERTS * SEG_ALIGN
SORT_CHUNK = 256
INLINE_GROUPS = 8
ROUTER_LANE0 = N_GROUPS
_B_SHIFTS = sorted({(HALO - (CONV_B - 1) + j) % SUBLANES for j in range(CONV_B)} - {0})


def _layer_norm(x, g, b):
    mu = jnp.mean(x, axis=-1, keepdims=True)
    xc = x - mu
    var = jnp.mean(xc * xc, axis=-1, keepdims=True)
    return xc * lax.rsqrt(var + LN_EPS) * g + b


def _const_spec(shape):
    nd = len(shape)
    return pl.BlockSpec(shape, lambda *_: (0,) * nd, pipeline_mode=pl.Buffered(1))


def _ab_mixer_kernel(x_ref, win_ref, bin_ref, adw_ref, adwb_ref, alng_ref, alnb_ref,
                     bdw_ref, wout_ref, lng_ref, lnb_ref, o_ref,
                     abuf, cbuf, gbuf, mbuf, asht, csht, *, d_a, d_b):
    ts = x_ref.shape[0]

    @pl.when(pl.program_id(1) == 0)
    def _():
        abuf[0:HALO, :] = jnp.zeros((HALO, d_a), F32)
        cbuf[0:HALO, :] = jnp.zeros((HALO, d_b), F32)

    x = x_ref[...]
    xb = x.astype(BF16)
    n_sh = HALO + ts - SUBLANES

    def tap(buf, sht, slot_of, r0, off):
        base = r0 + (off // SUBLANES) * SUBLANES
        if off % SUBLANES == 0:
            return buf[base:base + CONV_ROWS, :]
        return sht[slot_of(off % SUBLANES), base:base + CONV_ROWS, :]

    h_a = jnp.dot(xb, win_ref[:, 0:2 * d_a], preferred_element_type=F32) + bin_ref[:, 0:2 * d_a]
    abuf[HALO:HALO + ts, :] = h_a[:, 0:d_a] * jax.nn.sigmoid(h_a[:, d_a:])
    for sh in range(1, SUBLANES):
        asht[sh, 0:n_sh, :] = abuf[sh:sh + n_sh, :]
    for r0 in range(0, ts, CONV_ROWS):
        acc = jnp.broadcast_to(adwb_ref[...], (CONV_ROWS, d_a))
        for j in range(CONV_A):
            acc = acc + adw_ref[j:j + 1, :] * tap(abuf, asht, lambda sh: sh, r0, HALO - (CONV_A - 1) + j)
        a = _layer_norm(acc, alng_ref[...], alnb_ref[...])
        mbuf[r0:r0 + CONV_ROWS, 0:d_a] = (a * jax.nn.sigmoid(a)).astype(BF16)

    h_b = jnp.dot(xb, win_ref[:, 2 * d_a:], preferred_element_type=F32) + bin_ref[:, 2 * d_a:]
    gbuf[...] = h_b[:, 0:d_b]
    cbuf[HALO:HALO + ts, :] = h_b[:, d_b:2 * d_b] * h_b[:, 2 * d_b:]
    for sh in _B_SHIFTS:
        csht[_B_SHIFTS.index(sh), 0:n_sh, :] = cbuf[sh:sh + n_sh, :]
    for r0 in range(0, ts, CONV_ROWS):
        s = jnp.zeros((CONV_ROWS, d_b), F32)
        for j in range(CONV_B):
            s = s + bdw_ref[j:j + 1, :] * tap(cbuf, csht, _B_SHIFTS.index, r0, HALO - (CONV_B - 1) + j)
        mbuf[r0:r0 + CONV_ROWS, d_a:d_a + d_b] = (gbuf[r0:r0 + CONV_ROWS, :] * s).astype(BF16)

    abuf[0:HALO, :] = abuf[ts:ts + HALO, :]
    cbuf[0:HALO, :] = cbuf[ts:ts + HALO, :]

    mix = jnp.dot(mbuf[...], wout_ref[...], preferred_element_type=F32)
    o_ref[...] = _layer_norm(ALPHA * x + mix, lng_ref[...], lnb_ref[...])


def _ab_mixer(x, w_in, b_in, a_dw, a_dw_b, a_ln_g, a_ln_b, b_dw, w_out, ln_g, ln_b):
    bt, s_len, d = x.shape
    d_a = a_dw.shape[1]
    d_b = b_dw.shape[1]
    w_in_w = w_in.shape[1]
    ts = SEQ_TILE
    row = lambda v: v.reshape(1, -1)
    kern = functools.partial(_ab_mixer_kernel, d_a=d_a, d_b=d_b)
    return pl.pallas_call(
        kern,
        out_shape=jax.ShapeDtypeStruct((bt, s_len, d), F32),
        grid=(bt, s_len // ts),
        in_specs=[
            pl.BlockSpec((None, ts, d), lambda b, t: (b, t, 0)),
            _const_spec((d, w_in_w)),
            _const_spec((1, w_in_w)),
            _const_spec((CONV_A, d_a)),
            _const_spec((1, d_a)),
            _const_spec((1, d_a)),
            _const_spec((1, d_a)),
            _const_spec((CONV_B, d_b)),
            _const_spec((d_a + d_b, d)),
            _const_spec((1, d)),
            _const_spec((1, d)),
        ],
        out_specs=pl.BlockSpec((None, ts, d), lambda b, t: (b, t, 0)),
        scratch_shapes=[
            pltpu.VMEM((HALO + ts, d_a), F32),
            pltpu.VMEM((HALO + ts, d_b), F32),
            pltpu.VMEM((ts, d_b), F32),
            pltpu.VMEM((ts, d_a + d_b), BF16),
            pltpu.VMEM((SUBLANES, HALO + ts, d_a), F32),
            pltpu.VMEM((len(_B_SHIFTS), HALO + ts, d_b), F32),
        ],
        compiler_params=pltpu.CompilerParams(
            dimension_semantics=("arbitrary", "arbitrary"), vmem_limit_bytes=VMEM_LIMIT),
        name="ab_mixer",
    )(x, w_in.astype(BF16), row(b_in), a_dw, row(a_dw_b), row(a_ln_g), row(a_ln_b),
      b_dw, w_out.astype(BF16), row(ln_g), row(ln_b))


def _cd_mixer_kernel(x_ref, win_ref, bin_ref, clng_ref, clnb_ref, wsc_ref, wsb_ref,
                     bias_ref, wout_ref, lng_ref, lnb_ref, o_ref,
                     qbuf, kbuf, vbuf, wsm, vbd, mbuf, *, d_c, d_d):
    ts = x_ref.shape[0]
    t = pl.program_id(1)
    hc = d_c // HEADS_C
    hd = d_d // HEADS_D

    @pl.when(t == 0)
    def _():
        kbuf[0:HIST, :] = jnp.zeros((HIST, d_d), BF16)
        vbuf[0:HIST, :] = jnp.zeros((HIST, d_d), BF16)
        r = lax.broadcasted_iota(I32, (SGU_BLOCK, HEADS_C * SGU_BLOCK), 0)
        c = lax.broadcasted_iota(I32, (SGU_BLOCK, HEADS_C * SGU_BLOCK), 1) % SGU_BLOCK
        wsm[...] = jnp.where(c // CHUNK <= r // CHUNK, wsc_ref[...], 0.0).astype(BF16)

    x = x_ref[...]
    xb = x.astype(BF16)
    h_att = (jnp.dot(xb, win_ref[:, 2 * d_c:], preferred_element_type=F32) + bin_ref[:, 2 * d_c:])
    qbuf[...] = (h_att[:, 0:d_d] * (hd ** -0.5 * LOG2_E)).astype(BF16)
    kbuf[HIST:HIST + ts, :] = h_att[:, d_d:2 * d_d].astype(BF16)
    vbuf[HIST:HIST + ts, :] = h_att[:, 2 * d_d:].astype(BF16)

    def attention(qb):
        q0 = qb * Q_ROWS
        variant = jnp.where(t == 0, 1 + qb, 0)
        for hh in range(HEADS_D):
            q = qbuf[q0:q0 + Q_ROWS, hh * hd:(hh + 1) * hd]
            k = kbuf[q0:q0 + K_ROWS, hh * hd:(hh + 1) * hd]
            v = vbuf[q0:q0 + K_ROWS, hh * hd:(hh + 1) * hd]
            s = lax.dot_general(q, k, (((1,), (1,)), ((), ())), preferred_element_type=F32)
            s = s + bias_ref[variant, hh]
            p = jnp.exp2(s - jnp.max(s, axis=-1, keepdims=True))
            l = jnp.sum(p, axis=-1, keepdims=True)
            o = jnp.dot(p.astype(BF16), v, preferred_element_type=F32)
            mbuf[q0:q0 + Q_ROWS, d_c + hh * hd:d_c + (hh + 1) * hd] = (o / l).astype(BF16)

    for qb in range(ts // Q_ROWS):
        attention(qb)

    kbuf[0:HIST, :] = kbuf[ts:ts + HIST, :]
    vbuf[0:HIST, :] = vbuf[ts:ts + HIST, :]

    h_sgu = (jnp.dot(xb, win_ref[:, 0:2 * d_c], preferred_element_type=F32) + bin_ref[:, 0:2 * d_c])
    u = h_sgu[:, 0:d_c]
    vn = _layer_norm(h_sgu[:, d_c:], clng_ref[...], clnb_ref[...])
    lane_head = lax.broadcasted_iota(I32, (SGU_BLOCK, d_c), 1) // hc
    for nb in range(ts // SGU_BLOCK):
        v_blk = vn[nb * SGU_BLOCK:(nb + 1) * SGU_BLOCK, :]
        for hh in range(HEADS_C):
            vbd[hh * SGU_BLOCK:(hh + 1) * SGU_BLOCK, :] = jnp.where(
                lane_head == hh, v_blk, 0.0).astype(BF16)
        gate = jnp.dot(wsm[...], vbd[...], preferred_element_type=F32) + wsb_ref[...]
        mbuf[nb * SGU_BLOCK:(nb + 1) * SGU_BLOCK, 0:d_c] = (
            u[nb * SGU_BLOCK:(nb + 1) * SGU_BLOCK, :] * gate).astype(BF16)

    mix = jnp.dot(mbuf[...], wout_ref[...], preferred_element_type=F32)
    o_ref[...] = _layer_norm(ALPHA * x + mix, lng_ref[...], lnb_ref[...])


def _attention_bias(d_rel_bias):
    heads = d_rel_bias.shape[0]
    i = jnp.arange(Q_ROWS)[:, None]
    j = jnp.arange(K_ROWS)[None, :]
    jb = j - (i // CHUNK) * CHUNK
    in_band = jnp.logical_and(jb >= 0, jb < (LEFT_CHUNKS + 1) * CHUNK)
    n_diag = Q_ROWS + K_ROWS - 1
    m = jnp.arange(n_diag)
    diag = d_rel_bias[:, jnp.clip(Q_ROWS - 1 + HIST - m, -MAX_REL, MAX_REL) + MAX_REL].astype(F32)
    period = jnp.pad(diag, ((0, 0), (0, 1)))
    flat = jnp.tile(period, (1, Q_ROWS))[:, :Q_ROWS * n_diag]
    rel = flat.reshape(heads, Q_ROWS, n_diag)[:, :, Q_ROWS - 1:Q_ROWS - 1 + K_ROWS]
    rel = jnp.where(in_band[None], rel * LOG2_E, NEG_INF)
    variants = [rel] + [jnp.where(j[None] + qb * Q_ROWS >= HIST, rel, NEG_INF)
                        for qb in range(SEQ_TILE // Q_ROWS)]
    return jnp.stack(variants)


def _cd_mixer(x, w_in, b_in, c_ln_g, c_ln_b, c_ws, c_ws_b, d_rel_bias, w_out, ln_g, ln_b):
    bt, s_len, d = x.shape
    d_c = c_ln_g.shape[0]
    d_d = w_out.shape[0] - d_c
    w_in_w = w_in.shape[1]
    ts = SEQ_TILE
    assert ts == HIST and ts % Q_ROWS == 0 and ts % SGU_BLOCK == 0
    row = lambda v: v.reshape(1, -1)
    ws_cat = jnp.transpose(c_ws, (1, 0, 2)).reshape(SGU_BLOCK, HEADS_C * SGU_BLOCK)
    bias_full = jnp.repeat(c_ws_b.T, d_c // HEADS_C, axis=1)
    att_bias = _attention_bias(d_rel_bias)
    kern = functools.partial(_cd_mixer_kernel, d_c=d_c, d_d=d_d)
    return pl.pallas_call(
        kern,
        out_shape=jax.ShapeDtypeStruct((bt, s_len, d), F32),
        grid=(bt, s_len // ts),
        in_specs=[
            pl.BlockSpec((None, ts, d), lambda b, t: (b, t, 0)),
            _const_spec((d, w_in_w)),
            _const_spec((1, w_in_w)),
            _const_spec((1, d_c)),
            _const_spec((1, d_c)),
            _const_spec((SGU_BLOCK, HEADS_C * SGU_BLOCK)),
            _const_spec((SGU_BLOCK, d_c)),
            _const_spec((1 + ts // Q_ROWS, HEADS_D, Q_ROWS, K_ROWS)),
            _const_spec((d_c + d_d, d)),
            _const_spec((1, d)),
            _const_spec((1, d)),
        ],
        out_specs=pl.BlockSpec((None, ts, d), lambda b, t: (b, t, 0)),
        scratch_shapes=[
            pltpu.VMEM((ts, d_d), BF16),
            pltpu.VMEM((HIST + ts, d_d), BF16),
            pltpu.VMEM((HIST + ts, d_d), BF16),
            pltpu.VMEM((SGU_BLOCK, HEADS_C * SGU_BLOCK), BF16),
            pltpu.VMEM((HEADS_C * SGU_BLOCK, d_c), BF16),
            pltpu.VMEM((ts, d_c + d_d), BF16),
        ],
        compiler_params=pltpu.CompilerParams(
            dimension_semantics=("arbitrary", "arbitrary"), vmem_limit_bytes=VMEM_LIMIT),
        name="cd_mixer",
    )(x, w_in.astype(BF16), row(b_in), row(c_ln_g), row(c_ln_b), ws_cat, bias_full,
      att_bias, w_out.astype(BF16), row(ln_g), row(ln_b))


def _router_kernel(x_ref, wr_ref, br_ref, rt_ref, rn_ref, cnt_ref, before_ref, lower_ref):
    tr = x_ref.shape[0]

    @pl.when(pl.program_id(0) == 0)
    def _():
        rr = lax.broadcasted_iota(I32, (tr, tr), 0)
        cc = lax.broadcasted_iota(I32, (tr, tr), 1)
        before_ref[...] = jnp.where(cc < rr, 1.0, 0.0).astype(BF16)
        lr = lax.broadcasted_iota(I32, (LANES, LANES), 0)
        lc = lax.broadcasted_iota(I32, (LANES, LANES), 1)
        lower_ref[...] = jnp.where(lr < lc, 1.0, 0.0).astype(BF16)

    x = x_ref[...]
    x_hi = x.astype(BF16)
    x_lo = (x - x_hi.astype(F32)).astype(BF16)
    p_hi = jnp.dot(x_hi, wr_ref[...], preferred_element_type=F32)
    p_lo = jnp.dot(x_lo, wr_ref[...], preferred_element_type=F32)
    logits = p_hi + pltpu.roll(p_hi, LANES // 2, axis=1) + p_lo + br_ref[...]
    lane = lax.broadcasted_iota(I32, (tr, LANES), 1).astype(F32)

    def top1(vals):
        m = jnp.max(vals, axis=-1, keepdims=True)
        idx = jnp.min(jnp.where(vals == m, lane, float(LANES)), axis=-1, keepdims=True)
        return m, idx

    g_mask = lane < N_GROUPS
    g_max, g_idx = top1(jnp.where(g_mask, logits, NEG_INF))
    g_top = 1.0 / jnp.sum(jnp.where(g_mask, jnp.exp(logits - g_max), 0.0), axis=-1, keepdims=True)

    e_lo = ROUTER_LANE0 + EXPERTS_PER_GROUP * g_idx
    e_vals = jnp.where(jnp.logical_and(lane >= e_lo, lane < e_lo + EXPERTS_PER_GROUP), logits, NEG_INF)
    m1, i1 = top1(e_vals)
    m2, i2 = top1(jnp.where(lane == i1, NEG_INF, e_vals))
    ratio = jnp.exp(m2 - m1)
    w1 = g_top / (1.0 + ratio)
    w2 = g_top * ratio / (1.0 + ratio)

    chosen = jnp.logical_or(lane == i1, lane == i2)
    onehot = jnp.where(chosen, 1.0, 0.0)
    seen = jnp.dot(before_ref[...], onehot.astype(BF16), preferred_element_type=F32)
    counts = jnp.broadcast_to(jnp.sum(onehot, axis=0, keepdims=True), (SUBLANES, LANES))
    cnt_ref[...] = counts

    groups = (counts.astype(I32) + (SEG_ALIGN - 1)) // SEG_ALIGN
    seg_start = SEG_ALIGN * jnp.dot(groups.astype(F32).astype(BF16), lower_ref[...],
                                    preferred_element_type=F32)
    row = seen + seg_start[0:1, :]
    pos1 = jnp.sum(jnp.where(lane == i1, row, 0.0), axis=-1, keepdims=True)
    pos2 = jnp.sum(jnp.where(lane == i2, row, 0.0), axis=-1, keepdims=True)

    fields = (i1 - ROUTER_LANE0, i2 - ROUTER_LANE0, w1, w2, pos1, pos2)
    res = jnp.zeros((tr, LANES), F32)
    for f, val in enumerate(fields):
        res = jnp.where(lane == f, val, res)
    rn_ref[...] = res
    rt_ref[...] = res.T[0:SUBLANES, :]


def _router(xt, rg_w, rg_b, re_w, re_b):
    n_tok, d = xt.shape
    tr = ROUTER_TILE
    w_all = jnp.concatenate([rg_w, jnp.transpose(re_w, (1, 0, 2)).reshape(d, N_EXPERTS)], axis=1)
    b_all = jnp.concatenate([rg_b, re_b.reshape(N_EXPERTS)])
    half = LANES // 2
    pad = half - w_all.shape[1]
    w_hi = w_all.astype(BF16)
    w_lo = (w_all - w_hi.astype(F32)).astype(BF16)
    w_all = jnp.concatenate([jnp.pad(w_hi, ((0, 0), (0, pad))), jnp.pad(w_lo, ((0, 0), (0, pad)))], axis=1)
    b_all = jnp.pad(b_all, (0, LANES - b_all.shape[0])).reshape(1, LANES)
    return pl.pallas_call(
        _router_kernel,
        out_shape=(
            jax.ShapeDtypeStruct((SUBLANES, n_tok), F32),
            jax.ShapeDtypeStruct((n_tok, LANES), F32),
            jax.ShapeDtypeStruct((n_tok // tr, SUBLANES, LANES), F32),
        ),
        grid=(n_tok // tr,),
        in_specs=[
            pl.BlockSpec((tr, d), lambda i: (i, 0)),
            _const_spec((d, LANES)),
            _const_spec((1, LANES)),
        ],
        out_specs=(
            pl.BlockSpec((SUBLANES, tr), lambda i: (0, i)),
            pl.BlockSpec((tr, LANES), lambda i: (i, 0)),
            pl.BlockSpec((None, SUBLANES, LANES), lambda i: (i, 0, 0)),
        ),
        scratch_shapes=[pltpu.VMEM((tr, tr), BF16), pltpu.VMEM((LANES, LANES), BF16)],
        compiler_params=pltpu.CompilerParams(
            dimension_semantics=("arbitrary",), vmem_limit_bytes=VMEM_LIMIT),
        name="router",
    )(xt, w_all, b_all)


def _round_up(v, m):
    return ((v + m - 1) // m) * m


def _num_expert_blocks(n_tok):
    n_tiles = n_tok // MOE_TILE
    rows = n_tok * TOP_K + n_tiles * N_EXPERTS * (SEG_ALIGN - 1)
    return -(-rows // EXPERT_ROWS) + N_EXPERTS


def _routing_tables(cnt, n_tok):
    counts = cnt[:, 0, ROUTER_LANE0:ROUTER_LANE0 + N_EXPERTS].astype(I32)
    seg = _round_up(counts, SEG_ALIGN)
    tile_off = jnp.cumsum(seg, axis=1) - seg
    tot = jnp.sum(seg, axis=0)
    reg = _round_up(tot, EXPERT_ROWS)
    reg_end = jnp.cumsum(reg)
    reg_start = reg_end - reg
    glob_off = reg_start[None, :] + jnp.cumsum(seg, axis=0) - seg
    n_blocks = _num_expert_blocks(n_tok)
    blk_start = jnp.arange(n_blocks, dtype=I32) * EXPERT_ROWS
    block_e = jnp.sum((reg_end[None, :] <= blk_start[:, None]).astype(I32), axis=1)
    block_e = jnp.minimum(block_e, N_EXPERTS - 1).astype(I32)
    n_used = (reg_end[-1] // EXPERT_ROWS).astype(I32).reshape(1)
    groups = lambda a: (a.reshape(-1) // SEG_ALIGN).astype(I32)
    big = (jnp.max(seg, axis=1) >= INLINE_GROUPS * SEG_ALIGN).astype(I32)
    return dict(tile_off=groups(tile_off), glob_off=groups(glob_off), seg=groups(seg), big=big,
                tail_start=groups(reg_start + tot), tail_len=groups(reg - tot),
                block_e=block_e, n_used=n_used)


def _grouped(rows):
    return rows.reshape(rows.shape[0] // SEG_ALIGN, SEG_ALIGN, rows.shape[1])


def _ungrouped(groups):
    return groups.reshape(groups.shape[0] * SEG_ALIGN, groups.shape[2])


def _for_each_piece(length, max_len, fn):
    off = 0
    for size in reversed([1 << b for b in range(max_len.bit_length())]):
        take = (length & size) != 0

        @pl.when(take)
        def _(off=off, size=size):
            fn(off, size)

        off = off + jnp.where(take, size, 0)


def _start_segment_copies(copy, toff_ref, goff_ref, seg_ref, big_ref, t, live, inline):
    def per_expert(e, limit, enabled):
        idx = t * N_EXPERTS + e
        t0 = toff_ref[idx]
        g0 = goff_ref[idx]
        _for_each_piece(jnp.where(enabled, seg_ref[idx], 0), limit,
                        lambda off, size: copy(t0 + off, g0 + off, size))

    def looped(enabled):
        def body(e, c):
            per_expert(e, MOE_TILE // SEG_ALIGN, enabled)
            return c

        lax.fori_loop(0, N_EXPERTS, body, 0)

    if not inline:
        looped(live)
        return
    big = big_ref[t] != 0
    for e in range(N_EXPERTS):
        per_expert(e, INLINE_GROUPS - 1, jnp.logical_and(live, jnp.logical_not(big)))

    @pl.when(jnp.logical_and(live, big))
    def _():
        looped(True)


def _wait_segment_copies(wait, toff_ref, seg_ref, t):
    last = t * N_EXPERTS + N_EXPERTS - 1
    _for_each_piece(toff_ref[last] + seg_ref[last], SORT_ROWS // SEG_ALIGN, lambda off, size: wait(size))


def _dispatch_kernel(toff_ref, goff_ref, seg_ref, big_ref, tstart_ref, tlen_ref, nu_ref,
                     x_ref, rt_ref, xp_hbm, obuf, zbuf, sem, zsem):
    i = pl.program_id(0)
    n = pl.num_programs(0)
    tile = x_ref.shape[0]
    slot = i % 2

    def wait_segments(t, sl):
        _wait_segment_copies(
            lambda size: pltpu.make_async_copy(obuf.at[sl, pl.ds(0, size)], xp_hbm.at[pl.ds(0, size)],
                                               sem.at[sl]).wait(),
            toff_ref, seg_ref, t)

    def start_segments(t, sl, live, inline):
        _start_segment_copies(
            lambda tg, sg, size: pltpu.make_async_copy(
                obuf.at[sl, pl.ds(tg, size)], xp_hbm.at[pl.ds(sg, size)], sem.at[sl]).start(),
            toff_ref, goff_ref, seg_ref, big_ref, t, live, inline)

    @pl.when(i >= 2)
    def _():
        wait_segments(i - 2, slot)

    start_segments(jnp.maximum(i - 1, 0), 1 - slot, i >= 1, inline=True)

    pos1 = rt_ref[4:5, :].astype(I32)
    pos2 = rt_ref[5:6, :].astype(I32)
    xb = x_ref[...].astype(BF16)
    for c in range(SORT_ROWS // SORT_CHUNK):
        rows = lax.broadcasted_iota(I32, (SORT_CHUNK, tile), 0) + c * SORT_CHUNK
        hit = jnp.logical_or(rows == pos1, rows == pos2)
        g0, g1 = c * SORT_CHUNK // SEG_ALIGN, (c + 1) * SORT_CHUNK // SEG_ALIGN
        obuf[slot, g0:g1] = _grouped(jnp.dot(
            jnp.where(hit, 1.0, 0.0).astype(BF16), xb, preferred_element_type=F32).astype(BF16))

    def zero_fill(wait):
        def per_expert(e, c):
            def piece(off, size):
                cp = pltpu.make_async_copy(zbuf.at[pl.ds(0, size)],
                                           xp_hbm.at[pl.ds(tstart_ref[e] + off, size)], zsem)
                cp.wait() if wait else cp.start()

            _for_each_piece(tlen_ref[e], EXPERT_ROWS // SEG_ALIGN - 1, piece)
            return c

        lax.fori_loop(0, N_EXPERTS, per_expert, 0)

        zg = zbuf.shape[0]

        def per_zero_copy(h, c):
            cp = pltpu.make_async_copy(zbuf, xp_hbm.at[pl.ds(h * zg, zg)], zsem)
            cp.wait() if wait else cp.start()
            return c

        per_block = EXPERT_ROWS // SEG_ALIGN // zg
        lax.fori_loop(nu_ref[0] * per_block, xp_hbm.shape[0] // zg, per_zero_copy, 0)

    @pl.when(i == 0)
    def _():
        zbuf[...] = jnp.zeros(zbuf.shape, zbuf.dtype)
        zero_fill(False)

    @pl.when(i == n - 1)
    def _():
        start_segments(i, slot, True, inline=False)

        @pl.when(i >= 1)
        def _():
            wait_segments(i - 1, 1 - slot)

        wait_segments(i, slot)
        zero_fill(True)


def _dispatch(xt, rt, tabs):
    n_tok, d = xt.shape
    tile = MOE_TILE
    n_groups = _num_expert_blocks(n_tok) * EXPERT_ROWS // SEG_ALIGN
    return pl.pallas_call(
        _dispatch_kernel,
        out_shape=jax.ShapeDtypeStruct((n_groups, SEG_ALIGN, d), BF16),
        grid_spec=pltpu.PrefetchScalarGridSpec(
            num_scalar_prefetch=7,
            grid=(n_tok // tile,),
            in_specs=[
                pl.BlockSpec((tile, d), lambda i, *_: (i, 0)),
                pl.BlockSpec((SUBLANES, tile), lambda i, *_: (0, i)),
            ],
            out_specs=pl.BlockSpec(memory_space=pl.ANY),
            scratch_shapes=[
                pltpu.VMEM((2, SORT_ROWS // SEG_ALIGN, SEG_ALIGN, d), BF16),
                pltpu.VMEM((EXPERT_ROWS // SEG_ALIGN // 2, SEG_ALIGN, d), BF16),
                pltpu.SemaphoreType.DMA((2,)),
                pltpu.SemaphoreType.DMA,
            ],
        ),
        compiler_params=pltpu.CompilerParams(
            dimension_semantics=("arbitrary",), vmem_limit_bytes=VMEM_LIMIT),
        name="dispatch",
    )(tabs["tile_off"], tabs["glob_off"], tabs["seg"], tabs["big"], tabs["tail_start"],
      tabs["tail_len"], tabs["n_used"], xt, rt)


def _ffn_kernel(be_ref, nu_ref, x_ref, wg_ref, wu_ref, wd_ref, o_ref, wg_b, wu_b, wd_b):
    i = pl.program_id(0)
    used = i < nu_ref[0]

    @pl.when(jnp.logical_and(used, jnp.logical_or(i == 0, be_ref[i] != be_ref[jnp.maximum(i - 1, 0)])))
    def _():
        wg_b[...] = wg_ref[...].astype(BF16)
        wu_b[...] = wu_ref[...].astype(BF16)
        wd_b[...] = wd_ref[...].astype(BF16)

    @pl.when(used)
    def _():
        xb = _ungrouped(x_ref[...])
        g = jnp.dot(xb, wg_b[...], preferred_element_type=F32)
        u = jnp.dot(xb, wu_b[...], preferred_element_type=F32)
        hb = (g * jax.nn.sigmoid(g) * u).astype(BF16)
        o_ref[...] = _grouped(jnp.dot(hb, wd_b[...], preferred_element_type=F32).astype(BF16))

    @pl.when(jnp.logical_not(used))
    def _():
        o_ref[...] = jnp.zeros(o_ref.shape, o_ref.dtype)


def _expert_ffn(xp, block_e, n_used, w_gate, w_up, w_down, layer):
    n_groups, _, d = xp.shape
    d_e = w_gate.shape[3]
    blk_groups = EXPERT_ROWS // SEG_ALIGN
    n_blocks = n_groups // blk_groups
    blk = lambda i, be, nu: (jnp.minimum(i, nu[0] - 1), 0, 0)
    wsel = lambda i, be, nu: (layer, be[jnp.minimum(i, nu[0] - 1)], 0, 0)
    return pl.pallas_call(
        _ffn_kernel,
        out_shape=jax.ShapeDtypeStruct((n_groups, SEG_ALIGN, d), BF16),
        grid_spec=pltpu.PrefetchScalarGridSpec(
            num_scalar_prefetch=2,
            grid=(n_blocks,),
            in_specs=[
                pl.BlockSpec((blk_groups, SEG_ALIGN, d), blk),
                pl.BlockSpec((None, None, d, d_e), wsel),
                pl.BlockSpec((None, None, d, d_e), wsel),
                pl.BlockSpec((None, None, d_e, d), wsel),
            ],
            out_specs=pl.BlockSpec((blk_groups, SEG_ALIGN, d), lambda i, be, nu: (i, 0, 0)),
            scratch_shapes=[
                pltpu.VMEM((d, d_e), BF16),
                pltpu.VMEM((d, d_e), BF16),
                pltpu.VMEM((d_e, d), BF16),
            ],
        ),
        compiler_params=pltpu.CompilerParams(
            dimension_semantics=("arbitrary",), vmem_limit_bytes=VMEM_LIMIT),
        name="expert_ffn",
    )(block_e, n_used, xp, w_gate, w_up, w_down)


def _combine_kernel(toff_ref, goff_ref, seg_ref, big_ref, x_ref, rn_ref, yp_hbm, lng_ref, lnb_ref,
                    o_ref, ybuf, sem):
    i = pl.program_id(0)
    n = pl.num_programs(0)
    tile = x_ref.shape[0]
    slot = i % 2

    def wait_segments(t, sl):
        _wait_segment_copies(
            lambda size: pltpu.make_async_copy(yp_hbm.at[pl.ds(0, size)], ybuf.at[sl, pl.ds(0, size)],
                                               sem.at[sl]).wait(),
            toff_ref, seg_ref, t)

    def start_segments(t, sl, live, inline):
        _start_segment_copies(
            lambda tg, sg, size: pltpu.make_async_copy(
                yp_hbm.at[pl.ds(sg, size)], ybuf.at[sl, pl.ds(tg, size)], sem.at[sl]).start(),
            toff_ref, goff_ref, seg_ref, big_ref, t, live, inline)

    @pl.when(i == 0)
    def _():
        ybuf[...] = jnp.zeros(ybuf.shape, ybuf.dtype)
        start_segments(0, 0, True, inline=False)

    wait_segments(i, slot)
    start_segments(jnp.minimum(i + 1, n - 1), 1 - slot, i + 1 < n, inline=True)

    rn = rn_ref[...]
    full = lambda col: jnp.broadcast_to(col, (tile, LANES))
    w1, w2 = full(rn[:, 2:3]), full(rn[:, 3:4])
    pos1, pos2 = full(rn[:, 4:5]), full(rn[:, 5:6])
    lane = lax.broadcasted_iota(I32, (tile, LANES), 1).astype(F32)

    def pick_lanes(col0):
        cols = lane + float(col0)
        return jnp.where(cols == pos1, w1, 0.0) + jnp.where(cols == pos2, w2, 0.0)

    ffn = jnp.zeros(o_ref.shape, F32)
    for c in range(SORT_ROWS // SORT_CHUNK):
        pick = jnp.concatenate([pick_lanes(c * SORT_CHUNK + l0) for l0 in range(0, SORT_CHUNK, LANES)],
                               axis=1).astype(BF16)
        g0, g1 = c * SORT_CHUNK // SEG_ALIGN, (c + 1) * SORT_CHUNK // SEG_ALIGN
        ffn = ffn + jnp.dot(pick, _ungrouped(ybuf[slot, g0:g1]), preferred_element_type=F32)
    o_ref[...] = _layer_norm(ALPHA * x_ref[...] + ffn, lng_ref[...], lnb_ref[...])


def _combine(xt, rn, yp, tabs, ln_g, ln_b):
    n_tok, d = xt.shape
    tile = MOE_TILE
    return pl.pallas_call(
        _combine_kernel,
        out_shape=jax.ShapeDtypeStruct((n_tok, d), F32),
        grid_spec=pltpu.PrefetchScalarGridSpec(
            num_scalar_prefetch=4,
            grid=(n_tok // tile,),
            in_specs=[
                pl.BlockSpec((tile, d), lambda i, *_: (i, 0)),
                pl.BlockSpec((tile, LANES), lambda i, *_: (i, 0)),
                pl.BlockSpec(memory_space=pl.ANY),
                pl.BlockSpec((1, d), lambda i, *_: (0, 0)),
                pl.BlockSpec((1, d), lambda i, *_: (0, 0)),
            ],
            out_specs=pl.BlockSpec((tile, d), lambda i, *_: (i, 0)),
            scratch_shapes=[
                pltpu.VMEM((2, SORT_ROWS // SEG_ALIGN, SEG_ALIGN, d), BF16),
                pltpu.SemaphoreType.DMA((2,)),
            ],
        ),
        compiler_params=pltpu.CompilerParams(
            dimension_semantics=("arbitrary",), vmem_limit_bytes=VMEM_LIMIT),
        name="combine",
    )(tabs["tile_off"], tabs["glob_off"], tabs["seg"], tabs["big"], xt, rn, yp,
      ln_g.reshape(1, d), ln_b.reshape(1, d))


def _moe_layer(x, rg_w, rg_b, re_w, re_b, w_gate, w_up, w_down, layer, ln_g, ln_b):
    bt, s_len, d = x.shape
    n_tok = bt * s_len
    xt = x.reshape(n_tok, d)
    rt, rn, cnt = _router(xt, rg_w, rg_b, re_w, re_b)
    tabs = _routing_tables(cnt, n_tok)
    xp = _dispatch(xt, rt, tabs)
    yp = _expert_ffn(xp, tabs["block_e"], tabs["n_used"], w_gate, w_up, w_down, layer)
    out = _combine(xt, rn, yp, tabs, ln_g, ln_b)
    return out.reshape(bt, s_len, d)


def kernel(x, ab_w_in, ab_b_in, a_dw, a_dw_b, a_ln_g, a_ln_b, b_dw, ab_w_out, cd_w_in, cd_b_in, c_ln_g, c_ln_b, c_ws, c_ws_b, d_rel_bias, cd_w_out, mix_ln_g, mix_ln_b, moe_rg_w, moe_rg_b, moe_re_w, moe_re_b, moe_w_gate, moe_w_up, moe_w_down, ffn_ln_g, ffn_ln_b):
    for layer in range(DEPTH):
        i = layer // 2
        if layer % 2 == 0:
            x = _ab_mixer(x, ab_w_in[i], ab_b_in[i], a_dw[i], a_dw_b[i], a_ln_g[i], a_ln_b[i],
                          b_dw[i], ab_w_out[i], mix_ln_g[layer], mix_ln_b[layer])
        else:
            x = _cd_mixer(x, cd_w_in[i], cd_b_in[i], c_ln_g[i], c_ln_b[i], c_ws[i], c_ws_b[i],
                          d_rel_bias[i], cd_w_out[i], mix_ln_g[layer], mix_ln_b[layer])
        x = _moe_layer(x, moe_rg_w[layer], moe_rg_b[layer], moe_re_w[layer], moe_re_b[layer],
                       moe_w_gate, moe_w_up, moe_w_down, layer,
                       ffn_ln_g[layer], ffn_ln_b[layer])
    return x
```

```python
import functools

import jax
import jax.numpy as jnp
from jax import lax
from jax.experimental import pallas as pl
from jax.experimental.pallas import tpu as pltpu

F32 = jnp.float32
BF16 = jnp.bfloat16
I32 = jnp.int32

DEPTH = 2
CHUNK = 64
LEFT_CHUNKS = 8
HIST = LEFT_CHUNKS * CHUNK
CONV_A = 31
CONV_B = 3
HEADS_C = 8
SGU_BLOCK = 128
HEADS_D = 8
MAX_REL = 256
N_GROUPS = 4
EXPERTS_PER_GROUP = 8
N_EXPERTS = N_GROUPS * EXPERTS_PER_GROUP
TOP_K = 2
ALPHA = (2 * DEPTH) ** 0.25
LN_EPS = 1e-5
NEG_INF = -1e30
LOG2_E = 1.4426950408889634

LANES = 128
SUBLANES = 8
VMEM_LIMIT = 56 * 1024 * 1024

SEQ_TILE = 512
CONV_ROWS = 64
HALO = 32
Q_ROWS = 256
K_ROWS = Q_ROWS + HIST
MOE_TILE = 512
ROUTER_TILE = MOE_TILE
EXPERT_ROWS = 512
SEG_ALIGN = 2 * SUBLANES
SORT_ROWS = MOE_TILE * TOP_K + N_EXPERTS * SEG_ALIGN
SORT_CHUNK = 256
INLINE_GROUPS = 8
ROUTER_LANE0 = N_GROUPS
ROUTER_ROWS = -(-(N_GROUPS + N_EXPERTS) // SUBLANES) * SUBLANES
_B_SHIFTS = sorted({(HALO - (CONV_B - 1) + j) % SUBLANES for j in range(CONV_B)} - {0})


def _layer_norm(x, g, b):
    mu = jnp.mean(x, axis=-1, keepdims=True)
    xc = x - mu
    var = jnp.mean(xc * xc, axis=-1, keepdims=True)
    return xc * lax.rsqrt(var + LN_EPS) * g + b


def _const_spec(shape):
    nd = len(shape)
    return pl.BlockSpec(shape, lambda *_: (0,) * nd, pipeline_mode=pl.Buffered(1))


def _ab_mixer_kernel(x_ref, win_ref, bin_ref, adw_ref, adwb_ref, alng_ref, alnb_ref,
                     bdw_ref, wout_ref, lng_ref, lnb_ref, o_ref,
                     abuf, cbuf, gbuf, mbuf, asht, csht, *, d_a, d_b):
    ts = x_ref.shape[0]

    @pl.when(pl.program_id(1) == 0)
    def _():
        abuf[0:HALO, :] = jnp.zeros((HALO, d_a), F32)
        cbuf[0:HALO, :] = jnp.zeros((HALO, d_b), F32)

    x = x_ref[...]
    xb = x.astype(BF16)
    n_sh = HALO + ts - SUBLANES

    def tap(buf, sht, slot_of, r0, off):
        base = r0 + (off // SUBLANES) * SUBLANES
        if off % SUBLANES == 0:
            return buf[base:base + CONV_ROWS, :]
        return sht[slot_of(off % SUBLANES), base:base + CONV_ROWS, :]

    h_a = jnp.dot(xb, win_ref[:, 0:2 * d_a], preferred_element_type=F32) + bin_ref[:, 0:2 * d_a]
    abuf[HALO:HALO + ts, :] = h_a[:, 0:d_a] * jax.nn.sigmoid(h_a[:, d_a:])
    for sh in range(1, SUBLANES):
        asht[sh, 0:n_sh, :] = abuf[sh:sh + n_sh, :]
    for r0 in range(0, ts, CONV_ROWS):
        acc = jnp.broadcast_to(adwb_ref[...], (CONV_ROWS, d_a))
        for j in range(CONV_A):
            acc = acc + adw_ref[j:j + 1, :] * tap(abuf, asht, lambda sh: sh, r0, HALO - (CONV_A - 1) + j)
        a = _layer_norm(acc, alng_ref[...], alnb_ref[...])
        mbuf[r0:r0 + CONV_ROWS, 0:d_a] = (a * jax.nn.sigmoid(a)).astype(BF16)

    h_b = jnp.dot(xb, win_ref[:, 2 * d_a:], preferred_element_type=F32) + bin_ref[:, 2 * d_a:]
    gbuf[...] = h_b[:, 0:d_b]
    cbuf[HALO:HALO + ts, :] = h_b[:, d_b:2 * d_b] * h_b[:, 2 * d_b:]
    for sh in _B_SHIFTS:
        csht[_B_SHIFTS.index(sh), 0:n_sh, :] = cbuf[sh:sh + n_sh, :]
    for r0 in range(0, ts, CONV_ROWS):
        s = jnp.zeros((CONV_ROWS, d_b), F32)
        for j in range(CONV_B):
            s = s + bdw_ref[j:j + 1, :] * tap(cbuf, csht, _B_SHIFTS.index, r0, HALO - (CONV_B - 1) + j)
        mbuf[r0:r0 + CONV_ROWS, d_a:d_a + d_b] = (gbuf[r0:r0 + CONV_ROWS, :] * s).astype(BF16)

    abuf[0:HALO, :] = abuf[ts:ts + HALO, :]
    cbuf[0:HALO, :] = cbuf[ts:ts + HALO, :]

    mix = jnp.dot(mbuf[...], wout_ref[...], preferred_element_type=F32)
    o_ref[...] = _layer_norm(ALPHA * x + mix, lng_ref[...], lnb_ref[...])


def _ab_mixer(x, w_in, b_in, a_dw, a_dw_b, a_ln_g, a_ln_b, b_dw, w_out, ln_g, ln_b):
    bt, s_len, d = x.shape
    d_a = a_dw.shape[1]
    d_b = b_dw.shape[1]
    w_in_w = w_in.shape[1]
    ts = SEQ_TILE
    row = lambda v: v.reshape(1, -1)
    kern = functools.partial(_ab_mixer_kernel, d_a=d_a, d_b=d_b)
    return pl.pallas_call(
        kern,
        out_shape=jax.ShapeDtypeStruct((bt, s_len, d), F32),
        grid=(bt, s_len // ts),
        in_specs=[
            pl.BlockSpec((None, ts, d), lambda b, t: (b, t, 0)),
            _const_spec((d, w_in_w)),
            _const_spec((1, w_in_w)),
            _const_spec((CONV_A, d_a)),
            _const_spec((1, d_a)),
            _const_spec((1, d_a)),
            _const_spec((1, d_a)),
            _const_spec((CONV_B, d_b)),
            _const_spec((d_a + d_b, d)),
            _const_spec((1, d)),
            _const_spec((1, d)),
        ],
        out_specs=pl.BlockSpec((None, ts, d), lambda b, t: (b, t, 0)),
        scratch_shapes=[
            pltpu.VMEM((HALO + ts, d_a), F32),
            pltpu.VMEM((HALO + ts, d_b), F32),
            pltpu.VMEM((ts, d_b), F32),
            pltpu.VMEM((ts, d_a + d_b), BF16),
            pltpu.VMEM((SUBLANES, HALO + ts, d_a), F32),
            pltpu.VMEM((len(_B_SHIFTS), HALO + ts, d_b), F32),
        ],
        compiler_params=pltpu.CompilerParams(
            dimension_semantics=("arbitrary", "arbitrary"), vmem_limit_bytes=VMEM_LIMIT),
        name="ab_mixer",
    )(x, w_in.astype(BF16), row(b_in), a_dw, row(a_dw_b), row(a_ln_g), row(a_ln_b),
      b_dw, w_out.astype(BF16), row(ln_g), row(ln_b))


def _cd_mixer_kernel(x_ref, win_ref, bin_ref, clng_ref, clnb_ref, wsc_ref, wsb_ref,
                     bias_ref, wout_ref, lng_ref, lnb_ref, o_ref,
                     qbuf, kbuf, vbuf, wsm, vbd, mbuf, *, d_c, d_d):
    ts = x_ref.shape[0]
    t = pl.program_id(1)
    hc = d_c // HEADS_C
    hd = d_d // HEADS_D

    @pl.when(t == 0)
    def _():
        kbuf[0:HIST, :] = jnp.zeros((HIST, d_d), BF16)
        vbuf[0:HIST, :] = jnp.zeros((HIST, d_d), BF16)
        r = lax.broadcasted_iota(I32, (SGU_BLOCK, HEADS_C * SGU_BLOCK), 0)
        c = lax.broadcasted_iota(I32, (SGU_BLOCK, HEADS_C * SGU_BLOCK), 1) % SGU_BLOCK
        wsm[...] = jnp.where(c // CHUNK <= r // CHUNK, wsc_ref[...], 0.0).astype(BF16)

    x = x_ref[...]
    xb = x.astype(BF16)
    h_att = (jnp.dot(xb, win_ref[:, 2 * d_c:], preferred_element_type=F32) + bin_ref[:, 2 * d_c:])
    qbuf[...] = (h_att[:, 0:d_d] * (hd ** -0.5 * LOG2_E)).astype(BF16)
    kbuf[HIST:HIST + ts, :] = h_att[:, d_d:2 * d_d].astype(BF16)
    vbuf[HIST:HIST + ts, :] = h_att[:, 2 * d_d:].astype(BF16)

    def attention(qb):
        q0 = qb * Q_ROWS
        variant = jnp.where(t == 0, 1 + qb, 0)
        for hh in range(HEADS_D):
            q = qbuf[q0:q0 + Q_ROWS, hh * hd:(hh + 1) * hd]
            k = kbuf[q0:q0 + K_ROWS, hh * hd:(hh + 1) * hd]
            v = vbuf[q0:q0 + K_ROWS, hh * hd:(hh + 1) * hd]
            s = lax.dot_general(q, k, (((1,), (1,)), ((), ())), preferred_element_type=F32)
            s = s + bias_ref[variant, hh]
            p = jnp.exp2(s - jnp.max(s, axis=-1, keepdims=True))
            l = jnp.sum(p, axis=-1, keepdims=True)
            o = jnp.dot(p.astype(BF16), v, preferred_element_type=F32)
            mbuf[q0:q0 + Q_ROWS, d_c + hh * hd:d_c + (hh + 1) * hd] = (o / l).astype(BF16)

    for qb in range(ts // Q_ROWS):
        attention(qb)

    kbuf[0:HIST, :] = kbuf[ts:ts + HIST, :]
    vbuf[0:HIST, :] = vbuf[ts:ts + HIST, :]

    h_sgu = (jnp.dot(xb, win_ref[:, 0:2 * d_c], preferred_element_type=F32) + bin_ref[:, 0:2 * d_c])
    u = h_sgu[:, 0:d_c]
    vn = _layer_norm(h_sgu[:, d_c:], clng_ref[...], clnb_ref[...])
    lane_head = lax.broadcasted_iota(I32, (SGU_BLOCK, d_c), 1) // hc
    for nb in range(ts // SGU_BLOCK):
        v_blk = vn[nb * SGU_BLOCK:(nb + 1) * SGU_BLOCK, :]
        for hh in range(HEADS_C):
            vbd[hh * SGU_BLOCK:(hh + 1) * SGU_BLOCK, :] = jnp.where(
                lane_head == hh, v_blk, 0.0).astype(BF16)
        gate = jnp.dot(wsm[...], vbd[...], preferred_element_type=F32) + wsb_ref[...]
        mbuf[nb * SGU_BLOCK:(nb + 1) * SGU_BLOCK, 0:d_c] = (
            u[nb * SGU_BLOCK:(nb + 1) * SGU_BLOCK, :] * gate).astype(BF16)

    mix = jnp.dot(mbuf[...], wout_ref[...], preferred_element_type=F32)
    o_ref[...] = _layer_norm(ALPHA * x + mix, lng_ref[...], lnb_ref[...])


def _attention_bias(d_rel_bias):
    heads = d_rel_bias.shape[0]
    i = jnp.arange(Q_ROWS)[:, None]
    j = jnp.arange(K_ROWS)[None, :]
    jb = j - (i // CHUNK) * CHUNK
    in_band = jnp.logical_and(jb >= 0, jb < (LEFT_CHUNKS + 1) * CHUNK)
    n_diag = Q_ROWS + K_ROWS - 1
    m = jnp.arange(n_diag)
    diag = d_rel_bias[:, jnp.clip(Q_ROWS - 1 + HIST - m, -MAX_REL, MAX_REL) + MAX_REL].astype(F32)
    period = jnp.pad(diag, ((0, 0), (0, 1)))
    flat = jnp.tile(period, (1, Q_ROWS))[:, :Q_ROWS * n_diag]
    rel = flat.reshape(heads, Q_ROWS, n_diag)[:, :, Q_ROWS - 1:Q_ROWS - 1 + K_ROWS]
    rel = jnp.where(in_band[None], rel * LOG2_E, NEG_INF)
    variants = [rel] + [jnp.where(j[None] + qb * Q_ROWS >= HIST, rel, NEG_INF)
                        for qb in range(SEQ_TILE // Q_ROWS)]
    return jnp.stack(variants)


def _cd_mixer(x, w_in, b_in, c_ln_g, c_ln_b, c_ws, c_ws_b, d_rel_bias, w_out, ln_g, ln_b):
    bt, s_len, d = x.shape
    d_c = c_ln_g.shape[0]
    d_d = w_out.shape[0] - d_c
    w_in_w = w_in.shape[1]
    ts = SEQ_TILE
    assert ts == HIST and ts % Q_ROWS == 0 and ts % SGU_BLOCK == 0
    row = lambda v: v.reshape(1, -1)
    ws_cat = jnp.transpose(c_ws, (1, 0, 2)).reshape(SGU_BLOCK, HEADS_C * SGU_BLOCK)
    bias_full = jnp.repeat(c_ws_b.T, d_c // HEADS_C, axis=1)
    att_bias = _attention_bias(d_rel_bias)
    kern = functools.partial(_cd_mixer_kernel, d_c=d_c, d_d=d_d)
    return pl.pallas_call(
        kern,
        out_shape=jax.ShapeDtypeStruct((bt, s_len, d), F32),
        grid=(bt, s_len // ts),
        in_specs=[
            pl.BlockSpec((None, ts, d), lambda b, t: (b, t, 0)),
            _const_spec((d, w_in_w)),
            _const_spec((1, w_in_w)),
            _const_spec((1, d_c)),
            _const_spec((1, d_c)),
            _const_spec((SGU_BLOCK, HEADS_C * SGU_BLOCK)),
            _const_spec((SGU_BLOCK, d_c)),
            _const_spec((1 + ts // Q_ROWS, HEADS_D, Q_ROWS, K_ROWS)),
            _const_spec((d_c + d_d, d)),
            _const_spec((1, d)),
            _const_spec((1, d)),
        ],
        out_specs=pl.BlockSpec((None, ts, d), lambda b, t: (b, t, 0)),
        scratch_shapes=[
            pltpu.VMEM((ts, d_d), BF16),
            pltpu.VMEM((HIST + ts, d_d), BF16),
            pltpu.VMEM((HIST + ts, d_d), BF16),
            pltpu.VMEM((SGU_BLOCK, HEADS_C * SGU_BLOCK), BF16),
            pltpu.VMEM((HEADS_C * SGU_BLOCK, d_c), BF16),
            pltpu.VMEM((ts, d_c + d_d), BF16),
        ],
        compiler_params=pltpu.CompilerParams(
            dimension_semantics=("arbitrary", "arbitrary"), vmem_limit_bytes=VMEM_LIMIT),
        name="cd_mixer",
    )(x, w_in.astype(BF16), row(b_in), row(c_ln_g), row(c_ln_b), ws_cat, bias_full,
      att_bias, w_out.astype(BF16), row(ln_g), row(ln_b))


def _router_kernel(x_ref, wr_ref, bt_ref, rt_ref, rn_ref, cnt_ref, after_ref, below_ref):
    tr = x_ref.shape[0]
    nr = ROUTER_ROWS

    @pl.when(pl.program_id(0) == 0)
    def _():
        rr = lax.broadcasted_iota(I32, (tr, tr), 0)
        cc = lax.broadcasted_iota(I32, (tr, tr), 1)
        after_ref[...] = jnp.where(rr < cc, 1.0, 0.0).astype(BF16)
        lr = lax.broadcasted_iota(I32, (LANES, LANES), 0)
        lc = lax.broadcasted_iota(I32, (LANES, LANES), 1)
        below_ref[...] = jnp.where(lc < lr, 1.0, 0.0).astype(BF16)

    x = x_ref[...]
    x_hi = x.astype(BF16)
    x_lo = (x - x_hi.astype(F32)).astype(BF16)
    p_hi = jnp.dot(x_hi, wr_ref[...], preferred_element_type=F32)
    p_lo = jnp.dot(x_lo, wr_ref[...], preferred_element_type=F32)
    logits = (p_hi + pltpu.roll(p_hi, LANES // 2, axis=1) + p_lo).T[0:nr] + bt_ref[...]
    row = lax.broadcasted_iota(I32, (nr, tr), 0).astype(F32)

    def top1(vals):
        m = jnp.max(vals, axis=0, keepdims=True)
        idx = jnp.min(jnp.where(vals == m, row, float(nr)), axis=0, keepdims=True)
        return m, idx

    g_mask = row < N_GROUPS
    g_max, g_idx = top1(jnp.where(g_mask, logits, NEG_INF))
    g_top = 1.0 / jnp.sum(jnp.where(g_mask, jnp.exp(logits - g_max), 0.0), axis=0, keepdims=True)

    e_lo = ROUTER_LANE0 + EXPERTS_PER_GROUP * g_idx
    e_vals = jnp.where(jnp.logical_and(row >= e_lo, row < e_lo + EXPERTS_PER_GROUP), logits, NEG_INF)
    m1, i1 = top1(e_vals)
    m2, i2 = top1(jnp.where(row == i1, NEG_INF, e_vals))
    ratio = jnp.exp(m2 - m1)
    w1 = g_top / (1.0 + ratio)
    w2 = g_top * ratio / (1.0 + ratio)

    chosen = jnp.logical_or(row == i1, row == i2)
    onehot = jnp.where(chosen, 1.0, 0.0)
    seen = jnp.dot(onehot.astype(BF16), after_ref[...], preferred_element_type=F32)
    counts = jnp.broadcast_to(jnp.sum(onehot, axis=1, keepdims=True), (nr, LANES))
    cnt_ref[...] = counts

    groups = ((counts.astype(I32) + (SEG_ALIGN - 1)) // SEG_ALIGN).astype(F32).astype(BF16)
    groups = jnp.concatenate([groups, jnp.zeros((LANES - nr, LANES), BF16)], axis=0)
    seg_start = SEG_ALIGN * jnp.dot(below_ref[...], groups, preferred_element_type=F32)[0:nr]
    where_row = seen + jnp.concatenate([seg_start] * (tr // LANES), axis=1)
    pos1 = jnp.sum(jnp.where(row == i1, where_row, 0.0), axis=0, keepdims=True)
    pos2 = jnp.sum(jnp.where(row == i2, where_row, 0.0), axis=0, keepdims=True)

    fields = [i1 - ROUTER_LANE0, i2 - ROUTER_LANE0, w1, w2, pos1, pos2]
    res = jnp.concatenate(fields + [jnp.zeros((SUBLANES - len(fields), tr), F32)], axis=0)
    rt_ref[...] = res
    rn_ref[...] = jnp.concatenate([res, jnp.zeros((LANES - SUBLANES, tr), F32)], axis=0).T


def _router(xt, rg_w, rg_b, re_w, re_b):
    n_tok, d = xt.shape
    tr = ROUTER_TILE
    w_all = jnp.concatenate([rg_w, jnp.transpose(re_w, (1, 0, 2)).reshape(d, N_EXPERTS)], axis=1)
    b_all = jnp.concatenate([rg_b, re_b.reshape(N_EXPERTS)])
    half = LANES // 2
    pad = half - w_all.shape[1]
    w_hi = w_all.astype(BF16)
    w_lo = (w_all - w_hi.astype(F32)).astype(BF16)
    w_all = jnp.concatenate([jnp.pad(w_hi, ((0, 0), (0, pad))), jnp.pad(w_lo, ((0, 0), (0, pad)))], axis=1)
    b_t = jnp.broadcast_to(jnp.pad(b_all, (0, ROUTER_ROWS - b_all.shape[0]))[:, None], (ROUTER_ROWS, tr))
    return pl.pallas_call(
        _router_kernel,
        out_shape=(
            jax.ShapeDtypeStruct((SUBLANES, n_tok), F32),
            jax.ShapeDtypeStruct((n_tok, LANES), F32),
            jax.ShapeDtypeStruct((n_tok // tr, ROUTER_ROWS, LANES), F32),
        ),
        grid=(n_tok // tr,),
        in_specs=[
            pl.BlockSpec((tr, d), lambda i: (i, 0)),
            _const_spec((d, LANES)),
            _const_spec((ROUTER_ROWS, tr)),
        ],
        out_specs=(
            pl.BlockSpec((SUBLANES, tr), lambda i: (0, i)),
            pl.BlockSpec((tr, LANES), lambda i: (i, 0)),
            pl.BlockSpec((None, ROUTER_ROWS, LANES), lambda i: (i, 0, 0)),
        ),
        scratch_shapes=[pltpu.VMEM((tr, tr), BF16), pltpu.VMEM((LANES, LANES), BF16)],
        compiler_params=pltpu.CompilerParams(
            dimension_semantics=("arbitrary",), vmem_limit_bytes=VMEM_LIMIT),
        name="router",
    )(xt, w_all, b_t)


def _round_up(v, m):
    return ((v + m - 1) // m) * m


def _num_expert_blocks(n_tok):
    n_tiles = n_tok // MOE_TILE
    rows = n_tok * TOP_K + n_tiles * N_EXPERTS * (SEG_ALIGN - 1)
    return -(-rows // EXPERT_ROWS) + N_EXPERTS


def _routing_tables(cnt, n_tok):
    counts = cnt[:, ROUTER_LANE0:ROUTER_LANE0 + N_EXPERTS, 0].astype(I32)
    seg = _round_up(counts, SEG_ALIGN)
    tile_off = jnp.cumsum(seg, axis=1) - seg
    tot = jnp.sum(seg, axis=0)
    reg = _round_up(tot, EXPERT_ROWS)
    reg_end = jnp.cumsum(reg)
    reg_start = reg_end - reg
    glob_off = reg_start[None, :] + jnp.cumsum(seg, axis=0) - seg
    n_blocks = _num_expert_blocks(n_tok)
    blk_start = jnp.arange(n_blocks, dtype=I32) * EXPERT_ROWS
    block_e = jnp.sum((reg_end[None, :] <= blk_start[:, None]).astype(I32), axis=1)
    block_e = jnp.minimum(block_e, N_EXPERTS - 1).astype(I32)
    n_used = (reg_end[-1] // EXPERT_ROWS).astype(I32).reshape(1)
    groups = lambda a: (a.reshape(-1) // SEG_ALIGN).astype(I32)
    big = (jnp.max(seg, axis=1) >= INLINE_GROUPS * SEG_ALIGN).astype(I32)
    ar = jnp.arange(N_EXPERTS, dtype=I32)
    later = jnp.where(jnp.logical_and(ar[None, :] > ar[:, None], reg[None, :] > 0), ar[None, :], N_EXPERTS)
    next_of = jnp.min(later, axis=1)
    next_of = jnp.where(next_of < N_EXPERTS, next_of, ar)
    next_e = jnp.sum(jnp.where(block_e[:, None] == ar[None, :], next_of[None, :], 0), axis=1).astype(I32)
    return dict(tile_off=groups(tile_off), glob_off=groups(glob_off), seg=groups(seg), big=big,
                tail_start=groups(reg_start + tot), tail_len=groups(reg - tot),
                block_e=block_e, n_used=n_used, next_e=next_e)


def _grouped(rows):
    return rows.reshape(rows.shape[0] // SEG_ALIGN, SEG_ALIGN, rows.shape[1])


def _ungrouped(groups):
    return groups.reshape(groups.shape[0] * SEG_ALIGN, groups.shape[2])


def _for_each_piece(length, max_len, fn):
    off = 0
    for size in reversed([1 << b for b in range(max_len.bit_length())]):
        take = (length & size) != 0

        @pl.when(take)
        def _(off=off, size=size):
            fn(off, size)

        off = off + jnp.where(take, size, 0)


def _start_segment_copies(copy, toff_ref, goff_ref, seg_ref, big_ref, t, live, inline):
    def per_expert(e, limit, enabled):
        idx = t * N_EXPERTS + e
        t0 = toff_ref[idx]
        g0 = goff_ref[idx]
        _for_each_piece(jnp.where(enabled, seg_ref[idx], 0), limit,
                        lambda off, size: copy(t0 + off, g0 + off, size))

    def looped(enabled):
        def body(e, c):
            per_expert(e, MOE_TILE // SEG_ALIGN, enabled)
            return c

        lax.fori_loop(0, N_EXPERTS, body, 0)

    if not inline:
        looped(live)
        return
    big = big_ref[t] != 0
    for e in range(N_EXPERTS):
        per_expert(e, INLINE_GROUPS - 1, jnp.logical_and(live, jnp.logical_not(big)))

    @pl.when(jnp.logical_and(live, big))
    def _():
        looped(True)


def _wait_segment_copies(wait, toff_ref, seg_ref, t):
    last = t * N_EXPERTS + N_EXPERTS - 1
    _for_each_piece(toff_ref[last] + seg_ref[last], SORT_ROWS // SEG_ALIGN, lambda off, size: wait(size))


def _dispatch_kernel(toff_ref, goff_ref, seg_ref, big_ref, tstart_ref, tlen_ref, nu_ref,
                     x_ref, rt_ref, xp_hbm, obuf, zbuf, sem, zsem):
    i = pl.program_id(0)
    n = pl.num_programs(0)
    tile = x_ref.shape[0]
    slot = i % 2

    def wait_segments(t, sl):
        _wait_segment_copies(
            lambda size: pltpu.make_async_copy(obuf.at[sl, pl.ds(0, size)], xp_hbm.at[pl.ds(0, size)],
                                               sem.at[sl]).wait(),
            toff_ref, seg_ref, t)

    def start_segments(t, sl, live, inline):
        _start_segment_copies(
            lambda tg, sg, size: pltpu.make_async_copy(
                obuf.at[sl, pl.ds(tg, size)], xp_hbm.at[pl.ds(sg, size)], sem.at[sl]).start(),
            toff_ref, goff_ref, seg_ref, big_ref, t, live, inline)

    @pl.when(i >= 2)
    def _():
        wait_segments(i - 2, slot)

    start_segments(jnp.maximum(i - 1, 0), 1 - slot, i >= 1, inline=True)

    pos1 = rt_ref[4:5, :].astype(I32)
    pos2 = rt_ref[5:6, :].astype(I32)
    xb = x_ref[...].astype(BF16)
    for c in range(SORT_ROWS // SORT_CHUNK):
        rows = lax.broadcasted_iota(I32, (SORT_CHUNK, tile), 0) + c * SORT_CHUNK
        hit = jnp.logical_or(rows == pos1, rows == pos2)
        g0, g1 = c * SORT_CHUNK // SEG_ALIGN, (c + 1) * SORT_CHUNK // SEG_ALIGN
        obuf[slot, g0:g1] = _grouped(jnp.dot(
            jnp.where(hit, 1.0, 0.0).astype(BF16), xb, preferred_element_type=F32).astype(BF16))

    def zero_fill(wait):
        def per_expert(e, c):
            def piece(off, size):
                cp = pltpu.make_async_copy(zbuf.at[pl.ds(0, size)],
                                           xp_hbm.at[pl.ds(tstart_ref[e] + off, size)], zsem)
                cp.wait() if wait else cp.start()

            _for_each_piece(tlen_ref[e], EXPERT_ROWS // SEG_ALIGN - 1, piece)
            return c

        lax.fori_loop(0, N_EXPERTS, per_expert, 0)

        zg = zbuf.shape[0]

        def per_zero_copy(h, c):
            cp = pltpu.make_async_copy(zbuf, xp_hbm.at[pl.ds(h * zg, zg)], zsem)
            cp.wait() if wait else cp.start()
            return c

        per_block = EXPERT_ROWS // SEG_ALIGN // zg
        lax.fori_loop(nu_ref[0] * per_block, xp_hbm.shape[0] // zg, per_zero_copy, 0)

    @pl.when(i == 0)
    def _():
        zbuf[...] = jnp.zeros(zbuf.shape, zbuf.dtype)
        zero_fill(False)

    @pl.when(i == n - 1)
    def _():
        start_segments(i, slot, True, inline=False)

        @pl.when(i >= 1)
        def _():
            wait_segments(i - 1, 1 - slot)

        wait_segments(i, slot)
        zero_fill(True)


def _dispatch(xt, rt, tabs):
    n_tok, d = xt.shape
    tile = MOE_TILE
    n_groups = _num_expert_blocks(n_tok) * EXPERT_ROWS // SEG_ALIGN
    return pl.pallas_call(
        _dispatch_kernel,
        out_shape=jax.ShapeDtypeStruct((n_groups, SEG_ALIGN, d), BF16),
        grid_spec=pltpu.PrefetchScalarGridSpec(
            num_scalar_prefetch=7,
            grid=(n_tok // tile,),
            in_specs=[
                pl.BlockSpec((tile, d), lambda i, *_: (i, 0)),
                pl.BlockSpec((SUBLANES, tile), lambda i, *_: (0, i)),
            ],
            out_specs=pl.BlockSpec(memory_space=pl.ANY),
            scratch_shapes=[
                pltpu.VMEM((2, SORT_ROWS // SEG_ALIGN, SEG_ALIGN, d), BF16),
                pltpu.VMEM((EXPERT_ROWS // SEG_ALIGN // 2, SEG_ALIGN, d), BF16),
                pltpu.SemaphoreType.DMA((2,)),
                pltpu.SemaphoreType.DMA,
            ],
        ),
        compiler_params=pltpu.CompilerParams(
            dimension_semantics=("arbitrary",), vmem_limit_bytes=VMEM_LIMIT),
        name="dispatch",
    )(tabs["tile_off"], tabs["glob_off"], tabs["seg"], tabs["big"], tabs["tail_start"],
      tabs["tail_len"], tabs["n_used"], xt, rt)


def _ffn_kernel(be_ref, nu_ref, nxt_ref, x_ref, wg_hbm, wu_hbm, wd_hbm, o_ref,
                wg_f, wu_f, wd_f, wg_b, wu_b, wd_b, sem, *, layer):
    i = pl.program_id(0)
    used = i < nu_ref[0]
    e = be_ref[i]

    def weight_copies(expert):
        return [pltpu.make_async_copy(src.at[layer, expert], dst, sem.at[k])
                for k, (src, dst) in enumerate(((wg_hbm, wg_f), (wu_hbm, wu_f), (wd_hbm, wd_f)))]

    @pl.when(i == 0)
    def _():
        for cp in weight_copies(e):
            cp.start()

    @pl.when(jnp.logical_and(used, jnp.logical_or(i == 0, e != be_ref[jnp.maximum(i - 1, 0)])))
    def _():
        for cp in weight_copies(e):
            cp.wait()
        wg_b[...] = wg_f[...].astype(BF16)
        wu_b[...] = wu_f[...].astype(BF16)
        wd_b[...] = wd_f[...].astype(BF16)

        @pl.when(nxt_ref[i] != e)
        def _():
            for cp in weight_copies(nxt_ref[i]):
                cp.start()

    @pl.when(used)
    def _():
        xb = _ungrouped(x_ref[...])
        g = jnp.dot(xb, wg_b[...], preferred_element_type=F32)
        u = jnp.dot(xb, wu_b[...], preferred_element_type=F32)
        hb = (g * jax.nn.sigmoid(g) * u).astype(BF16)
        o_ref[...] = _grouped(jnp.dot(hb, wd_b[...], preferred_element_type=F32).astype(BF16))

    @pl.when(jnp.logical_not(used))
    def _():
        o_ref[...] = jnp.zeros(o_ref.shape, o_ref.dtype)


def _expert_ffn(xp, tabs, w_gate, w_up, w_down, layer):
    n_groups, _, d = xp.shape
    d_e = w_gate.shape[3]
    blk_groups = EXPERT_ROWS // SEG_ALIGN
    n_blocks = n_groups // blk_groups
    blk = lambda i, be, nu, nxt: (jnp.minimum(i, nu[0] - 1), 0, 0)
    return pl.pallas_call(
        functools.partial(_ffn_kernel, layer=layer),
        out_shape=jax.ShapeDtypeStruct((n_groups, SEG_ALIGN, d), BF16),
        grid_spec=pltpu.PrefetchScalarGridSpec(
            num_scalar_prefetch=3,
            grid=(n_blocks,),
            in_specs=[
                pl.BlockSpec((blk_groups, SEG_ALIGN, d), blk),
                pl.BlockSpec(memory_space=pl.ANY),
                pl.BlockSpec(memory_space=pl.ANY),
                pl.BlockSpec(memory_space=pl.ANY),
            ],
            out_specs=pl.BlockSpec((blk_groups, SEG_ALIGN, d), lambda i, be, nu, nxt: (i, 0, 0)),
            scratch_shapes=[
                pltpu.VMEM((d, d_e), F32),
                pltpu.VMEM((d, d_e), F32),
                pltpu.VMEM((d_e, d), F32),
                pltpu.VMEM((d, d_e), BF16),
                pltpu.VMEM((d, d_e), BF16),
                pltpu.VMEM((d_e, d), BF16),
                pltpu.SemaphoreType.DMA((3,)),
            ],
        ),
        compiler_params=pltpu.CompilerParams(
            dimension_semantics=("arbitrary",), vmem_limit_bytes=VMEM_LIMIT),
        name="expert_ffn",
    )(tabs["block_e"], tabs["n_used"], tabs["next_e"], xp, w_gate, w_up, w_down)


def _combine_kernel(toff_ref, goff_ref, seg_ref, big_ref, x_ref, rn_ref, yp_hbm, lng_ref, lnb_ref,
                    o_ref, ybuf, sem):
    i = pl.program_id(0)
    n = pl.num_programs(0)
    tile = x_ref.shape[0]
    slot = i % 2

    def wait_segments(t, sl):
        _wait_segment_copies(
            lambda size: pltpu.make_async_copy(yp_hbm.at[pl.ds(0, size)], ybuf.at[sl, pl.ds(0, size)],
                                               sem.at[sl]).wait(),
            toff_ref, seg_ref, t)

    def start_segments(t, sl, live, inline):
        _start_segment_copies(
            lambda tg, sg, size: pltpu.make_async_copy(
                yp_hbm.at[pl.ds(sg, size)], ybuf.at[sl, pl.ds(tg, size)], sem.at[sl]).start(),
            toff_ref, goff_ref, seg_ref, big_ref, t, live, inline)

    @pl.when(i == 0)
    def _():
        ybuf[...] = jnp.zeros(ybuf.shape, ybuf.dtype)
        start_segments(0, 0, True, inline=False)

    wait_segments(i, slot)
    start_segments(jnp.minimum(i + 1, n - 1), 1 - slot, i + 1 < n, inline=True)

    rn = rn_ref[...]
    full = lambda col: jnp.broadcast_to(col, (tile, LANES))
    w1, w2 = full(rn[:, 2:3]), full(rn[:, 3:4])
    pos1, pos2 = full(rn[:, 4:5]), full(rn[:, 5:6])
    lane = lax.broadcasted_iota(I32, (tile, LANES), 1).astype(F32)

    def pick_lanes(col0):
        cols = lane + float(col0)
        return jnp.where(cols == pos1, w1, 0.0) + jnp.where(cols == pos2, w2, 0.0)

    ffn = jnp.zeros(o_ref.shape, F32)
    for c in range(SORT_ROWS // SORT_CHUNK):
        pick = jnp.concatenate([pick_lanes(c * SORT_CHUNK + l0) for l0 in range(0, SORT_CHUNK, LANES)],
                               axis=1).astype(BF16)
        g0, g1 = c * SORT_CHUNK // SEG_ALIGN, (c + 1) * SORT_CHUNK // SEG_ALIGN
        ffn = ffn + jnp.dot(pick, _ungrouped(ybuf[slot, g0:g1]), preferred_element_type=F32)
    o_ref[...] = _layer_norm(ALPHA * x_ref[...] + ffn, lng_ref[...], lnb_ref[...])


def _combine(xt, rn, yp, tabs, ln_g, ln_b):
    n_tok, d = xt.shape
    tile = MOE_TILE
    return pl.pallas_call(
        _combine_kernel,
        out_shape=jax.ShapeDtypeStruct((n_tok, d), F32),
        grid_spec=pltpu.PrefetchScalarGridSpec(
            num_scalar_prefetch=4,
            grid=(n_tok // tile,),
            in_specs=[
                pl.BlockSpec((tile, d), lambda i, *_: (i, 0)),
                pl.BlockSpec((tile, LANES), lambda i, *_: (i, 0)),
                pl.BlockSpec(memory_space=pl.ANY),
                pl.BlockSpec((1, d), lambda i, *_: (0, 0)),
                pl.BlockSpec((1, d), lambda i, *_: (0, 0)),
            ],
            out_specs=pl.BlockSpec((tile, d), lambda i, *_: (i, 0)),
            scratch_shapes=[
                pltpu.VMEM((2, SORT_ROWS // SEG_ALIGN, SEG_ALIGN, d), BF16),
                pltpu.SemaphoreType.DMA((2,)),
            ],
        ),
        compiler_params=pltpu.CompilerParams(
            dimension_semantics=("arbitrary",), vmem_limit_bytes=VMEM_LIMIT),
        name="combine",
    )(tabs["tile_off"], tabs["glob_off"], tabs["seg"], tabs["big"], xt, rn, yp,
      ln_g.reshape(1, d), ln_b.reshape(1, d))


def _moe_layer(x, rg_w, rg_b, re_w, re_b, w_gate, w_up, w_down, layer, ln_g, ln_b):
    bt, s_len, d = x.shape
    n_tok = bt * s_len
    xt = x.reshape(n_tok, d)
    rt, rn, cnt = _router(xt, rg_w, rg_b, re_w, re_b)
    tabs = _routing_tables(cnt, n_tok)
    xp = _dispatch(xt, rt, tabs)
    yp = _expert_ffn(xp, tabs, w_gate, w_up, w_down, layer)
    out = _combine(xt, rn, yp, tabs, ln_g, ln_b)
    return out.reshape(bt, s_len, d)


def kernel(x, ab_w_in, ab_b_in, a_dw, a_dw_b, a_ln_g, a_ln_b, b_dw, ab_w_out, cd_w_in, cd_b_in, c_ln_g, c_ln_b, c_ws, c_ws_b, d_rel_bias, cd_w_out, mix_ln_g, mix_ln_b, moe_rg_w, moe_rg_b, moe_re_w, moe_re_b, moe_w_gate, moe_w_up, moe_w_down, ffn_ln_g, ffn_ln_b):
    for layer in range(DEPTH):
        i = layer // 2
        if layer % 2 == 0:
            x = _ab_mixer(x, ab_w_in[i], ab_b_in[i], a_dw[i], a_dw_b[i], a_ln_g[i], a_ln_b[i],
                          b_dw[i], ab_w_out[i], mix_ln_g[layer], mix_ln_b[layer])
        else:
            x = _cd_mixer(x, cd_w_in[i], cd_b_in[i], c_ln_g[i], c_ln_b[i], c_ws[i], c_ws_b[i],
                          d_rel_bias[i], cd_w_out[i], mix_ln_g[layer], mix_ln_b[layer])
        x = _moe_layer(x, moe_rg_w[layer], moe_rg_b[layer], moe_re_w[layer], moe_re_b[layer],
                       moe_w_gate, moe_w_up, moe_w_down, layer,
                       ffn_ln_g[layer], ffn_ln_b[layer])
    return x
```

```python
import functools

import jax
import jax.numpy as jnp
from jax import lax
from jax.experimental import pallas as pl
from jax.experimental.pallas import tpu as pltpu

F32 = jnp.float32
BF16 = jnp.bfloat16
I32 = jnp.int32

DEPTH = 2
CHUNK = 64
LEFT_CHUNKS = 8
HIST = LEFT_CHUNKS * CHUNK
CONV_A = 31
CONV_B = 3
HEADS_C = 8
SGU_BLOCK = 128
HEADS_D = 8
MAX_REL = 256
N_GROUPS = 4
EXPERTS_PER_GROUP = 8
N_EXPERTS = N_GROUPS * EXPERTS_PER_GROUP
TOP_K = 2
ALPHA = (2 * DEPTH) ** 0.25
LN_EPS = 1e-5
NEG_INF = -1e30
LOG2_E = 1.4426950408889634

LANES = 128
SUBLANES = 8
VMEM_LIMIT = 56 * 1024 * 1024

SEQ_TILE = 512
CONV_ROWS = 64
HALO = 32
Q_ROWS = 256
K_ROWS = Q_ROWS + HIST
MOE_TILE = 512
ROUTER_TILE = MOE_TILE
EXPERT_ROWS = 512
SEG_ALIGN = 2 * SUBLANES
SORT_ROWS = MOE_TILE * TOP_K + N_EXPERTS * SEG_ALIGN
SORT_CHUNK = 256
INLINE_GROUPS = 8
ROUTER_LANE0 = N_GROUPS
ROUTER_ROWS = -(-(N_GROUPS + N_EXPERTS) // SUBLANES) * SUBLANES
_B_SHIFTS = sorted({(HALO - (CONV_B - 1) + j) % SUBLANES for j in range(CONV_B)} - {0})


def _layer_norm(x, g, b):
    mu = jnp.mean(x, axis=-1, keepdims=True)
    xc = x - mu
    var = jnp.mean(xc * xc, axis=-1, keepdims=True)
    return xc * lax.rsqrt(var + LN_EPS) * g + b


def _const_spec(shape):
    nd = len(shape)
    return pl.BlockSpec(shape, lambda *_: (0,) * nd, pipeline_mode=pl.Buffered(1))


def _ab_mixer_kernel(x_ref, win_ref, bin_ref, adw_ref, adwb_ref, alng_ref, alnb_ref,
                     bdw_ref, wout_ref, lng_ref, lnb_ref, o_ref,
                     abuf, cbuf, gbuf, mbuf, asht, csht, *, d_a, d_b):
    ts = x_ref.shape[0]

    @pl.when(pl.program_id(1) == 0)
    def _():
        abuf[0:HALO, :] = jnp.zeros((HALO, d_a), F32)
        cbuf[0:HALO, :] = jnp.zeros((HALO, d_b), F32)

    x = x_ref[...]
    xb = x.astype(BF16)
    n_sh = HALO + ts - SUBLANES

    def tap(buf, sht, slot_of, r0, off):
        base = r0 + (off // SUBLANES) * SUBLANES
        if off % SUBLANES == 0:
            return buf[base:base + CONV_ROWS, :]
        return sht[slot_of(off % SUBLANES), base:base + CONV_ROWS, :]

    h_a = jnp.dot(xb, win_ref[:, 0:2 * d_a], preferred_element_type=F32) + bin_ref[:, 0:2 * d_a]
    abuf[HALO:HALO + ts, :] = h_a[:, 0:d_a] * jax.nn.sigmoid(h_a[:, d_a:])
    for sh in range(1, SUBLANES):
        asht[sh, 0:n_sh, :] = abuf[sh:sh + n_sh, :]
    for r0 in range(0, ts, CONV_ROWS):
        acc = jnp.broadcast_to(adwb_ref[...], (CONV_ROWS, d_a))
        for j in range(CONV_A):
            acc = acc + adw_ref[j:j + 1, :] * tap(abuf, asht, lambda sh: sh, r0, HALO - (CONV_A - 1) + j)
        a = _layer_norm(acc, alng_ref[...], alnb_ref[...])
        mbuf[r0:r0 + CONV_ROWS, 0:d_a] = (a * jax.nn.sigmoid(a)).astype(BF16)

    h_b = jnp.dot(xb, win_ref[:, 2 * d_a:], preferred_element_type=F32) + bin_ref[:, 2 * d_a:]
    gbuf[...] = h_b[:, 0:d_b]
    cbuf[HALO:HALO + ts, :] = h_b[:, d_b:2 * d_b] * h_b[:, 2 * d_b:]
    for sh in _B_SHIFTS:
        csht[_B_SHIFTS.index(sh), 0:n_sh, :] = cbuf[sh:sh + n_sh, :]
    for r0 in range(0, ts, CONV_ROWS):
        s = jnp.zeros((CONV_ROWS, d_b), F32)
        for j in range(CONV_B):
            s = s + bdw_ref[j:j + 1, :] * tap(cbuf, csht, _B_SHIFTS.index, r0, HALO - (CONV_B - 1) + j)
        mbuf[r0:r0 + CONV_ROWS, d_a:d_a + d_b] = (gbuf[r0:r0 + CONV_ROWS, :] * s).astype(BF16)

    abuf[0:HALO, :] = abuf[ts:ts + HALO, :]
    cbuf[0:HALO, :] = cbuf[ts:ts + HALO, :]

    mix = jnp.dot(mbuf[...], wout_ref[...], preferred_element_type=F32)
    o_ref[...] = _layer_norm(ALPHA * x + mix, lng_ref[...], lnb_ref[...])


def _ab_mixer(x, w_in, b_in, a_dw, a_dw_b, a_ln_g, a_ln_b, b_dw, w_out, ln_g, ln_b):
    bt, s_len, d = x.shape
    d_a = a_dw.shape[1]
    d_b = b_dw.shape[1]
    w_in_w = w_in.shape[1]
    ts = SEQ_TILE
    row = lambda v: v.reshape(1, -1)
    kern = functools.partial(_ab_mixer_kernel, d_a=d_a, d_b=d_b)
    return pl.pallas_call(
        kern,
        out_shape=jax.ShapeDtypeStruct((bt, s_len, d), F32),
        grid=(bt, s_len // ts),
        in_specs=[
            pl.BlockSpec((None, ts, d), lambda b, t: (b, t, 0)),
            _const_spec((d, w_in_w)),
            _const_spec((1, w_in_w)),
            _const_spec((CONV_A, d_a)),
            _const_spec((1, d_a)),
            _const_spec((1, d_a)),
            _const_spec((1, d_a)),
            _const_spec((CONV_B, d_b)),
            _const_spec((d_a + d_b, d)),
            _const_spec((1, d)),
            _const_spec((1, d)),
        ],
        out_specs=pl.BlockSpec((None, ts, d), lambda b, t: (b, t, 0)),
        scratch_shapes=[
            pltpu.VMEM((HALO + ts, d_a), F32),
            pltpu.VMEM((HALO + ts, d_b), F32),
            pltpu.VMEM((ts, d_b), F32),
            pltpu.VMEM((ts, d_a + d_b), BF16),
            pltpu.VMEM((SUBLANES, HALO + ts, d_a), F32),
            pltpu.VMEM((len(_B_SHIFTS), HALO + ts, d_b), F32),
        ],
        compiler_params=pltpu.CompilerParams(
            dimension_semantics=("arbitrary", "arbitrary"), vmem_limit_bytes=VMEM_LIMIT),
        name="ab_mixer",
    )(x, w_in.astype(BF16), row(b_in), a_dw, row(a_dw_b), row(a_ln_g), row(a_ln_b),
      b_dw, w_out.astype(BF16), row(ln_g), row(ln_b))


def _cd_mixer_kernel(x_ref, xn_ref, win_ref, bin_ref, clng_ref, clnb_ref, wsc_ref, wsb_ref,
                     bias_ref, wout_ref, lng_ref, lnb_ref, o_ref,
                     qbuf, qnext, kbuf, vbuf, wsm, vbd, mbuf, *, d_c, d_d, tiles_per_seq):
    ts = x_ref.shape[0]
    g = pl.program_id(0)
    first_in_seq = g % tiles_per_seq == 0
    hc = d_c // HEADS_C
    hd = d_d // HEADS_D

    def project_attention(src_ref, q_dst, row0):
        h_att = (jnp.dot(src_ref[...].astype(BF16), win_ref[:, 2 * d_c:], preferred_element_type=F32)
                 + bin_ref[:, 2 * d_c:])
        q_dst[...] = (h_att[:, 0:d_d] * (hd ** -0.5 * LOG2_E)).astype(BF16)
        kbuf[row0:row0 + ts, :] = h_att[:, d_d:2 * d_d].astype(BF16)
        vbuf[row0:row0 + ts, :] = h_att[:, 2 * d_d:].astype(BF16)

    @pl.when(g == 0)
    def _():
        kbuf[0:HIST, :] = jnp.zeros((HIST, d_d), BF16)
        vbuf[0:HIST, :] = jnp.zeros((HIST, d_d), BF16)
        project_attention(x_ref, qbuf, HIST)
        r = lax.broadcasted_iota(I32, (SGU_BLOCK, HEADS_C * SGU_BLOCK), 0)
        c = lax.broadcasted_iota(I32, (SGU_BLOCK, HEADS_C * SGU_BLOCK), 1) % SGU_BLOCK
        wsm[...] = jnp.where(c // CHUNK <= r // CHUNK, wsc_ref[...], 0.0).astype(BF16)

    x = x_ref[...]
    xb = x.astype(BF16)
    project_attention(xn_ref, qnext, HIST + ts)

    def attention(qb):
        q0 = qb * Q_ROWS
        col = lax.broadcasted_iota(I32, (1, K_ROWS), 1)
        no_key = jnp.where(jnp.logical_and(first_in_seq, col + q0 < HIST), NEG_INF, 0.0)
        for hh in range(HEADS_D):
            q = qbuf[q0:q0 + Q_ROWS, hh * hd:(hh + 1) * hd]
            k = kbuf[q0:q0 + K_ROWS, hh * hd:(hh + 1) * hd]
            v = vbuf[q0:q0 + K_ROWS, hh * hd:(hh + 1) * hd]
            s = lax.dot_general(q, k, (((1,), (1,)), ((), ())), preferred_element_type=F32)
            s = s + bias_ref[hh] + no_key
            p = jnp.exp2(s - jnp.max(s, axis=-1, keepdims=True))
            l = jnp.sum(p, axis=-1, keepdims=True)
            o = jnp.dot(p.astype(BF16), v, preferred_element_type=F32)
            mbuf[q0:q0 + Q_ROWS, d_c + hh * hd:d_c + (hh + 1) * hd] = (o / l).astype(BF16)

    for qb in range(ts // Q_ROWS):
        attention(qb)

    h_sgu = (jnp.dot(xb, win_ref[:, 0:2 * d_c], preferred_element_type=F32) + bin_ref[:, 0:2 * d_c])
    u = h_sgu[:, 0:d_c]
    vn = _layer_norm(h_sgu[:, d_c:], clng_ref[...], clnb_ref[...])
    lane_head = lax.broadcasted_iota(I32, (SGU_BLOCK, d_c), 1) // hc
    for nb in range(ts // SGU_BLOCK):
        v_blk = vn[nb * SGU_BLOCK:(nb + 1) * SGU_BLOCK, :]
        for hh in range(HEADS_C):
            vbd[hh * SGU_BLOCK:(hh + 1) * SGU_BLOCK, :] = jnp.where(
                lane_head == hh, v_blk, 0.0).astype(BF16)
        gate = jnp.dot(wsm[...], vbd[...], preferred_element_type=F32) + wsb_ref[...]
        mbuf[nb * SGU_BLOCK:(nb + 1) * SGU_BLOCK, 0:d_c] = (
            u[nb * SGU_BLOCK:(nb + 1) * SGU_BLOCK, :] * gate).astype(BF16)

    mix = jnp.dot(mbuf[...], wout_ref[...], preferred_element_type=F32)
    o_ref[...] = _layer_norm(ALPHA * x + mix, lng_ref[...], lnb_ref[...])

    kbuf[0:HIST + ts, :] = kbuf[ts:HIST + 2 * ts, :]
    vbuf[0:HIST + ts, :] = vbuf[ts:HIST + 2 * ts, :]
    qbuf[...] = qnext[...]


def _attention_bias(d_rel_bias):
    heads = d_rel_bias.shape[0]
    i = jnp.arange(Q_ROWS)[:, None]
    j = jnp.arange(K_ROWS)[None, :]
    jb = j - (i // CHUNK) * CHUNK
    in_band = jnp.logical_and(jb >= 0, jb < (LEFT_CHUNKS + 1) * CHUNK)
    n_diag = Q_ROWS + K_ROWS - 1
    m = jnp.arange(n_diag)
    diag = d_rel_bias[:, jnp.clip(Q_ROWS - 1 + HIST - m, -MAX_REL, MAX_REL) + MAX_REL].astype(F32)
    period = jnp.pad(diag, ((0, 0), (0, 1)))
    flat = jnp.tile(period, (1, Q_ROWS))[:, :Q_ROWS * n_diag]
    rel = flat.reshape(heads, Q_ROWS, n_diag)[:, :, Q_ROWS - 1:Q_ROWS - 1 + K_ROWS]
    return jnp.where(in_band[None], rel * LOG2_E, NEG_INF)


def _cd_mixer(x, w_in, b_in, c_ln_g, c_ln_b, c_ws, c_ws_b, d_rel_bias, w_out, ln_g, ln_b):
    bt, s_len, d = x.shape
    d_c = c_ln_g.shape[0]
    d_d = w_out.shape[0] - d_c
    w_in_w = w_in.shape[1]
    ts = SEQ_TILE
    assert ts == HIST and ts % Q_ROWS == 0 and ts % SGU_BLOCK == 0
    row = lambda v: v.reshape(1, -1)
    ws_cat = jnp.transpose(c_ws, (1, 0, 2)).reshape(SGU_BLOCK, HEADS_C * SGU_BLOCK)
    bias_full = jnp.repeat(c_ws_b.T, d_c // HEADS_C, axis=1)
    att_bias = _attention_bias(d_rel_bias)
    n_tiles = bt * s_len // ts
    kern = functools.partial(_cd_mixer_kernel, d_c=d_c, d_d=d_d, tiles_per_seq=s_len // ts)
    xt = x.reshape(bt * s_len, d)
    out = pl.pallas_call(
        kern,
        out_shape=jax.ShapeDtypeStruct((bt * s_len, d), F32),
        grid=(n_tiles,),
        in_specs=[
            pl.BlockSpec((ts, d), lambda g: (g, 0)),
            pl.BlockSpec((ts, d), lambda g: (jnp.minimum(g + 1, n_tiles - 1), 0)),
            _const_spec((d, w_in_w)),
            _const_spec((1, w_in_w)),
            _const_spec((1, d_c)),
            _const_spec((1, d_c)),
            _const_spec((SGU_BLOCK, HEADS_C * SGU_BLOCK)),
            _const_spec((SGU_BLOCK, d_c)),
            _const_spec((HEADS_D, Q_ROWS, K_ROWS)),
            _const_spec((d_c + d_d, d)),
            _const_spec((1, d)),
            _const_spec((1, d)),
        ],
        out_specs=pl.BlockSpec((ts, d), lambda g: (g, 0)),
        scratch_shapes=[
            pltpu.VMEM((ts, d_d), BF16),
            pltpu.VMEM((ts, d_d), BF16),
            pltpu.VMEM((HIST + 2 * ts, d_d), BF16),
            pltpu.VMEM((HIST + 2 * ts, d_d), BF16),
            pltpu.VMEM((SGU_BLOCK, HEADS_C * SGU_BLOCK), BF16),
            pltpu.VMEM((HEADS_C * SGU_BLOCK, d_c), BF16),
            pltpu.VMEM((ts, d_c + d_d), BF16),
        ],
        compiler_params=pltpu.CompilerParams(
            dimension_semantics=("arbitrary",), vmem_limit_bytes=VMEM_LIMIT),
        name="cd_mixer",
    )(xt, xt, w_in.astype(BF16), row(b_in), row(c_ln_g), row(c_ln_b), ws_cat, bias_full,
      att_bias, w_out.astype(BF16), row(ln_g), row(ln_b))
    return out.reshape(bt, s_len, d)


def _router_kernel(x_ref, wr_ref, bt_ref, rt_ref, rn_ref, cnt_ref, after_ref, below_ref):
    tr = x_ref.shape[0]
    nr = ROUTER_ROWS

    @pl.when(pl.program_id(0) == 0)
    def _():
        rr = lax.broadcasted_iota(I32, (tr, tr), 0)
        cc = lax.broadcasted_iota(I32, (tr, tr), 1)
        after_ref[...] = jnp.where(rr < cc, 1.0, 0.0).astype(BF16)
        lr = lax.broadcasted_iota(I32, (LANES, LANES), 0)
        lc = lax.broadcasted_iota(I32, (LANES, LANES), 1)
        below_ref[...] = jnp.where(lc < lr, 1.0, 0.0).astype(BF16)

    x = x_ref[...]
    x_hi = x.astype(BF16)
    x_lo = (x - x_hi.astype(F32)).astype(BF16)
    p_hi = jnp.dot(x_hi, wr_ref[...], preferred_element_type=F32)
    p_lo = jnp.dot(x_lo, wr_ref[...], preferred_element_type=F32)
    logits = (p_hi + pltpu.roll(p_hi, LANES // 2, axis=1) + p_lo).T[0:nr] + bt_ref[...]
    row = lax.broadcasted_iota(I32, (nr, tr), 0).astype(F32)

    def top1(vals):
        m = jnp.max(vals, axis=0, keepdims=True)
        idx = jnp.min(jnp.where(vals == m, row, float(nr)), axis=0, keepdims=True)
        return m, idx

    g_mask = row < N_GROUPS
    g_max, g_idx = top1(jnp.where(g_mask, logits, NEG_INF))
    g_top = 1.0 / jnp.sum(jnp.where(g_mask, jnp.exp(logits - g_max), 0.0), axis=0, keepdims=True)

    e_lo = ROUTER_LANE0 + EXPERTS_PER_GROUP * g_idx
    e_vals = jnp.where(jnp.logical_and(row >= e_lo, row < e_lo + EXPERTS_PER_GROUP), logits, NEG_INF)
    m1, i1 = top1(e_vals)
    m2, i2 = top1(jnp.where(row == i1, NEG_INF, e_vals))
    ratio = jnp.exp(m2 - m1)
    w1 = g_top / (1.0 + ratio)
    w2 = g_top * ratio / (1.0 + ratio)

    chosen = jnp.logical_or(row == i1, row == i2)
    onehot = jnp.where(chosen, 1.0, 0.0)
    seen = jnp.dot(onehot.astype(BF16), after_ref[...], preferred_element_type=F32)
    counts = jnp.broadcast_to(jnp.sum(onehot, axis=1, keepdims=True), (nr, LANES))
    cnt_ref[...] = counts

    groups = ((counts.astype(I32) + (SEG_ALIGN - 1)) // SEG_ALIGN).astype(F32).astype(BF16)
    groups = jnp.concatenate([groups, jnp.zeros((LANES - nr, LANES), BF16)], axis=0)
    seg_start = SEG_ALIGN * jnp.dot(below_ref[...], groups, preferred_element_type=F32)[0:nr]
    where_row = seen + jnp.concatenate([seg_start] * (tr // LANES), axis=1)
    pos1 = jnp.sum(jnp.where(row == i1, where_row, 0.0), axis=0, keepdims=True)
    pos2 = jnp.sum(jnp.where(row == i2, where_row, 0.0), axis=0, keepdims=True)

    fields = [i1 - ROUTER_LANE0, i2 - ROUTER_LANE0, w1, w2, pos1, pos2]
    res = jnp.concatenate(fields + [jnp.zeros((SUBLANES - len(fields), tr), F32)], axis=0)
    rt_ref[...] = res
    rn_ref[...] = jnp.concatenate([res, jnp.zeros((LANES - SUBLANES, tr), F32)], axis=0).T


def _router(xt, rg_w, rg_b, re_w, re_b):
    n_tok, d = xt.shape
    tr = ROUTER_TILE
    w_all = jnp.concatenate([rg_w, jnp.transpose(re_w, (1, 0, 2)).reshape(d, N_EXPERTS)], axis=1)
    b_all = jnp.concatenate([rg_b, re_b.reshape(N_EXPERTS)])
    half = LANES // 2
    pad = half - w_all.shape[1]
    w_hi = w_all.astype(BF16)
    w_lo = (w_all - w_hi.astype(F32)).astype(BF16)
    w_all = jnp.concatenate([jnp.pad(w_hi, ((0, 0), (0, pad))), jnp.pad(w_lo, ((0, 0), (0, pad)))], axis=1)
    b_t = jnp.broadcast_to(jnp.pad(b_all, (0, ROUTER_ROWS - b_all.shape[0]))[:, None], (ROUTER_ROWS, tr))
    return pl.pallas_call(
        _router_kernel,
        out_shape=(
            jax.ShapeDtypeStruct((SUBLANES, n_tok), F32),
            jax.ShapeDtypeStruct((n_tok, LANES), F32),
            jax.ShapeDtypeStruct((n_tok // tr, ROUTER_ROWS, LANES), F32),
        ),
        grid=(n_tok // tr,),
        in_specs=[
            pl.BlockSpec((tr, d), lambda i: (i, 0)),
            _const_spec((d, LANES)),
            _const_spec((ROUTER_ROWS, tr)),
        ],
        out_specs=(
            pl.BlockSpec((SUBLANES, tr), lambda i: (0, i)),
            pl.BlockSpec((tr, LANES), lambda i: (i, 0)),
            pl.BlockSpec((None, ROUTER_ROWS, LANES), lambda i: (i, 0, 0)),
        ),
        scratch_shapes=[pltpu.VMEM((tr, tr), BF16), pltpu.VMEM((LANES, LANES), BF16)],
        compiler_params=pltpu.CompilerParams(
            dimension_semantics=("arbitrary",), vmem_limit_bytes=VMEM_LIMIT),
        name="router",
    )(xt, w_all, b_t)


def _round_up(v, m):
    return ((v + m - 1) // m) * m


def _num_expert_blocks(n_tok):
    n_tiles = n_tok // MOE_TILE
    rows = n_tok * TOP_K + n_tiles * N_EXPERTS * (SEG_ALIGN - 1)
    return -(-rows // EXPERT_ROWS) + N_EXPERTS


def _routing_tables(cnt, n_tok):
    counts = cnt[:, ROUTER_LANE0:ROUTER_LANE0 + N_EXPERTS, 0].astype(I32)
    seg = _round_up(counts, SEG_ALIGN)
    tile_off = jnp.cumsum(seg, axis=1) - seg
    tot = jnp.sum(seg, axis=0)
    reg = _round_up(tot, EXPERT_ROWS)
    reg_end = jnp.cumsum(reg)
    reg_start = reg_end - reg
    glob_off = reg_start[None, :] + jnp.cumsum(seg, axis=0) - seg
    n_blocks = _num_expert_blocks(n_tok)
    blk_start = jnp.arange(n_blocks, dtype=I32) * EXPERT_ROWS
    block_e = jnp.sum((reg_end[None, :] <= blk_start[:, None]).astype(I32), axis=1)
    block_e = jnp.minimum(block_e, N_EXPERTS - 1).astype(I32)
    n_used = (reg_end[-1] // EXPERT_ROWS).astype(I32).reshape(1)
    groups = lambda a: (a.reshape(-1) // SEG_ALIGN).astype(I32)
    big = (jnp.max(seg, axis=1) >= INLINE_GROUPS * SEG_ALIGN).astype(I32)
    ar = jnp.arange(N_EXPERTS, dtype=I32)
    later = jnp.where(jnp.logical_and(ar[None, :] > ar[:, None], reg[None, :] > 0), ar[None, :], N_EXPERTS)
    next_of = jnp.min(later, axis=1)
    next_of = jnp.where(next_of < N_EXPERTS, next_of, ar)
    next_e = jnp.sum(jnp.where(block_e[:, None] == ar[None, :], next_of[None, :], 0), axis=1).astype(I32)
    return dict(tile_off=groups(tile_off), glob_off=groups(glob_off), seg=groups(seg), big=big,
                tail_start=groups(reg_start + tot), tail_len=groups(reg - tot),
                block_e=block_e, n_used=n_used, next_e=next_e)


def _grouped(rows):
    return rows.reshape(rows.shape[0] // SEG_ALIGN, SEG_ALIGN, rows.shape[1])


def _ungrouped(groups):
    return groups.reshape(groups.shape[0] * SEG_ALIGN, groups.shape[2])


def _for_each_piece(length, max_len, fn):
    off = 0
    for size in reversed([1 << b for b in range(max_len.bit_length())]):
        take = (length & size) != 0

        @pl.when(take)
        def _(off=off, size=size):
            fn(off, size)

        off = off + jnp.where(take, size, 0)


def _start_segment_copies(copy, toff_ref, goff_ref, seg_ref, big_ref, t, live, inline):
    def per_expert(e, limit, enabled):
        idx = t * N_EXPERTS + e
        t0 = toff_ref[idx]
        g0 = goff_ref[idx]
        _for_each_piece(jnp.where(enabled, seg_ref[idx], 0), limit,
                        lambda off, size: copy(t0 + off, g0 + off, size))

    def looped(enabled):
        def body(e, c):
            per_expert(e, MOE_TILE // SEG_ALIGN, enabled)
            return c

        lax.fori_loop(0, N_EXPERTS, body, 0)

    if not inline:
        looped(live)
        return
    big = big_ref[t] != 0
    for e in range(N_EXPERTS):
        per_expert(e, INLINE_GROUPS - 1, jnp.logical_and(live, jnp.logical_not(big)))

    @pl.when(jnp.logical_and(live, big))
    def _():
        looped(True)


def _wait_segment_copies(wait, toff_ref, seg_ref, t):
    last = t * N_EXPERTS + N_EXPERTS - 1
    _for_each_piece(toff_ref[last] + seg_ref[last], SORT_ROWS // SEG_ALIGN, lambda off, size: wait(size))


def _dispatch_kernel(toff_ref, goff_ref, seg_ref, big_ref, tstart_ref, tlen_ref, nu_ref,
                     x_ref, rt_ref, xp_hbm, obuf, zbuf, sem, zsem):
    i = pl.program_id(0)
    n = pl.num_programs(0)
    tile = x_ref.shape[0]
    slot = i % 2

    def wait_segments(t, sl):
        _wait_segment_copies(
            lambda size: pltpu.make_async_copy(obuf.at[sl, pl.ds(0, size)], xp_hbm.at[pl.ds(0, size)],
                                               sem.at[sl]).wait(),
            toff_ref, seg_ref, t)

    def start_segments(t, sl, live, inline):
        _start_segment_copies(
            lambda tg, sg, size: pltpu.make_async_copy(
                obuf.at[sl, pl.ds(tg, size)], xp_hbm.at[pl.ds(sg, size)], sem.at[sl]).start(),
            toff_ref, goff_ref, seg_ref, big_ref, t, live, inline)

    @pl.when(i >= 2)
    def _():
        wait_segments(i - 2, slot)

    start_segments(jnp.maximum(i - 1, 0), 1 - slot, i >= 1, inline=True)

    pos1 = rt_ref[4:5, :].astype(I32)
    pos2 = rt_ref[5:6, :].astype(I32)
    xb = x_ref[...].astype(BF16)
    for c in range(SORT_ROWS // SORT_CHUNK):
        rows = lax.broadcasted_iota(I32, (SORT_CHUNK, tile), 0) + c * SORT_CHUNK
        hit = jnp.logical_or(rows == pos1, rows == pos2)
        g0, g1 = c * SORT_CHUNK // SEG_ALIGN, (c + 1) * SORT_CHUNK // SEG_ALIGN
        obuf[slot, g0:g1] = _grouped(jnp.dot(
            jnp.where(hit, 1.0, 0.0).astype(BF16), xb, preferred_element_type=F32).astype(BF16))

    def zero_fill(wait):
        def per_expert(e, c):
            def piece(off, size):
                cp = pltpu.make_async_copy(zbuf.at[pl.ds(0, size)],
                                           xp_hbm.at[pl.ds(tstart_ref[e] + off, size)], zsem)
                cp.wait() if wait else cp.start()

            _for_each_piece(tlen_ref[e], EXPERT_ROWS // SEG_ALIGN - 1, piece)
            return c

        lax.fori_loop(0, N_EXPERTS, per_expert, 0)

        zg = zbuf.shape[0]

        def per_zero_copy(h, c):
            cp = pltpu.make_async_copy(zbuf, xp_hbm.at[pl.ds(h * zg, zg)], zsem)
            cp.wait() if wait else cp.start()
            return c

        per_block = EXPERT_ROWS // SEG_ALIGN // zg
        lax.fori_loop(nu_ref[0] * per_block, xp_hbm.shape[0] // zg, per_zero_copy, 0)

    @pl.when(i == 0)
    def _():
        zbuf[...] = jnp.zeros(zbuf.shape, zbuf.dtype)
        zero_fill(False)

    @pl.when(i == n - 1)
    def _():
        start_segments(i, slot, True, inline=False)

        @pl.when(i >= 1)
        def _():
            wait_segments(i - 1, 1 - slot)

        wait_segments(i, slot)
        zero_fill(True)


def _dispatch(xt, rt, tabs):
    n_tok, d = xt.shape
    tile = MOE_TILE
    n_groups = _num_expert_blocks(n_tok) * EXPERT_ROWS // SEG_ALIGN
    return pl.pallas_call(
        _dispatch_kernel,
        out_shape=jax.ShapeDtypeStruct((n_groups, SEG_ALIGN, d), BF16),
        grid_spec=pltpu.PrefetchScalarGridSpec(
            num_scalar_prefetch=7,
            grid=(n_tok // tile,),
            in_specs=[
                pl.BlockSpec((tile, d), lambda i, *_: (i, 0)),
                pl.BlockSpec((SUBLANES, tile), lambda i, *_: (0, i)),
            ],
            out_specs=pl.BlockSpec(memory_space=pl.ANY),
            scratch_shapes=[
                pltpu.VMEM((2, SORT_ROWS // SEG_ALIGN, SEG_ALIGN, d), BF16),
                pltpu.VMEM((EXPERT_ROWS // SEG_ALIGN // 2, SEG_ALIGN, d), BF16),
                pltpu.SemaphoreType.DMA((2,)),
                pltpu.SemaphoreType.DMA,
            ],
        ),
        compiler_params=pltpu.CompilerParams(
            dimension_semantics=("arbitrary",), vmem_limit_bytes=VMEM_LIMIT),
        name="dispatch",
    )(tabs["tile_off"], tabs["glob_off"], tabs["seg"], tabs["big"], tabs["tail_start"],
      tabs["tail_len"], tabs["n_used"], xt, rt)


def _ffn_kernel(be_ref, nu_ref, nxt_ref, x_ref, wg_hbm, wu_hbm, wd_hbm, o_ref,
                wg_f, wu_f, wd_f, wg_b, wu_b, wd_b, sem, *, layer):
    i = pl.program_id(0)
    used = i < nu_ref[0]
    e = be_ref[i]

    def weight_copies(expert):
        return [pltpu.make_async_copy(src.at[layer, expert], dst, sem.at[k])
                for k, (src, dst) in enumerate(((wg_hbm, wg_f), (wu_hbm, wu_f), (wd_hbm, wd_f)))]

    @pl.when(i == 0)
    def _():
        for cp in weight_copies(e):
            cp.start()

    @pl.when(jnp.logical_and(used, jnp.logical_or(i == 0, e != be_ref[jnp.maximum(i - 1, 0)])))
    def _():
        for cp in weight_copies(e):
            cp.wait()
        wg_b[...] = wg_f[...].astype(BF16)
        wu_b[...] = wu_f[...].astype(BF16)
        wd_b[...] = wd_f[...].astype(BF16)

        @pl.when(nxt_ref[i] != e)
        def _():
            for cp in weight_copies(nxt_ref[i]):
                cp.start()

    @pl.when(used)
    def _():
        xb = _ungrouped(x_ref[...])
        g = jnp.dot(xb, wg_b[...], preferred_element_type=F32)
        u = jnp.dot(xb, wu_b[...], preferred_element_type=F32)
        hb = (g * jax.nn.sigmoid(g) * u).astype(BF16)
        o_ref[...] = _grouped(jnp.dot(hb, wd_b[...], preferred_element_type=F32).astype(BF16))

    @pl.when(jnp.logical_not(used))
    def _():
        o_ref[...] = jnp.zeros(o_ref.shape, o_ref.dtype)


def _expert_ffn(xp, tabs, w_gate, w_up, w_down, layer):
    n_groups, _, d = xp.shape
    d_e = w_gate.shape[3]
    blk_groups = EXPERT_ROWS // SEG_ALIGN
    n_blocks = n_groups // blk_groups
    blk = lambda i, be, nu, nxt: (jnp.minimum(i, nu[0] - 1), 0, 0)
    return pl.pallas_call(
        functools.partial(_ffn_kernel, layer=layer),
        out_shape=jax.ShapeDtypeStruct((n_groups, SEG_ALIGN, d), BF16),
        grid_spec=pltpu.PrefetchScalarGridSpec(
            num_scalar_prefetch=3,
            grid=(n_blocks,),
            in_specs=[
                pl.BlockSpec((blk_groups, SEG_ALIGN, d), blk),
                pl.BlockSpec(memory_space=pl.ANY),
                pl.BlockSpec(memory_space=pl.ANY),
                pl.BlockSpec(memory_space=pl.ANY),
            ],
            out_specs=pl.BlockSpec((blk_groups, SEG_ALIGN, d), lambda i, be, nu, nxt: (i, 0, 0)),
            scratch_shapes=[
                pltpu.VMEM((d, d_e), F32),
                pltpu.VMEM((d, d_e), F32),
                pltpu.VMEM((d_e, d), F32),
                pltpu.VMEM((d, d_e), BF16),
                pltpu.VMEM((d, d_e), BF16),
                pltpu.VMEM((d_e, d), BF16),
                pltpu.SemaphoreType.DMA((3,)),
            ],
        ),
        compiler_params=pltpu.CompilerParams(
            dimension_semantics=("arbitrary",), vmem_limit_bytes=VMEM_LIMIT),
        name="expert_ffn",
    )(tabs["block_e"], tabs["n_used"], tabs["next_e"], xp, w_gate, w_up, w_down)


def _combine_kernel(toff_ref, goff_ref, seg_ref, big_ref, x_ref, rn_ref, yp_hbm, lng_ref, lnb_ref,
                    o_ref, ybuf, sem):
    i = pl.program_id(0)
    n = pl.num_programs(0)
    tile = x_ref.shape[0]
    slot = i % 2

    def wait_segments(t, sl):
        _wait_segment_copies(
            lambda size: pltpu.make_async_copy(yp_hbm.at[pl.ds(0, size)], ybuf.at[sl, pl.ds(0, size)],
                                               sem.at[sl]).wait(),
            toff_ref, seg_ref, t)

    def start_segments(t, sl, live, inline):
        _start_segment_copies(
            lambda tg, sg, size: pltpu.make_async_copy(
                yp_hbm.at[pl.ds(sg, size)], ybuf.at[sl, pl.ds(tg, size)], sem.at[sl]).start(),
            toff_ref, goff_ref, seg_ref, big_ref, t, live, inline)

    @pl.when(i == 0)
    def _():
        ybuf[...] = jnp.zeros(ybuf.shape, ybuf.dtype)
        start_segments(0, 0, True, inline=False)

    wait_segments(i, slot)
    start_segments(jnp.minimum(i + 1, n - 1), 1 - slot, i + 1 < n, inline=True)

    rn = rn_ref[...]
    full = lambda col: jnp.broadcast_to(col, (tile, LANES))
    w1, w2 = full(rn[:, 2:3]), full(rn[:, 3:4])
    pos1, pos2 = full(rn[:, 4:5]), full(rn[:, 5:6])
    lane = lax.broadcasted_iota(I32, (tile, LANES), 1).astype(F32)

    def pick_lanes(col0):
        cols = lane + float(col0)
        return jnp.where(cols == pos1, w1, 0.0) + jnp.where(cols == pos2, w2, 0.0)

    ffn = jnp.zeros(o_ref.shape, F32)
    for c in range(SORT_ROWS // SORT_CHUNK):
        pick = jnp.concatenate([pick_lanes(c * SORT_CHUNK + l0) for l0 in range(0, SORT_CHUNK, LANES)],
                               axis=1).astype(BF16)
        g0, g1 = c * SORT_CHUNK // SEG_ALIGN, (c + 1) * SORT_CHUNK // SEG_ALIGN
        ffn = ffn + jnp.dot(pick, _ungrouped(ybuf[slot, g0:g1]), preferred_element_type=F32)
    o_ref[...] = _layer_norm(ALPHA * x_ref[...] + ffn, lng_ref[...], lnb_ref[...])


def _combine(xt, rn, yp, tabs, ln_g, ln_b):
    n_tok, d = xt.shape
    tile = MOE_TILE
    return pl.pallas_call(
        _combine_kernel,
        out_shape=jax.ShapeDtypeStruct((n_tok, d), F32),
        grid_spec=pltpu.PrefetchScalarGridSpec(
            num_scalar_prefetch=4,
            grid=(n_tok // tile,),
            in_specs=[
                pl.BlockSpec((tile, d), lambda i, *_: (i, 0)),
                pl.BlockSpec((tile, LANES), lambda i, *_: (i, 0)),
                pl.BlockSpec(memory_space=pl.ANY),
                pl.BlockSpec((1, d), lambda i, *_: (0, 0)),
                pl.BlockSpec((1, d), lambda i, *_: (0, 0)),
            ],
            out_specs=pl.BlockSpec((tile, d), lambda i, *_: (i, 0)),
            scratch_shapes=[
                pltpu.VMEM((2, SORT_ROWS // SEG_ALIGN, SEG_ALIGN, d), BF16),
                pltpu.SemaphoreType.DMA((2,)),
            ],
        ),
        compiler_params=pltpu.CompilerParams(
            dimension_semantics=("arbitrary",), vmem_limit_bytes=VMEM_LIMIT),
        name="combine",
    )(tabs["tile_off"], tabs["glob_off"], tabs["seg"], tabs["big"], xt, rn, yp,
      ln_g.reshape(1, d), ln_b.reshape(1, d))


def _moe_layer(x, rg_w, rg_b, re_w, re_b, w_gate, w_up, w_down, layer, ln_g, ln_b):
    bt, s_len, d = x.shape
    n_tok = bt * s_len
    xt = x.reshape(n_tok, d)
    rt, rn, cnt = _router(xt, rg_w, rg_b, re_w, re_b)
    tabs = _routing_tables(cnt, n_tok)
    xp = _dispatch(xt, rt, tabs)
    yp = _expert_ffn(xp, tabs, w_gate, w_up, w_down, layer)
    out = _combine(xt, rn, yp, tabs, ln_g, ln_b)
    return out.reshape(bt, s_len, d)


def kernel(x, ab_w_in, ab_b_in, a_dw, a_dw_b, a_ln_g, a_ln_b, b_dw, ab_w_out, cd_w_in, cd_b_in, c_ln_g, c_ln_b, c_ws, c_ws_b, d_rel_bias, cd_w_out, mix_ln_g, mix_ln_b, moe_rg_w, moe_rg_b, moe_re_w, moe_re_b, moe_w_gate, moe_w_up, moe_w_down, ffn_ln_g, ffn_ln_b):
    for layer in range(DEPTH):
        i = layer // 2
        if layer % 2 == 0:
            x = _ab_mixer(x, ab_w_in[i], ab_b_in[i], a_dw[i], a_dw_b[i], a_ln_g[i], a_ln_b[i],
                          b_dw[i], ab_w_out[i], mix_ln_g[layer], mix_ln_b[layer])
        else:
            x = _cd_mixer(x, cd_w_in[i], cd_b_in[i], c_ln_g[i], c_ln_b[i], c_ws[i], c_ws_b[i],
                          d_rel_bias[i], cd_w_out[i], mix_ln_g[layer], mix_ln_b[layer])
        x = _moe_layer(x, moe_rg_w[layer], moe_rg_b[layer], moe_re_w[layer], moe_re_b[layer],
                       moe_w_gate, moe_w_up, moe_w_down, layer,
                       ffn_ln_g[layer], ffn_ln_b[layer])
    return x
```

```python
import functools

import jax
import jax.numpy as jnp
from jax import lax
from jax.experimental import pallas as pl
from jax.experimental.pallas import tpu as pltpu

F32 = jnp.float32
BF16 = jnp.bfloat16
I32 = jnp.int32

DEPTH = 2
CHUNK = 64
LEFT_CHUNKS = 8
HIST = LEFT_CHUNKS * CHUNK
CONV_A = 31
CONV_B = 3
HEADS_C = 8
SGU_BLOCK = 128
HEADS_D = 8
MAX_REL = 256
N_GROUPS = 4
EXPERTS_PER_GROUP = 8
N_EXPERTS = N_GROUPS * EXPERTS_PER_GROUP
TOP_K = 2
ALPHA = (2 * DEPTH) ** 0.25
LN_EPS = 1e-5
NEG_INF = -1e30
LOG2_E = 1.4426950408889634

LANES = 128
SUBLANES = 8
VMEM_LIMIT = 56 * 1024 * 1024

SEQ_TILE = 512
CONV_ROWS = 64
HALO = 32
Q_ROWS = 256
K_ROWS = Q_ROWS + HIST
MOE_TILE = 512
ROUTER_TILE = MOE_TILE
EXPERT_ROWS = 512
SEG_ALIGN = 2 * SUBLANES
SORT_ROWS = MOE_TILE * TOP_K + N_EXPERTS * SEG_ALIGN
SORT_CHUNK = 256
INLINE_GROUPS = 8
ROUTER_LANE0 = N_GROUPS
ROUTER_ROWS = -(-(N_GROUPS + N_EXPERTS) // SUBLANES) * SUBLANES
_B_SHIFTS = sorted({(HALO - (CONV_B - 1) + j) % SUBLANES for j in range(CONV_B)} - {0})


def _layer_norm(x, g, b):
    mu = jnp.mean(x, axis=-1, keepdims=True)
    xc = x - mu
    var = jnp.mean(xc * xc, axis=-1, keepdims=True)
    return xc * lax.rsqrt(var + LN_EPS) * g + b


def _const_spec(shape):
    nd = len(shape)
    return pl.BlockSpec(shape, lambda *_: (0,) * nd, pipeline_mode=pl.Buffered(1))


def _ab_mixer_kernel(x_ref, win_ref, bin_ref, adw_ref, adwb_ref, alng_ref, alnb_ref,
                     bdw_ref, wout_ref, lng_ref, lnb_ref, o_ref,
                     abuf, cbuf, gbuf, mbuf, asht, csht, *, d_a, d_b):
    ts = x_ref.shape[0]

    @pl.when(pl.program_id(1) == 0)
    def _():
        abuf[0:HALO, :] = jnp.zeros((HALO, d_a), F32)
        cbuf[0:HALO, :] = jnp.zeros((HALO, d_b), F32)

    x = x_ref[...]
    xb = x.astype(BF16)
    n_sh = HALO + ts - SUBLANES

    def tap(buf, sht, slot_of, r0, off):
        base = r0 + (off // SUBLANES) * SUBLANES
        if off % SUBLANES == 0:
            return buf[base:base + CONV_ROWS, :]
        return sht[slot_of(off % SUBLANES), base:base + CONV_ROWS, :]

    h_a = jnp.dot(xb, win_ref[:, 0:2 * d_a], preferred_element_type=F32) + bin_ref[:, 0:2 * d_a]
    abuf[HALO:HALO + ts, :] = h_a[:, 0:d_a] * jax.nn.sigmoid(h_a[:, d_a:])
    for sh in range(1, SUBLANES):
        asht[sh, 0:n_sh, :] = abuf[sh:sh + n_sh, :]
    for r0 in range(0, ts, CONV_ROWS):
        acc = jnp.broadcast_to(adwb_ref[...], (CONV_ROWS, d_a))
        for j in range(CONV_A):
            acc = acc + adw_ref[j:j + 1, :] * tap(abuf, asht, lambda sh: sh, r0, HALO - (CONV_A - 1) + j)
        a = _layer_norm(acc, alng_ref[...], alnb_ref[...])
        mbuf[r0:r0 + CONV_ROWS, 0:d_a] = (a * jax.nn.sigmoid(a)).astype(BF16)

    h_b = jnp.dot(xb, win_ref[:, 2 * d_a:], preferred_element_type=F32) + bin_ref[:, 2 * d_a:]
    gbuf[...] = h_b[:, 0:d_b]
    cbuf[HALO:HALO + ts, :] = h_b[:, d_b:2 * d_b] * h_b[:, 2 * d_b:]
    for sh in _B_SHIFTS:
        csht[_B_SHIFTS.index(sh), 0:n_sh, :] = cbuf[sh:sh + n_sh, :]
    for r0 in range(0, ts, CONV_ROWS):
        s = jnp.zeros((CONV_ROWS, d_b), F32)
        for j in range(CONV_B):
            s = s + bdw_ref[j:j + 1, :] * tap(cbuf, csht, _B_SHIFTS.index, r0, HALO - (CONV_B - 1) + j)
        mbuf[r0:r0 + CONV_ROWS, d_a:d_a + d_b] = (gbuf[r0:r0 + CONV_ROWS, :] * s).astype(BF16)

    abuf[0:HALO, :] = abuf[ts:ts + HALO, :]
    cbuf[0:HALO, :] = cbuf[ts:ts + HALO, :]

    mix = jnp.dot(mbuf[...], wout_ref[...], preferred_element_type=F32)
    o_ref[...] = _layer_norm(ALPHA * x + mix, lng_ref[...], lnb_ref[...])


def _ab_mixer(x, w_in, b_in, a_dw, a_dw_b, a_ln_g, a_ln_b, b_dw, w_out, ln_g, ln_b):
    bt, s_len, d = x.shape
    d_a = a_dw.shape[1]
    d_b = b_dw.shape[1]
    w_in_w = w_in.shape[1]
    ts = SEQ_TILE
    row = lambda v: v.reshape(1, -1)
    kern = functools.partial(_ab_mixer_kernel, d_a=d_a, d_b=d_b)
    return pl.pallas_call(
        kern,
        out_shape=jax.ShapeDtypeStruct((bt, s_len, d), F32),
        grid=(bt, s_len // ts),
        in_specs=[
            pl.BlockSpec((None, ts, d), lambda b, t: (b, t, 0)),
            _const_spec((d, w_in_w)),
            _const_spec((1, w_in_w)),
            _const_spec((CONV_A, d_a)),
            _const_spec((1, d_a)),
            _const_spec((1, d_a)),
            _const_spec((1, d_a)),
            _const_spec((CONV_B, d_b)),
            _const_spec((d_a + d_b, d)),
            _const_spec((1, d)),
            _const_spec((1, d)),
        ],
        out_specs=pl.BlockSpec((None, ts, d), lambda b, t: (b, t, 0)),
        scratch_shapes=[
            pltpu.VMEM((HALO + ts, d_a), F32),
            pltpu.VMEM((HALO + ts, d_b), F32),
            pltpu.VMEM((ts, d_b), F32),
            pltpu.VMEM((ts, d_a + d_b), BF16),
            pltpu.VMEM((SUBLANES, HALO + ts, d_a), F32),
            pltpu.VMEM((len(_B_SHIFTS), HALO + ts, d_b), F32),
        ],
        compiler_params=pltpu.CompilerParams(
            dimension_semantics=("arbitrary", "arbitrary"), vmem_limit_bytes=VMEM_LIMIT),
        name="ab_mixer",
    )(x, w_in.astype(BF16), row(b_in), a_dw, row(a_dw_b), row(a_ln_g), row(a_ln_b),
      b_dw, w_out.astype(BF16), row(ln_g), row(ln_b))


def _cd_mixer_kernel(x_ref, win_ref, bin_ref, clng_ref, clnb_ref, wsc_ref, wsb_ref,
                     bias_ref, wout_ref, lng_ref, lnb_ref, o_ref,
                     qbuf, kbuf, vbuf, wsm, vbd, mbuf, *, d_c, d_d):
    ts = x_ref.shape[0]
    first_in_seq = pl.program_id(1) == 0
    hc = d_c // HEADS_C
    hd = d_d // HEADS_D

    @pl.when(first_in_seq)
    def _():
        kbuf[0:HIST, :] = jnp.zeros((HIST, d_d), BF16)
        vbuf[0:HIST, :] = jnp.zeros((HIST, d_d), BF16)
        r = lax.broadcasted_iota(I32, (SGU_BLOCK, HEADS_C * SGU_BLOCK), 0)
        c = lax.broadcasted_iota(I32, (SGU_BLOCK, HEADS_C * SGU_BLOCK), 1) % SGU_BLOCK
        wsm[...] = jnp.where(c // CHUNK <= r // CHUNK, wsc_ref[...], 0.0).astype(BF16)

    x = x_ref[...]
    xb = x.astype(BF16)
    h_att = (jnp.dot(xb, win_ref[:, 2 * d_c:], preferred_element_type=F32) + bin_ref[:, 2 * d_c:])
    qbuf[...] = (h_att[:, 0:d_d] * (hd ** -0.5 * LOG2_E)).astype(BF16)
    kbuf[HIST:HIST + ts, :] = h_att[:, d_d:2 * d_d].astype(BF16)
    vbuf[HIST:HIST + ts, :] = h_att[:, 2 * d_d:].astype(BF16)

    def attention(qb):
        q0 = qb * Q_ROWS
        col = lax.broadcasted_iota(I32, (1, K_ROWS), 1)
        no_key = jnp.where(jnp.logical_and(first_in_seq, col + q0 < HIST), NEG_INF, 0.0)
        for hh in range(HEADS_D):
            q = qbuf[q0:q0 + Q_ROWS, hh * hd:(hh + 1) * hd]
            k = kbuf[q0:q0 + K_ROWS, hh * hd:(hh + 1) * hd]
            v = vbuf[q0:q0 + K_ROWS, hh * hd:(hh + 1) * hd]
            s = lax.dot_general(q, k, (((1,), (1,)), ((), ())), preferred_element_type=F32)
            s = s + bias_ref[hh] + no_key
            p = jnp.exp2(s - jnp.max(s, axis=-1, keepdims=True))
            l = jnp.sum(p, axis=-1, keepdims=True)
            o = jnp.dot(p.astype(BF16), v, preferred_element_type=F32)
            mbuf[q0:q0 + Q_ROWS, d_c + hh * hd:d_c + (hh + 1) * hd] = (o / l).astype(BF16)

    for qb in range(ts // Q_ROWS):
        attention(qb)

    kbuf[0:HIST, :] = kbuf[ts:ts + HIST, :]
    vbuf[0:HIST, :] = vbuf[ts:ts + HIST, :]

    h_sgu = (jnp.dot(xb, win_ref[:, 0:2 * d_c], preferred_element_type=F32) + bin_ref[:, 0:2 * d_c])
    u = h_sgu[:, 0:d_c]
    vn = _layer_norm(h_sgu[:, d_c:], clng_ref[...], clnb_ref[...])
    lane_head = lax.broadcasted_iota(I32, (SGU_BLOCK, d_c), 1) // hc
    for nb in range(ts // SGU_BLOCK):
        v_blk = vn[nb * SGU_BLOCK:(nb + 1) * SGU_BLOCK, :]
        for hh in range(HEADS_C):
            vbd[hh * SGU_BLOCK:(hh + 1) * SGU_BLOCK, :] = jnp.where(
                lane_head == hh, v_blk, 0.0).astype(BF16)
        gate = jnp.dot(wsm[...], vbd[...], preferred_element_type=F32) + wsb_ref[...]
        mbuf[nb * SGU_BLOCK:(nb + 1) * SGU_BLOCK, 0:d_c] = (
            u[nb * SGU_BLOCK:(nb + 1) * SGU_BLOCK, :] * gate).astype(BF16)

    mix = jnp.dot(mbuf[...], wout_ref[...], preferred_element_type=F32)
    o_ref[...] = _layer_norm(ALPHA * x + mix, lng_ref[...], lnb_ref[...])


def _attention_bias(d_rel_bias):
    heads = d_rel_bias.shape[0]
    i = jnp.arange(Q_ROWS)[:, None]
    j = jnp.arange(K_ROWS)[None, :]
    jb = j - (i // CHUNK) * CHUNK
    in_band = jnp.logical_and(jb >= 0, jb < (LEFT_CHUNKS + 1) * CHUNK)
    n_diag = Q_ROWS + K_ROWS - 1
    m = jnp.arange(n_diag)
    diag = d_rel_bias[:, jnp.clip(Q_ROWS - 1 + HIST - m, -MAX_REL, MAX_REL) + MAX_REL].astype(F32)
    period = jnp.pad(diag, ((0, 0), (0, 1)))
    flat = jnp.tile(period, (1, Q_ROWS))[:, :Q_ROWS * n_diag]
    rel = flat.reshape(heads, Q_ROWS, n_diag)[:, :, Q_ROWS - 1:Q_ROWS - 1 + K_ROWS]
    return jnp.where(in_band[None], rel * LOG2_E, NEG_INF)


def _cd_mixer(x, w_in, b_in, c_ln_g, c_ln_b, c_ws, c_ws_b, d_rel_bias, w_out, ln_g, ln_b):
    bt, s_len, d = x.shape
    d_c = c_ln_g.shape[0]
    d_d = w_out.shape[0] - d_c
    w_in_w = w_in.shape[1]
    ts = SEQ_TILE
    assert ts == HIST and ts % Q_ROWS == 0 and ts % SGU_BLOCK == 0
    row = lambda v: v.reshape(1, -1)
    ws_cat = jnp.transpose(c_ws, (1, 0, 2)).reshape(SGU_BLOCK, HEADS_C * SGU_BLOCK)
    bias_full = jnp.repeat(c_ws_b.T, d_c // HEADS_C, axis=1)
    att_bias = _attention_bias(d_rel_bias)
    kern = functools.partial(_cd_mixer_kernel, d_c=d_c, d_d=d_d)
    return pl.pallas_call(
        kern,
        out_shape=jax.ShapeDtypeStruct((bt, s_len, d), F32),
        grid=(bt, s_len // ts),
        in_specs=[
            pl.BlockSpec((None, ts, d), lambda b, t: (b, t, 0)),
            _const_spec((d, w_in_w)),
            _const_spec((1, w_in_w)),
            _const_spec((1, d_c)),
            _const_spec((1, d_c)),
            _const_spec((SGU_BLOCK, HEADS_C * SGU_BLOCK)),
            _const_spec((SGU_BLOCK, d_c)),
            _const_spec((HEADS_D, Q_ROWS, K_ROWS)),
            _const_spec((d_c + d_d, d)),
            _const_spec((1, d)),
            _const_spec((1, d)),
        ],
        out_specs=pl.BlockSpec((None, ts, d), lambda b, t: (b, t, 0)),
        scratch_shapes=[
            pltpu.VMEM((ts, d_d), BF16),
            pltpu.VMEM((HIST + ts, d_d), BF16),
            pltpu.VMEM((HIST + ts, d_d), BF16),
            pltpu.VMEM((SGU_BLOCK, HEADS_C * SGU_BLOCK), BF16),
            pltpu.VMEM((HEADS_C * SGU_BLOCK, d_c), BF16),
            pltpu.VMEM((ts, d_c + d_d), BF16),
        ],
        compiler_params=pltpu.CompilerParams(
            dimension_semantics=("arbitrary", "arbitrary"), vmem_limit_bytes=VMEM_LIMIT),
        name="cd_mixer",
    )(x, w_in.astype(BF16), row(b_in), row(c_ln_g), row(c_ln_b), ws_cat, bias_full,
      att_bias, w_out.astype(BF16), row(ln_g), row(ln_b))


def _router_kernel(x_ref, wr_ref, bt_ref, rt_ref, rn_ref, cnt_ref, after_ref, below_ref):
    tr = x_ref.shape[0]
    nr = ROUTER_ROWS

    @pl.when(pl.program_id(0) == 0)
    def _():
        rr = lax.broadcasted_iota(I32, (tr, tr), 0)
        cc = lax.broadcasted_iota(I32, (tr, tr), 1)
        after_ref[...] = jnp.where(rr < cc, 1.0, 0.0).astype(BF16)
        lr = lax.broadcasted_iota(I32, (LANES, LANES), 0)
        lc = lax.broadcasted_iota(I32, (LANES, LANES), 1)
        below_ref[...] = jnp.where(lc < lr, 1.0, 0.0).astype(BF16)

    x = x_ref[...]
    x_hi = x.astype(BF16)
    x_lo = (x - x_hi.astype(F32)).astype(BF16)
    p_hi = jnp.dot(x_hi, wr_ref[...], preferred_element_type=F32)
    p_lo = jnp.dot(x_lo, wr_ref[...], preferred_element_type=F32)
    logits = (p_hi + pltpu.roll(p_hi, LANES // 2, axis=1) + p_lo).T[0:nr] + bt_ref[...]
    row = lax.broadcasted_iota(I32, (nr, tr), 0).astype(F32)

    def top1(vals):
        m = jnp.max(vals, axis=0, keepdims=True)
        idx = jnp.min(jnp.where(vals == m, row, float(nr)), axis=0, keepdims=True)
        return m, idx

    g_mask = row < N_GROUPS
    g_max, g_idx = top1(jnp.where(g_mask, logits, NEG_INF))
    g_top = 1.0 / jnp.sum(jnp.where(g_mask, jnp.exp(logits - g_max), 0.0), axis=0, keepdims=True)

    e_lo = ROUTER_LANE0 + EXPERTS_PER_GROUP * g_idx
    e_vals = jnp.where(jnp.logical_and(row >= e_lo, row < e_lo + EXPERTS_PER_GROUP), logits, NEG_INF)
    m1, i1 = top1(e_vals)
    m2, i2 = top1(jnp.where(row == i1, NEG_INF, e_vals))
    ratio = jnp.exp(m2 - m1)
    w1 = g_top / (1.0 + ratio)
    w2 = g_top * ratio / (1.0 + ratio)

    chosen = jnp.logical_or(row == i1, row == i2)
    onehot = jnp.where(chosen, 1.0, 0.0)
    seen = jnp.dot(onehot.astype(BF16), after_ref[...], preferred_element_type=F32)
    counts = jnp.broadcast_to(jnp.sum(onehot, axis=1, keepdims=True), (nr, LANES))
    cnt_ref[...] = counts

    groups = ((counts.astype(I32) + (SEG_ALIGN - 1)) // SEG_ALIGN).astype(F32).astype(BF16)
    groups = jnp.concatenate([groups, jnp.zeros((LANES - nr, LANES), BF16)], axis=0)
    seg_start = SEG_ALIGN * jnp.dot(below_ref[...], groups, preferred_element_type=F32)[0:nr]
    where_row = seen + jnp.concatenate([seg_start] * (tr // LANES), axis=1)
    pos1 = jnp.sum(jnp.where(row == i1, where_row, 0.0), axis=0, keepdims=True)
    pos2 = jnp.sum(jnp.where(row == i2, where_row, 0.0), axis=0, keepdims=True)

    fields = [i1 - ROUTER_LANE0, i2 - ROUTER_LANE0, w1, w2, pos1, pos2]
    res = jnp.concatenate(fields + [jnp.zeros((SUBLANES - len(fields), tr), F32)], axis=0)
    rt_ref[...] = res
    rn_ref[...] = jnp.concatenate([res, jnp.zeros((LANES - SUBLANES, tr), F32)], axis=0).T


def _router(xt, rg_w, rg_b, re_w, re_b):
    n_tok, d = xt.shape
    tr = ROUTER_TILE
    w_all = jnp.concatenate([rg_w, jnp.transpose(re_w, (1, 0, 2)).reshape(d, N_EXPERTS)], axis=1)
    b_all = jnp.concatenate([rg_b, re_b.reshape(N_EXPERTS)])
    half = LANES // 2
    pad = half - w_all.shape[1]
    w_hi = w_all.astype(BF16)
    w_lo = (w_all - w_hi.astype(F32)).astype(BF16)
    w_all = jnp.concatenate([jnp.pad(w_hi, ((0, 0), (0, pad))), jnp.pad(w_lo, ((0, 0), (0, pad)))], axis=1)
    b_t = jnp.broadcast_to(jnp.pad(b_all, (0, ROUTER_ROWS - b_all.shape[0]))[:, None], (ROUTER_ROWS, tr))
    return pl.pallas_call(
        _router_kernel,
        out_shape=(
            jax.ShapeDtypeStruct((SUBLANES, n_tok), F32),
            jax.ShapeDtypeStruct((n_tok, LANES), F32),
            jax.ShapeDtypeStruct((n_tok // tr, ROUTER_ROWS, LANES), F32),
        ),
        grid=(n_tok // tr,),
        in_specs=[
            pl.BlockSpec((tr, d), lambda i: (i, 0)),
            _const_spec((d, LANES)),
            _const_spec((ROUTER_ROWS, tr)),
        ],
        out_specs=(
            pl.BlockSpec((SUBLANES, tr), lambda i: (0, i)),
            pl.BlockSpec((tr, LANES), lambda i: (i, 0)),
            pl.BlockSpec((None, ROUTER_ROWS, LANES), lambda i: (i, 0, 0)),
        ),
        scratch_shapes=[pltpu.VMEM((tr, tr), BF16), pltpu.VMEM((LANES, LANES), BF16)],
        compiler_params=pltpu.CompilerParams(
            dimension_semantics=("arbitrary",), vmem_limit_bytes=VMEM_LIMIT),
        name="router",
    )(xt, w_all, b_t)


def _round_up(v, m):
    return ((v + m - 1) // m) * m


def _num_expert_blocks(n_tok):
    n_tiles = n_tok // MOE_TILE
    rows = n_tok * TOP_K + n_tiles * N_EXPERTS * (SEG_ALIGN - 1)
    return -(-rows // EXPERT_ROWS) + N_EXPERTS


def _routing_tables(cnt, n_tok):
    counts = cnt[:, ROUTER_LANE0:ROUTER_LANE0 + N_EXPERTS, 0].astype(I32)
    seg = _round_up(counts, SEG_ALIGN)
    tile_off = jnp.cumsum(seg, axis=1) - seg
    tot = jnp.sum(seg, axis=0)
    reg = _round_up(tot, EXPERT_ROWS)
    reg_end = jnp.cumsum(reg)
    reg_start = reg_end - reg
    glob_off = reg_start[None, :] + jnp.cumsum(seg, axis=0) - seg
    n_blocks = _num_expert_blocks(n_tok)
    blk_start = jnp.arange(n_blocks, dtype=I32) * EXPERT_ROWS
    block_e = jnp.sum((reg_end[None, :] <= blk_start[:, None]).astype(I32), axis=1)
    block_e = jnp.minimum(block_e, N_EXPERTS - 1).astype(I32)
    n_used = (reg_end[-1] // EXPERT_ROWS).astype(I32).reshape(1)
    groups = lambda a: (a.reshape(-1) // SEG_ALIGN).astype(I32)
    big = (jnp.max(seg, axis=1) >= INLINE_GROUPS * SEG_ALIGN).astype(I32)
    ar = jnp.arange(N_EXPERTS, dtype=I32)
    later = jnp.where(jnp.logical_and(ar[None, :] > ar[:, None], reg[None, :] > 0), ar[None, :], N_EXPERTS)
    next_of = jnp.min(later, axis=1)
    next_of = jnp.where(next_of < N_EXPERTS, next_of, ar)
    next_e = jnp.sum(jnp.where(block_e[:, None] == ar[None, :], next_of[None, :], 0), axis=1).astype(I32)
    return dict(tile_off=groups(tile_off), glob_off=groups(glob_off), seg=groups(seg), big=big,
                tail_start=groups(reg_start + tot), tail_len=groups(reg - tot),
                block_e=block_e, n_used=n_used, next_e=next_e)


def _grouped(rows):
    return rows.reshape(rows.shape[0] // SEG_ALIGN, SEG_ALIGN, rows.shape[1])


def _ungrouped(groups):
    return groups.reshape(groups.shape[0] * SEG_ALIGN, groups.shape[2])


def _for_each_piece(length, max_len, fn):
    off = 0
    for size in reversed([1 << b for b in range(max_len.bit_length())]):
        take = (length & size) != 0

        @pl.when(take)
        def _(off=off, size=size):
            fn(off, size)

        off = off + jnp.where(take, size, 0)


def _start_segment_copies(copy, toff_ref, goff_ref, seg_ref, big_ref, t, live, inline):
    def per_expert(e, limit, enabled):
        idx = t * N_EXPERTS + e
        t0 = toff_ref[idx]
        g0 = goff_ref[idx]
        _for_each_piece(jnp.where(enabled, seg_ref[idx], 0), limit,
                        lambda off, size: copy(t0 + off, g0 + off, size))

    def looped(enabled):
        def body(e, c):
            per_expert(e, MOE_TILE // SEG_ALIGN, enabled)
            return c

        lax.fori_loop(0, N_EXPERTS, body, 0)

    if not inline:
        looped(live)
        return
    big = big_ref[t] != 0
    for e in range(N_EXPERTS):
        per_expert(e, INLINE_GROUPS - 1, jnp.logical_and(live, jnp.logical_not(big)))

    @pl.when(jnp.logical_and(live, big))
    def _():
        looped(True)


def _wait_segment_copies(wait, toff_ref, seg_ref, t):
    last = t * N_EXPERTS + N_EXPERTS - 1
    _for_each_piece(toff_ref[last] + seg_ref[last], SORT_ROWS // SEG_ALIGN, lambda off, size: wait(size))


def _dispatch_kernel(toff_ref, goff_ref, seg_ref, big_ref, tstart_ref, tlen_ref, nu_ref,
                     x_ref, rt_ref, xp_hbm, obuf, zbuf, sem, zsem):
    i = pl.program_id(0)
    n = pl.num_programs(0)
    tile = x_ref.shape[0]
    slot = i % 2

    def wait_segments(t, sl):
        _wait_segment_copies(
            lambda size: pltpu.make_async_copy(obuf.at[sl, pl.ds(0, size)], xp_hbm.at[pl.ds(0, size)],
                                               sem.at[sl]).wait(),
            toff_ref, seg_ref, t)

    def start_segments(t, sl, live, inline):
        _start_segment_copies(
            lambda tg, sg, size: pltpu.make_async_copy(
                obuf.at[sl, pl.ds(tg, size)], xp_hbm.at[pl.ds(sg, size)], sem.at[sl]).start(),
            toff_ref, goff_ref, seg_ref, big_ref, t, live, inline)

    @pl.when(i >= 2)
    def _():
        wait_segments(i - 2, slot)

    start_segments(jnp.maximum(i - 1, 0), 1 - slot, i >= 1, inline=True)

    pos1 = rt_ref[4:5, :].astype(I32)
    pos2 = rt_ref[5:6, :].astype(I32)
    xb = x_ref[...].astype(BF16)
    for c in range(SORT_ROWS // SORT_CHUNK):
        rows = lax.broadcasted_iota(I32, (SORT_CHUNK, tile), 0) + c * SORT_CHUNK
        hit = jnp.logical_or(rows == pos1, rows == pos2)
        g0, g1 = c * SORT_CHUNK // SEG_ALIGN, (c + 1) * SORT_CHUNK // SEG_ALIGN
        obuf[slot, g0:g1] = _grouped(jnp.dot(
            jnp.where(hit, 1.0, 0.0).astype(BF16), xb, preferred_element_type=F32).astype(BF16))

    def zero_fill(wait):
        def per_expert(e, c):
            def piece(off, size):
                cp = pltpu.make_async_copy(zbuf.at[pl.ds(0, size)],
                                           xp_hbm.at[pl.ds(tstart_ref[e] + off, size)], zsem)
                cp.wait() if wait else cp.start()

            _for_each_piece(tlen_ref[e], EXPERT_ROWS // SEG_ALIGN - 1, piece)
            return c

        lax.fori_loop(0, N_EXPERTS, per_expert, 0)

        zg = zbuf.shape[0]

        def per_zero_copy(h, c):
            cp = pltpu.make_async_copy(zbuf, xp_hbm.at[pl.ds(h * zg, zg)], zsem)
            cp.wait() if wait else cp.start()
            return c

        per_block = EXPERT_ROWS // SEG_ALIGN // zg
        lax.fori_loop(nu_ref[0] * per_block, xp_hbm.shape[0] // zg, per_zero_copy, 0)

    @pl.when(i == 0)
    def _():
        zbuf[...] = jnp.zeros(zbuf.shape, zbuf.dtype)
        zero_fill(False)

    @pl.when(i == n - 1)
    def _():
        start_segments(i, slot, True, inline=False)

        @pl.when(i >= 1)
        def _():
            wait_segments(i - 1, 1 - slot)

        wait_segments(i, slot)
        zero_fill(True)


def _dispatch(xt, rt, tabs):
    n_tok, d = xt.shape
    tile = MOE_TILE
    n_groups = _num_expert_blocks(n_tok) * EXPERT_ROWS // SEG_ALIGN
    return pl.pallas_call(
        _dispatch_kernel,
        out_shape=jax.ShapeDtypeStruct((n_groups, SEG_ALIGN, d), BF16),
        grid_spec=pltpu.PrefetchScalarGridSpec(
            num_scalar_prefetch=7,
            grid=(n_tok // tile,),
            in_specs=[
                pl.BlockSpec((tile, d), lambda i, *_: (i, 0)),
                pl.BlockSpec((SUBLANES, tile), lambda i, *_: (0, i)),
            ],
            out_specs=pl.BlockSpec(memory_space=pl.ANY),
            scratch_shapes=[
                pltpu.VMEM((2, SORT_ROWS // SEG_ALIGN, SEG_ALIGN, d), BF16),
                pltpu.VMEM((EXPERT_ROWS // SEG_ALIGN // 2, SEG_ALIGN, d), BF16),
                pltpu.SemaphoreType.DMA((2,)),
                pltpu.SemaphoreType.DMA,
            ],
        ),
        compiler_params=pltpu.CompilerParams(
            dimension_semantics=("arbitrary",), vmem_limit_bytes=VMEM_LIMIT),
        name="dispatch",
    )(tabs["tile_off"], tabs["glob_off"], tabs["seg"], tabs["big"], tabs["tail_start"],
      tabs["tail_len"], tabs["n_used"], xt, rt)


def _ffn_kernel(be_ref, nu_ref, nxt_ref, x_ref, wg_hbm, wu_hbm, wd_hbm, o_ref,
                wg_f, wu_f, wd_f, wg_b, wu_b, wd_b, sem, *, layer):
    i = pl.program_id(0)
    used = i < nu_ref[0]
    e = be_ref[i]

    def weight_copies(expert):
        return [pltpu.make_async_copy(src.at[layer, expert], dst, sem.at[k])
                for k, (src, dst) in enumerate(((wg_hbm, wg_f), (wu_hbm, wu_f), (wd_hbm, wd_f)))]

    @pl.when(i == 0)
    def _():
        for cp in weight_copies(e):
            cp.start()

    @pl.when(jnp.logical_and(used, jnp.logical_or(i == 0, e != be_ref[jnp.maximum(i - 1, 0)])))
    def _():
        for cp in weight_copies(e):
            cp.wait()
        wg_b[...] = wg_f[...].astype(BF16)
        wu_b[...] = wu_f[...].astype(BF16)
        wd_b[...] = wd_f[...].astype(BF16)

        @pl.when(nxt_ref[i] != e)
        def _():
            for cp in weight_copies(nxt_ref[i]):
                cp.start()

    @pl.when(used)
    def _():
        xb = _ungrouped(x_ref[...])
        g = jnp.dot(xb, wg_b[...], preferred_element_type=F32)
        u = jnp.dot(xb, wu_b[...], preferred_element_type=F32)
        hb = (g * jax.nn.sigmoid(g) * u).astype(BF16)
        o_ref[...] = _grouped(jnp.dot(hb, wd_b[...], preferred_element_type=F32).astype(BF16))

    @pl.when(jnp.logical_not(used))
    def _():
        o_ref[...] = jnp.zeros(o_ref.shape, o_ref.dtype)


def _expert_ffn(xp, tabs, w_gate, w_up, w_down, layer):
    n_groups, _, d = xp.shape
    d_e = w_gate.shape[3]
    blk_groups = EXPERT_ROWS // SEG_ALIGN
    n_blocks = n_groups // blk_groups
    blk = lambda i, be, nu, nxt: (jnp.minimum(i, nu[0] - 1), 0, 0)
    return pl.pallas_call(
        functools.partial(_ffn_kernel, layer=layer),
        out_shape=jax.ShapeDtypeStruct((n_groups, SEG_ALIGN, d), BF16),
        grid_spec=pltpu.PrefetchScalarGridSpec(
            num_scalar_prefetch=3,
            grid=(n_blocks,),
            in_specs=[
                pl.BlockSpec((blk_groups, SEG_ALIGN, d), blk),
                pl.BlockSpec(memory_space=pl.ANY),
                pl.BlockSpec(memory_space=pl.ANY),
                pl.BlockSpec(memory_space=pl.ANY),
            ],
            out_specs=pl.BlockSpec((blk_groups, SEG_ALIGN, d), lambda i, be, nu, nxt: (i, 0, 0)),
            scratch_shapes=[
                pltpu.VMEM((d, d_e), F32),
                pltpu.VMEM((d, d_e), F32),
                pltpu.VMEM((d_e, d), F32),
                pltpu.VMEM((d, d_e), BF16),
                pltpu.VMEM((d, d_e), BF16),
                pltpu.VMEM((d_e, d), BF16),
                pltpu.SemaphoreType.DMA((3,)),
            ],
        ),
        compiler_params=pltpu.CompilerParams(
            dimension_semantics=("arbitrary",), vmem_limit_bytes=VMEM_LIMIT),
        name="expert_ffn",
    )(tabs["block_e"], tabs["n_used"], tabs["next_e"], xp, w_gate, w_up, w_down)


def _combine_kernel(toff_ref, goff_ref, seg_ref, big_ref, x_ref, rn_ref, yp_hbm, lng_ref, lnb_ref,
                    o_ref, ybuf, sem):
    i = pl.program_id(0)
    n = pl.num_programs(0)
    tile = x_ref.shape[0]
    slot = i % 2

    def wait_segments(t, sl):
        _wait_segment_copies(
            lambda size: pltpu.make_async_copy(yp_hbm.at[pl.ds(0, size)], ybuf.at[sl, pl.ds(0, size)],
                                               sem.at[sl]).wait(),
            toff_ref, seg_ref, t)

    def start_segments(t, sl, live, inline):
        _start_segment_copies(
            lambda tg, sg, size: pltpu.make_async_copy(
                yp_hbm.at[pl.ds(sg, size)], ybuf.at[sl, pl.ds(tg, size)], sem.at[sl]).start(),
            toff_ref, goff_ref, seg_ref, big_ref, t, live, inline)

    @pl.when(i == 0)
    def _():
        ybuf[...] = jnp.zeros(ybuf.shape, ybuf.dtype)
        start_segments(0, 0, True, inline=False)

    wait_segments(i, slot)
    start_segments(jnp.minimum(i + 1, n - 1), 1 - slot, i + 1 < n, inline=True)

    rn = rn_ref[...]
    full = lambda col: jnp.broadcast_to(col, (tile, LANES))
    w1, w2 = full(rn[:, 2:3]), full(rn[:, 3:4])
    pos1, pos2 = full(rn[:, 4:5]), full(rn[:, 5:6])
    lane = lax.broadcasted_iota(I32, (tile, LANES), 1).astype(F32)

    def pick_lanes(col0):
        cols = lane + float(col0)
        return jnp.where(cols == pos1, w1, 0.0) + jnp.where(cols == pos2, w2, 0.0)

    ffn = jnp.zeros(o_ref.shape, F32)
    for c in range(SORT_ROWS // SORT_CHUNK):
        pick = jnp.concatenate([pick_lanes(c * SORT_CHUNK + l0) for l0 in range(0, SORT_CHUNK, LANES)],
                               axis=1).astype(BF16)
        g0, g1 = c * SORT_CHUNK // SEG_ALIGN, (c + 1) * SORT_CHUNK // SEG_ALIGN
        ffn = ffn + jnp.dot(pick, _ungrouped(ybuf[slot, g0:g1]), preferred_element_type=F32)
    o_ref[...] = _layer_norm(ALPHA * x_ref[...] + ffn, lng_ref[...], lnb_ref[...])


def _combine(xt, rn, yp, tabs, ln_g, ln_b):
    n_tok, d = xt.shape
    tile = MOE_TILE
    return pl.pallas_call(
        _combine_kernel,
        out_shape=jax.ShapeDtypeStruct((n_tok, d), F32),
        grid_spec=pltpu.PrefetchScalarGridSpec(
            num_scalar_prefetch=4,
            grid=(n_tok // tile,),
            in_specs=[
                pl.BlockSpec((tile, d), lambda i, *_: (i, 0)),
                pl.BlockSpec((tile, LANES), lambda i, *_: (i, 0)),
                pl.BlockSpec(memory_space=pl.ANY),
                pl.BlockSpec((1, d), lambda i, *_: (0, 0)),
                pl.BlockSpec((1, d), lambda i, *_: (0, 0)),
            ],
            out_specs=pl.BlockSpec((tile, d), lambda i, *_: (i, 0)),
            scratch_shapes=[
                pltpu.VMEM((2, SORT_ROWS // SEG_ALIGN, SEG_ALIGN, d), BF16),
                pltpu.SemaphoreType.DMA((2,)),
            ],
        ),
        compiler_params=pltpu.CompilerParams(
            dimension_semantics=("arbitrary",), vmem_limit_bytes=VMEM_LIMIT),
        name="combine",
    )(tabs["tile_off"], tabs["glob_off"], tabs["seg"], tabs["big"], xt, rn, yp,
      ln_g.reshape(1, d), ln_b.reshape(1, d))


def _moe_layer(x, rg_w, rg_b, re_w, re_b, w_gate, w_up, w_down, layer, ln_g, ln_b):
    bt, s_len, d = x.shape
    n_tok = bt * s_len
    xt = x.reshape(n_tok, d)
    rt, rn, cnt = _router(xt, rg_w, rg_b, re_w, re_b)
    tabs = _routing_tables(cnt, n_tok)
    xp = _dispatch(xt, rt, tabs)
    yp = _expert_ffn(xp, tabs, w_gate, w_up, w_down, layer)
    out = _combine(xt, rn, yp, tabs, ln_g, ln_b)
    return out.reshape(bt, s_len, d)


def kernel(x, ab_w_in, ab_b_in, a_dw, a_dw_b, a_ln_g, a_ln_b, b_dw, ab_w_out, cd_w_in, cd_b_in, c_ln_g, c_ln_b, c_ws, c_ws_b, d_rel_bias, cd_w_out, mix_ln_g, mix_ln_b, moe_rg_w, moe_rg_b, moe_re_w, moe_re_b, moe_w_gate, moe_w_up, moe_w_down, ffn_ln_g, ffn_ln_b):
    for layer in range(DEPTH):
        i = layer // 2
        if layer % 2 == 0:
            x = _ab_mixer(x, ab_w_in[i], ab_b_in[i], a_dw[i], a_dw_b[i], a_ln_g[i], a_ln_b[i],
                          b_dw[i], ab_w_out[i], mix_ln_g[layer], mix_ln_b[layer])
        else:
            x = _cd_mixer(x, cd_w_in[i], cd_b_in[i], c_ln_g[i], c_ln_b[i], c_ws[i], c_ws_b[i],
                          d_rel_bias[i], cd_w_out[i], mix_ln_g[layer], mix_ln_b[layer])
        x = _moe_layer(x, moe_rg_w[layer], moe_rg_b[layer], moe_re_w[layer], moe_re_b[layer],
                       moe_w_gate, moe_w_up, moe_w_down, layer,
                       ffn_ln_g[layer], ffn_ln_b[layer])
    return x
```

```python
import functools

import jax
import jax.numpy as jnp
from jax import lax
from jax.experimental import pallas as pl
from jax.experimental.pallas import tpu as pltpu

F32 = jnp.float32
BF16 = jnp.bfloat16
I32 = jnp.int32

DEPTH = 2
CHUNK = 64
LEFT_CHUNKS = 8
HIST = LEFT_CHUNKS * CHUNK
CONV_A = 31
CONV_B = 3
HEADS_C = 8
SGU_BLOCK = 128
HEADS_D = 8
MAX_REL = 256
N_GROUPS = 4
EXPERTS_PER_GROUP = 8
N_EXPERTS = N_GROUPS * EXPERTS_PER_GROUP
TOP_K = 2
ALPHA = (2 * DEPTH) ** 0.25
LN_EPS = 1e-5
NEG_INF = -1e30
LOG2_E = 1.4426950408889634

LANES = 128
SUBLANES = 8
VMEM_LIMIT = 56 * 1024 * 1024

SEQ_TILE = 512
CONV_ROWS = 64
HALO = 32
Q_ROWS = 256
K_ROWS = Q_ROWS + HIST
MOE_TILE = 512
ROUTER_TILE = MOE_TILE
EXPERT_ROWS = 512
SEG_ALIGN = 2 * SUBLANES
SORT_ROWS = MOE_TILE * TOP_K + N_EXPERTS * SEG_ALIGN
SORT_CHUNK = 256
INLINE_GROUPS = 4
ROUTER_LANE0 = N_GROUPS
ROUTER_ROWS = -(-(N_GROUPS + N_EXPERTS) // SUBLANES) * SUBLANES
_B_SHIFTS = sorted({(HALO - (CONV_B - 1) + j) % SUBLANES for j in range(CONV_B)} - {0})


def _layer_norm(x, g, b):
    mu = jnp.mean(x, axis=-1, keepdims=True)
    xc = x - mu
    var = jnp.mean(xc * xc, axis=-1, keepdims=True)
    return xc * lax.rsqrt(var + LN_EPS) * g + b


def _const_spec(shape):
    nd = len(shape)
    return pl.BlockSpec(shape, lambda *_: (0,) * nd, pipeline_mode=pl.Buffered(1))


def _ab_mixer_kernel(x_ref, win_ref, bin_ref, adw_ref, adwb_ref, alng_ref, alnb_ref,
                     bdw_ref, wout_ref, lng_ref, lnb_ref, o_ref,
                     abuf, cbuf, gbuf, mbuf, asht, csht, *, d_a, d_b):
    ts = x_ref.shape[0]

    @pl.when(pl.program_id(1) == 0)
    def _():
        abuf[0:HALO, :] = jnp.zeros((HALO, d_a), F32)
        cbuf[0:HALO, :] = jnp.zeros((HALO, d_b), F32)

    x = x_ref[...]
    xb = x.astype(BF16)
    n_sh = HALO + ts - SUBLANES

    def tap(buf, sht, slot_of, r0, off):
        base = r0 + (off // SUBLANES) * SUBLANES
        if off % SUBLANES == 0:
            return buf[base:base + CONV_ROWS, :]
        return sht[slot_of(off % SUBLANES), base:base + CONV_ROWS, :]

    h_a = jnp.dot(xb, win_ref[:, 0:2 * d_a], preferred_element_type=F32) + bin_ref[:, 0:2 * d_a]
    abuf[HALO:HALO + ts, :] = h_a[:, 0:d_a] * jax.nn.sigmoid(h_a[:, d_a:])
    for sh in range(1, SUBLANES):
        asht[sh, 0:n_sh, :] = abuf[sh:sh + n_sh, :]
    for r0 in range(0, ts, CONV_ROWS):
        acc = jnp.broadcast_to(adwb_ref[...], (CONV_ROWS, d_a))
        for j in range(CONV_A):
            acc = acc + adw_ref[j:j + 1, :] * tap(abuf, asht, lambda sh: sh, r0, HALO - (CONV_A - 1) + j)
        a = _layer_norm(acc, alng_ref[...], alnb_ref[...])
        mbuf[r0:r0 + CONV_ROWS, 0:d_a] = (a * jax.nn.sigmoid(a)).astype(BF16)

    h_b = jnp.dot(xb, win_ref[:, 2 * d_a:], preferred_element_type=F32) + bin_ref[:, 2 * d_a:]
    gbuf[...] = h_b[:, 0:d_b]
    cbuf[HALO:HALO + ts, :] = h_b[:, d_b:2 * d_b] * h_b[:, 2 * d_b:]
    for sh in _B_SHIFTS:
        csht[_B_SHIFTS.index(sh), 0:n_sh, :] = cbuf[sh:sh + n_sh, :]
    for r0 in range(0, ts, CONV_ROWS):
        s = jnp.zeros((CONV_ROWS, d_b), F32)
        for j in range(CONV_B):
            s = s + bdw_ref[j:j + 1, :] * tap(cbuf, csht, _B_SHIFTS.index, r0, HALO - (CONV_B - 1) + j)
        mbuf[r0:r0 + CONV_ROWS, d_a:d_a + d_b] = (gbuf[r0:r0 + CONV_ROWS, :] * s).astype(BF16)

    abuf[0:HALO, :] = abuf[ts:ts + HALO, :]
    cbuf[0:HALO, :] = cbuf[ts:ts + HALO, :]

    mix = jnp.dot(mbuf[...], wout_ref[...], preferred_element_type=F32)
    o_ref[...] = _layer_norm(ALPHA * x + mix, lng_ref[...], lnb_ref[...])


def _ab_mixer(x, w_in, b_in, a_dw, a_dw_b, a_ln_g, a_ln_b, b_dw, w_out, ln_g, ln_b):
    bt, s_len, d = x.shape
    d_a = a_dw.shape[1]
    d_b = b_dw.shape[1]
    w_in_w = w_in.shape[1]
    ts = SEQ_TILE
    row = lambda v: v.reshape(1, -1)
    kern = functools.partial(_ab_mixer_kernel, d_a=d_a, d_b=d_b)
    return pl.pallas_call(
        kern,
        out_shape=jax.ShapeDtypeStruct((bt, s_len, d), F32),
        grid=(bt, s_len // ts),
        in_specs=[
            pl.BlockSpec((None, ts, d), lambda b, t: (b, t, 0)),
            _const_spec((d, w_in_w)),
            _const_spec((1, w_in_w)),
            _const_spec((CONV_A, d_a)),
            _const_spec((1, d_a)),
            _const_spec((1, d_a)),
            _const_spec((1, d_a)),
            _const_spec((CONV_B, d_b)),
            _const_spec((d_a + d_b, d)),
            _const_spec((1, d)),
            _const_spec((1, d)),
        ],
        out_specs=pl.BlockSpec((None, ts, d), lambda b, t: (b, t, 0)),
        scratch_shapes=[
            pltpu.VMEM((HALO + ts, d_a), F32),
            pltpu.VMEM((HALO + ts, d_b), F32),
            pltpu.VMEM((ts, d_b), F32),
            pltpu.VMEM((ts, d_a + d_b), BF16),
            pltpu.VMEM((SUBLANES, HALO + ts, d_a), F32),
            pltpu.VMEM((len(_B_SHIFTS), HALO + ts, d_b), F32),
        ],
        compiler_params=pltpu.CompilerParams(
            dimension_semantics=("arbitrary", "arbitrary"), vmem_limit_bytes=VMEM_LIMIT),
        name="ab_mixer",
    )(x, w_in.astype(BF16), row(b_in), a_dw, row(a_dw_b), row(a_ln_g), row(a_ln_b),
      b_dw, w_out.astype(BF16), row(ln_g), row(ln_b))


def _cd_mixer_kernel(x_ref, win_ref, bin_ref, clng_ref, clnb_ref, wsc_ref, wsb_ref,
                     bias_ref, wout_ref, lng_ref, lnb_ref, o_ref,
                     qbuf, kbuf, vbuf, wsm, vbd, mbuf, *, d_c, d_d):
    ts = x_ref.shape[0]
    first_in_seq = pl.program_id(1) == 0
    hc = d_c // HEADS_C
    hd = d_d // HEADS_D

    @pl.when(first_in_seq)
    def _():
        kbuf[0:HIST, :] = jnp.zeros((HIST, d_d), BF16)
        vbuf[0:HIST, :] = jnp.zeros((HIST, d_d), BF16)
        r = lax.broadcasted_iota(I32, (SGU_BLOCK, HEADS_C * SGU_BLOCK), 0)
        c = lax.broadcasted_iota(I32, (SGU_BLOCK, HEADS_C * SGU_BLOCK), 1) % SGU_BLOCK
        wsm[...] = jnp.where(c // CHUNK <= r // CHUNK, wsc_ref[...], 0.0).astype(BF16)

    x = x_ref[...]
    xb = x.astype(BF16)
    h_att = (jnp.dot(xb, win_ref[:, 2 * d_c:], preferred_element_type=F32) + bin_ref[:, 2 * d_c:])
    qbuf[...] = (h_att[:, 0:d_d] * (hd ** -0.5 * LOG2_E)).astype(BF16)
    kbuf[HIST:HIST + ts, :] = h_att[:, d_d:2 * d_d].astype(BF16)
    vbuf[HIST:HIST + ts, :] = h_att[:, 2 * d_d:].astype(BF16)

    def attention(qb):
        q0 = qb * Q_ROWS
        col = lax.broadcasted_iota(I32, (1, K_ROWS), 1)
        no_key = jnp.where(jnp.logical_and(first_in_seq, col + q0 < HIST), NEG_INF, 0.0)
        for hh in range(HEADS_D):
            q = qbuf[q0:q0 + Q_ROWS, hh * hd:(hh + 1) * hd]
            k = kbuf[q0:q0 + K_ROWS, hh * hd:(hh + 1) * hd]
            v = vbuf[q0:q0 + K_ROWS, hh * hd:(hh + 1) * hd]
            s = lax.dot_general(q, k, (((1,), (1,)), ((), ())), preferred_element_type=F32)
            s = s + bias_ref[hh] + no_key
            p = jnp.exp2(s - jnp.max(s, axis=-1, keepdims=True))
            l = jnp.sum(p, axis=-1, keepdims=True)
            o = jnp.dot(p.astype(BF16), v, preferred_element_type=F32)
            mbuf[q0:q0 + Q_ROWS, d_c + hh * hd:d_c + (hh + 1) * hd] = (o / l).astype(BF16)

    for qb in range(ts // Q_ROWS):
        attention(qb)

    kbuf[0:HIST, :] = kbuf[ts:ts + HIST, :]
    vbuf[0:HIST, :] = vbuf[ts:ts + HIST, :]

    h_sgu = (jnp.dot(xb, win_ref[:, 0:2 * d_c], preferred_element_type=F32) + bin_ref[:, 0:2 * d_c])
    u = h_sgu[:, 0:d_c]
    vn = _layer_norm(h_sgu[:, d_c:], clng_ref[...], clnb_ref[...])
    lane_head = lax.broadcasted_iota(I32, (SGU_BLOCK, d_c), 1) // hc
    for nb in range(ts // SGU_BLOCK):
        v_blk = vn[nb * SGU_BLOCK:(nb + 1) * SGU_BLOCK, :]
        for hh in range(HEADS_C):
            vbd[hh * SGU_BLOCK:(hh + 1) * SGU_BLOCK, :] = jnp.where(
                lane_head == hh, v_blk, 0.0).astype(BF16)
        gate = jnp.dot(wsm[...], vbd[...], preferred_element_type=F32) + wsb_ref[...]
        mbuf[nb * SGU_BLOCK:(nb + 1) * SGU_BLOCK, 0:d_c] = (
            u[nb * SGU_BLOCK:(nb + 1) * SGU_BLOCK, :] * gate).astype(BF16)

    mix = jnp.dot(mbuf[...], wout_ref[...], preferred_element_type=F32)
    o_ref[...] = _layer_norm(ALPHA * x + mix, lng_ref[...], lnb_ref[...])


def _attention_bias(d_rel_bias):
    heads = d_rel_bias.shape[0]
    i = jnp.arange(Q_ROWS)[:, None]
    j = jnp.arange(K_ROWS)[None, :]
    jb = j - (i // CHUNK) * CHUNK
    in_band = jnp.logical_and(jb >= 0, jb < (LEFT_CHUNKS + 1) * CHUNK)
    n_diag = Q_ROWS + K_ROWS - 1
    m = jnp.arange(n_diag)
    diag = d_rel_bias[:, jnp.clip(Q_ROWS - 1 + HIST - m, -MAX_REL, MAX_REL) + MAX_REL].astype(F32)
    period = jnp.pad(diag, ((0, 0), (0, 1)))
    flat = jnp.tile(period, (1, Q_ROWS))[:, :Q_ROWS * n_diag]
    rel = flat.reshape(heads, Q_ROWS, n_diag)[:, :, Q_ROWS - 1:Q_ROWS - 1 + K_ROWS]
    return jnp.where(in_band[None], rel * LOG2_E, NEG_INF)


def _cd_mixer(x, w_in, b_in, c_ln_g, c_ln_b, c_ws, c_ws_b, d_rel_bias, w_out, ln_g, ln_b):
    bt, s_len, d = x.shape
    d_c = c_ln_g.shape[0]
    d_d = w_out.shape[0] - d_c
    w_in_w = w_in.shape[1]
    ts = SEQ_TILE
    assert ts == HIST and ts % Q_ROWS == 0 and ts % SGU_BLOCK == 0
    row = lambda v: v.reshape(1, -1)
    ws_cat = jnp.transpose(c_ws, (1, 0, 2)).reshape(SGU_BLOCK, HEADS_C * SGU_BLOCK)
    bias_full = jnp.repeat(c_ws_b.T, d_c // HEADS_C, axis=1)
    att_bias = _attention_bias(d_rel_bias)
    kern = functools.partial(_cd_mixer_kernel, d_c=d_c, d_d=d_d)
    return pl.pallas_call(
        kern,
        out_shape=jax.ShapeDtypeStruct((bt, s_len, d), F32),
        grid=(bt, s_len // ts),
        in_specs=[
            pl.BlockSpec((None, ts, d), lambda b, t: (b, t, 0)),
            _const_spec((d, w_in_w)),
            _const_spec((1, w_in_w)),
            _const_spec((1, d_c)),
            _const_spec((1, d_c)),
            _const_spec((SGU_BLOCK, HEADS_C * SGU_BLOCK)),
            _const_spec((SGU_BLOCK, d_c)),
            _const_spec((HEADS_D, Q_ROWS, K_ROWS)),
            _const_spec((d_c + d_d, d)),
            _const_spec((1, d)),
            _const_spec((1, d)),
        ],
        out_specs=pl.BlockSpec((None, ts, d), lambda b, t: (b, t, 0)),
        scratch_shapes=[
            pltpu.VMEM((ts, d_d), BF16),
            pltpu.VMEM((HIST + ts, d_d), BF16),
            pltpu.VMEM((HIST + ts, d_d), BF16),
            pltpu.VMEM((SGU_BLOCK, HEADS_C * SGU_BLOCK), BF16),
            pltpu.VMEM((HEADS_C * SGU_BLOCK, d_c), BF16),
            pltpu.VMEM((ts, d_c + d_d), BF16),
        ],
        compiler_params=pltpu.CompilerParams(
            dimension_semantics=("arbitrary", "arbitrary"), vmem_limit_bytes=VMEM_LIMIT),
        name="cd_mixer",
    )(x, w_in.astype(BF16), row(b_in), row(c_ln_g), row(c_ln_b), ws_cat, bias_full,
      att_bias, w_out.astype(BF16), row(ln_g), row(ln_b))


def _router_kernel(x_ref, wr_ref, bt_ref, rt_ref, rn_ref, cnt_ref, after_ref, below_ref):
    tr = x_ref.shape[0]
    nr = ROUTER_ROWS

    @pl.when(pl.program_id(0) == 0)
    def _():
        rr = lax.broadcasted_iota(I32, (tr, tr), 0)
        cc = lax.broadcasted_iota(I32, (tr, tr), 1)
        after_ref[...] = jnp.where(rr < cc, 1.0, 0.0).astype(BF16)
        lr = lax.broadcasted_iota(I32, (LANES, LANES), 0)
        lc = lax.broadcasted_iota(I32, (LANES, LANES), 1)
        below_ref[...] = jnp.where(lc < lr, 1.0, 0.0).astype(BF16)

    x = x_ref[...]
    x_hi = x.astype(BF16)
    x_lo = (x - x_hi.astype(F32)).astype(BF16)
    p_hi = jnp.dot(x_hi, wr_ref[...], preferred_element_type=F32)
    p_lo = jnp.dot(x_lo, wr_ref[...], preferred_element_type=F32)
    logits = (p_hi + pltpu.roll(p_hi, LANES // 2, axis=1) + p_lo).T[0:nr] + bt_ref[...]
    row = lax.broadcasted_iota(I32, (nr, tr), 0).astype(F32)

    def top1(vals):
        m = jnp.max(vals, axis=0, keepdims=True)
        idx = jnp.min(jnp.where(vals == m, row, float(nr)), axis=0, keepdims=True)
        return m, idx

    g_mask = row < N_GROUPS
    g_max, g_idx = top1(jnp.where(g_mask, logits, NEG_INF))
    g_top = 1.0 / jnp.sum(jnp.where(g_mask, jnp.exp(logits - g_max), 0.0), axis=0, keepdims=True)

    e_lo = ROUTER_LANE0 + EXPERTS_PER_GROUP * g_idx
    e_vals = jnp.where(jnp.logical_and(row >= e_lo, row < e_lo + EXPERTS_PER_GROUP), logits, NEG_INF)
    m1, i1 = top1(e_vals)
    m2, i2 = top1(jnp.where(row == i1, NEG_INF, e_vals))
    ratio = jnp.exp(m2 - m1)
    w1 = g_top / (1.0 + ratio)
    w2 = g_top * ratio / (1.0 + ratio)

    chosen = jnp.logical_or(row == i1, row == i2)
    onehot = jnp.where(chosen, 1.0, 0.0)
    seen = jnp.dot(onehot.astype(BF16), after_ref[...], preferred_element_type=F32)
    counts = jnp.broadcast_to(jnp.sum(onehot, axis=1, keepdims=True), (nr, LANES))
    cnt_ref[...] = counts

    groups = ((counts.astype(I32) + (SEG_ALIGN - 1)) // SEG_ALIGN).astype(F32).astype(BF16)
    groups = jnp.concatenate([groups, jnp.zeros((LANES - nr, LANES), BF16)], axis=0)
    seg_start = SEG_ALIGN * jnp.dot(below_ref[...], groups, preferred_element_type=F32)[0:nr]
    where_row = seen + jnp.concatenate([seg_start] * (tr // LANES), axis=1)
    pos1 = jnp.sum(jnp.where(row == i1, where_row, 0.0), axis=0, keepdims=True)
    pos2 = jnp.sum(jnp.where(row == i2, where_row, 0.0), axis=0, keepdims=True)

    fields = [i1 - ROUTER_LANE0, i2 - ROUTER_LANE0, w1, w2, pos1, pos2]
    res = jnp.concatenate(fields + [jnp.zeros((SUBLANES - len(fields), tr), F32)], axis=0)
    rt_ref[...] = res
    rn_ref[...] = jnp.concatenate([res, jnp.zeros((LANES - SUBLANES, tr), F32)], axis=0).T


def _router(xt, rg_w, rg_b, re_w, re_b):
    n_tok, d = xt.shape
    tr = ROUTER_TILE
    w_all = jnp.concatenate([rg_w, jnp.transpose(re_w, (1, 0, 2)).reshape(d, N_EXPERTS)], axis=1)
    b_all = jnp.concatenate([rg_b, re_b.reshape(N_EXPERTS)])
    half = LANES // 2
    pad = half - w_all.shape[1]
    w_hi = w_all.astype(BF16)
    w_lo = (w_all - w_hi.astype(F32)).astype(BF16)
    w_all = jnp.concatenate([jnp.pad(w_hi, ((0, 0), (0, pad))), jnp.pad(w_lo, ((0, 0), (0, pad)))], axis=1)
    b_t = jnp.broadcast_to(jnp.pad(b_all, (0, ROUTER_ROWS - b_all.shape[0]))[:, None], (ROUTER_ROWS, tr))
    return pl.pallas_call(
        _router_kernel,
        out_shape=(
            jax.ShapeDtypeStruct((SUBLANES, n_tok), F32),
            jax.ShapeDtypeStruct((n_tok, LANES), F32),
            jax.ShapeDtypeStruct((n_tok // tr, ROUTER_ROWS, LANES), F32),
        ),
        grid=(n_tok // tr,),
        in_specs=[
            pl.BlockSpec((tr, d), lambda i: (i, 0)),
            _const_spec((d, LANES)),
            _const_spec((ROUTER_ROWS, tr)),
        ],
        out_specs=(
            pl.BlockSpec((SUBLANES, tr), lambda i: (0, i)),
            pl.BlockSpec((tr, LANES), lambda i: (i, 0)),
            pl.BlockSpec((None, ROUTER_ROWS, LANES), lambda i: (i, 0, 0)),
        ),
        scratch_shapes=[pltpu.VMEM((tr, tr), BF16), pltpu.VMEM((LANES, LANES), BF16)],
        compiler_params=pltpu.CompilerParams(
            dimension_semantics=("arbitrary",), vmem_limit_bytes=VMEM_LIMIT),
        name="router",
    )(xt, w_all, b_t)


def _round_up(v, m):
    return ((v + m - 1) // m) * m


def _num_expert_blocks(n_tok):
    n_tiles = n_tok // MOE_TILE
    rows = n_tok * TOP_K + n_tiles * N_EXPERTS * (SEG_ALIGN - 1)
    return -(-rows // EXPERT_ROWS) + N_EXPERTS


def _routing_tables(cnt, n_tok):
    counts = cnt[:, ROUTER_LANE0:ROUTER_LANE0 + N_EXPERTS, 0].astype(I32)
    seg = _round_up(counts, SEG_ALIGN)
    tile_off = jnp.cumsum(seg, axis=1) - seg
    tot = jnp.sum(seg, axis=0)
    reg = _round_up(tot, EXPERT_ROWS)
    reg_end = jnp.cumsum(reg)
    reg_start = reg_end - reg
    glob_off = reg_start[None, :] + jnp.cumsum(seg, axis=0) - seg
    n_blocks = _num_expert_blocks(n_tok)
    blk_start = jnp.arange(n_blocks, dtype=I32) * EXPERT_ROWS
    block_e = jnp.sum((reg_end[None, :] <= blk_start[:, None]).astype(I32), axis=1)
    block_e = jnp.minimum(block_e, N_EXPERTS - 1).astype(I32)
    n_used = (reg_end[-1] // EXPERT_ROWS).astype(I32).reshape(1)
    groups = lambda a: (a.reshape(-1) // SEG_ALIGN).astype(I32)
    big = (jnp.max(seg, axis=1) >= INLINE_GROUPS * SEG_ALIGN).astype(I32)
    ar = jnp.arange(N_EXPERTS, dtype=I32)
    later = jnp.where(jnp.logical_and(ar[None, :] > ar[:, None], reg[None, :] > 0), ar[None, :], N_EXPERTS)
    next_of = jnp.min(later, axis=1)
    next_of = jnp.where(next_of < N_EXPERTS, next_of, ar)
    next_e = jnp.sum(jnp.where(block_e[:, None] == ar[None, :], next_of[None, :], 0), axis=1).astype(I32)
    return dict(tile_off=groups(tile_off), glob_off=groups(glob_off), seg=groups(seg), big=big,
                tail_start=groups(reg_start + tot), tail_len=groups(reg - tot),
                block_e=block_e, n_used=n_used, next_e=next_e)


def _grouped(rows):
    return rows.reshape(rows.shape[0] // SEG_ALIGN, SEG_ALIGN, rows.shape[1])


def _ungrouped(groups):
    return groups.reshape(groups.shape[0] * SEG_ALIGN, groups.shape[2])


def _for_each_piece(length, max_len, fn):
    off = 0
    for size in reversed([1 << b for b in range(max_len.bit_length())]):
        take = (length & size) != 0

        @pl.when(take)
        def _(off=off, size=size):
            fn(off, size)

        off = off + jnp.where(take, size, 0)


def _start_segment_copies(copy, toff_ref, goff_ref, seg_ref, big_ref, t, live, inline):
    def per_expert(e, limit, enabled):
        idx = t * N_EXPERTS + e
        t0 = toff_ref[idx]
        g0 = goff_ref[idx]
        _for_each_piece(jnp.where(enabled, seg_ref[idx], 0), limit,
                        lambda off, size: copy(t0 + off, g0 + off, size))

    def looped(enabled):
        def body(e, c):
            per_expert(e, MOE_TILE // SEG_ALIGN, enabled)
            return c

        lax.fori_loop(0, N_EXPERTS, body, 0)

    if not inline:
        looped(live)
        return
    big = big_ref[t] != 0
    for e in range(N_EXPERTS):
        per_expert(e, INLINE_GROUPS - 1, jnp.logical_and(live, jnp.logical_not(big)))

    @pl.when(jnp.logical_and(live, big))
    def _():
        looped(True)


def _wait_segment_copies(wait, toff_ref, seg_ref, t):
    last = t * N_EXPERTS + N_EXPERTS - 1
    _for_each_piece(toff_ref[last] + seg_ref[last], SORT_ROWS // SEG_ALIGN, lambda off, size: wait(size))


def _dispatch_kernel(toff_ref, goff_ref, seg_ref, big_ref, tstart_ref, tlen_ref, nu_ref,
                     x_ref, rt_ref, xp_hbm, obuf, zbuf, sem, zsem):
    i = pl.program_id(0)
    n = pl.num_programs(0)
    tile = x_ref.shape[0]
    slot = i % 2

    def wait_segments(t, sl):
        _wait_segment_copies(
            lambda size: pltpu.make_async_copy(obuf.at[sl, pl.ds(0, size)], xp_hbm.at[pl.ds(0, size)],
                                               sem.at[sl]).wait(),
            toff_ref, seg_ref, t)

    def start_segments(t, sl, live, inline):
        _start_segment_copies(
            lambda tg, sg, size: pltpu.make_async_copy(
                obuf.at[sl, pl.ds(tg, size)], xp_hbm.at[pl.ds(sg, size)], sem.at[sl]).start(),
            toff_ref, goff_ref, seg_ref, big_ref, t, live, inline)

    @pl.when(i >= 2)
    def _():
        wait_segments(i - 2, slot)

    start_segments(jnp.maximum(i - 1, 0), 1 - slot, i >= 1, inline=True)

    pos1 = rt_ref[4:5, :].astype(I32)
    pos2 = rt_ref[5:6, :].astype(I32)
    xb = x_ref[...].astype(BF16)
    for c in range(SORT_ROWS // SORT_CHUNK):
        rows = lax.broadcasted_iota(I32, (SORT_CHUNK, tile), 0) + c * SORT_CHUNK
        hit = jnp.logical_or(rows == pos1, rows == pos2)
        g0, g1 = c * SORT_CHUNK // SEG_ALIGN, (c + 1) * SORT_CHUNK // SEG_ALIGN
        obuf[slot, g0:g1] = _grouped(jnp.dot(
            jnp.where(hit, 1.0, 0.0).astype(BF16), xb, preferred_element_type=F32).astype(BF16))

    def zero_fill(wait):
        def per_expert(e, c):
            def piece(off, size):
                cp = pltpu.make_async_copy(zbuf.at[pl.ds(0, size)],
                                           xp_hbm.at[pl.ds(tstart_ref[e] + off, size)], zsem)
                cp.wait() if wait else cp.start()

            _for_each_piece(tlen_ref[e], EXPERT_ROWS // SEG_ALIGN - 1, piece)
            return c

        lax.fori_loop(0, N_EXPERTS, per_expert, 0)

        zg = zbuf.shape[0]

        def per_zero_copy(h, c):
            cp = pltpu.make_async_copy(zbuf, xp_hbm.at[pl.ds(h * zg, zg)], zsem)
            cp.wait() if wait else cp.start()
            return c

        per_block = EXPERT_ROWS // SEG_ALIGN // zg
        lax.fori_loop(nu_ref[0] * per_block, xp_hbm.shape[0] // zg, per_zero_copy, 0)

    @pl.when(i == 0)
    def _():
        zbuf[...] = jnp.zeros(zbuf.shape, zbuf.dtype)
        zero_fill(False)

    @pl.when(i == n - 1)
    def _():
        start_segments(i, slot, True, inline=False)

        @pl.when(i >= 1)
        def _():
            wait_segments(i - 1, 1 - slot)

        wait_segments(i, slot)
        zero_fill(True)


def _dispatch(xt, rt, tabs):
    n_tok, d = xt.shape
    tile = MOE_TILE
    n_groups = _num_expert_blocks(n_tok) * EXPERT_ROWS // SEG_ALIGN
    return pl.pallas_call(
        _dispatch_kernel,
        out_shape=jax.ShapeDtypeStruct((n_groups, SEG_ALIGN, d), BF16),
        grid_spec=pltpu.PrefetchScalarGridSpec(
            num_scalar_prefetch=7,
            grid=(n_tok // tile,),
            in_specs=[
                pl.BlockSpec((tile, d), lambda i, *_: (i, 0)),
                pl.BlockSpec((SUBLANES, tile), lambda i, *_: (0, i)),
            ],
            out_specs=pl.BlockSpec(memory_space=pl.ANY),
            scratch_shapes=[
                pltpu.VMEM((2, SORT_ROWS // SEG_ALIGN, SEG_ALIGN, d), BF16),
                pltpu.VMEM((EXPERT_ROWS // SEG_ALIGN // 2, SEG_ALIGN, d), BF16),
                pltpu.SemaphoreType.DMA((2,)),
                pltpu.SemaphoreType.DMA,
            ],
        ),
        compiler_params=pltpu.CompilerParams(
            dimension_semantics=("arbitrary",), vmem_limit_bytes=VMEM_LIMIT),
        name="dispatch",
    )(tabs["tile_off"], tabs["glob_off"], tabs["seg"], tabs["big"], tabs["tail_start"],
      tabs["tail_len"], tabs["n_used"], xt, rt)


def _ffn_kernel(be_ref, nu_ref, nxt_ref, x_ref, wg_hbm, wu_hbm, wd_hbm, o_ref,
                wg_f, wu_f, wd_f, wg_b, wu_b, wd_b, sem, *, layer):
    i = pl.program_id(0)
    used = i < nu_ref[0]
    e = be_ref[i]

    def weight_copies(expert):
        return [pltpu.make_async_copy(src.at[layer, expert], dst, sem.at[k])
                for k, (src, dst) in enumerate(((wg_hbm, wg_f), (wu_hbm, wu_f), (wd_hbm, wd_f)))]

    @pl.when(i == 0)
    def _():
        for cp in weight_copies(e):
            cp.start()

    @pl.when(jnp.logical_and(used, jnp.logical_or(i == 0, e != be_ref[jnp.maximum(i - 1, 0)])))
    def _():
        for cp in weight_copies(e):
            cp.wait()
        wg_b[...] = wg_f[...].astype(BF16)
        wu_b[...] = wu_f[...].astype(BF16)
        wd_b[...] = wd_f[...].astype(BF16)

        @pl.when(nxt_ref[i] != e)
        def _():
            for cp in weight_copies(nxt_ref[i]):
                cp.start()

    @pl.when(used)
    def _():
        xb = _ungrouped(x_ref[...])
        g = jnp.dot(xb, wg_b[...], preferred_element_type=F32)
        u = jnp.dot(xb, wu_b[...], preferred_element_type=F32)
        hb = (g * jax.nn.sigmoid(g) * u).astype(BF16)
        o_ref[...] = _grouped(jnp.dot(hb, wd_b[...], preferred_element_type=F32).astype(BF16))

    @pl.when(jnp.logical_not(used))
    def _():
        o_ref[...] = jnp.zeros(o_ref.shape, o_ref.dtype)


def _expert_ffn(xp, tabs, w_gate, w_up, w_down, layer):
    n_groups, _, d = xp.shape
    d_e = w_gate.shape[3]
    blk_groups = EXPERT_ROWS // SEG_ALIGN
    n_blocks = n_groups // blk_groups
    blk = lambda i, be, nu, nxt: (jnp.minimum(i, nu[0] - 1), 0, 0)
    return pl.pallas_call(
        functools.partial(_ffn_kernel, layer=layer),
        out_shape=jax.ShapeDtypeStruct((n_groups, SEG_ALIGN, d), BF16),
        grid_spec=pltpu.PrefetchScalarGridSpec(
            num_scalar_prefetch=3,
            grid=(n_blocks,),
            in_specs=[
                pl.BlockSpec((blk_groups, SEG_ALIGN, d), blk),
                pl.BlockSpec(memory_space=pl.ANY),
                pl.BlockSpec(memory_space=pl.ANY),
                pl.BlockSpec(memory_space=pl.ANY),
            ],
            out_specs=pl.BlockSpec((blk_groups, SEG_ALIGN, d), lambda i, be, nu, nxt: (i, 0, 0)),
            scratch_shapes=[
                pltpu.VMEM((d, d_e), F32),
                pltpu.VMEM((d, d_e), F32),
                pltpu.VMEM((d_e, d), F32),
                pltpu.VMEM((d, d_e), BF16),
                pltpu.VMEM((d, d_e), BF16),
                pltpu.VMEM((d_e, d), BF16),
                pltpu.SemaphoreType.DMA((3,)),
            ],
        ),
        compiler_params=pltpu.CompilerParams(
            dimension_semantics=("arbitrary",), vmem_limit_bytes=VMEM_LIMIT),
        name="expert_ffn",
    )(tabs["block_e"], tabs["n_used"], tabs["next_e"], xp, w_gate, w_up, w_down)


def _combine_kernel(toff_ref, goff_ref, seg_ref, big_ref, x_ref, rn_ref, yp_hbm, lng_ref, lnb_ref,
                    o_ref, ybuf, sem):
    i = pl.program_id(0)
    n = pl.num_programs(0)
    tile = x_ref.shape[0]
    slot = i % 2

    def wait_segments(t, sl):
        _wait_segment_copies(
            lambda size: pltpu.make_async_copy(yp_hbm.at[pl.ds(0, size)], ybuf.at[sl, pl.ds(0, size)],
                                               sem.at[sl]).wait(),
            toff_ref, seg_ref, t)

    def start_segments(t, sl, live, inline):
        _start_segment_copies(
            lambda tg, sg, size: pltpu.make_async_copy(
                yp_hbm.at[pl.ds(sg, size)], ybuf.at[sl, pl.ds(tg, size)], sem.at[sl]).start(),
            toff_ref, goff_ref, seg_ref, big_ref, t, live, inline)

    @pl.when(i == 0)
    def _():
        ybuf[...] = jnp.zeros(ybuf.shape, ybuf.dtype)
        start_segments(0, 0, True, inline=False)

    wait_segments(i, slot)
    start_segments(jnp.minimum(i + 1, n - 1), 1 - slot, i + 1 < n, inline=True)

    rn = rn_ref[...]
    full = lambda col: jnp.broadcast_to(col, (tile, LANES))
    w1, w2 = full(rn[:, 2:3]), full(rn[:, 3:4])
    pos1, pos2 = full(rn[:, 4:5]), full(rn[:, 5:6])
    lane = lax.broadcasted_iota(I32, (tile, LANES), 1).astype(F32)

    def pick_lanes(col0):
        cols = lane + float(col0)
        return jnp.where(cols == pos1, w1, 0.0) + jnp.where(cols == pos2, w2, 0.0)

    ffn = jnp.zeros(o_ref.shape, F32)
    for c in range(SORT_ROWS // SORT_CHUNK):
        pick = jnp.concatenate([pick_lanes(c * SORT_CHUNK + l0) for l0 in range(0, SORT_CHUNK, LANES)],
                               axis=1).astype(BF16)
        g0, g1 = c * SORT_CHUNK // SEG_ALIGN, (c + 1) * SORT_CHUNK // SEG_ALIGN
        ffn = ffn + jnp.dot(pick, _ungrouped(ybuf[slot, g0:g1]), preferred_element_type=F32)
    o_ref[...] = _layer_norm(ALPHA * x_ref[...] + ffn, lng_ref[...], lnb_ref[...])


def _combine(xt, rn, yp, tabs, ln_g, ln_b):
    n_tok, d = xt.shape
    tile = MOE_TILE
    return pl.pallas_call(
        _combine_kernel,
        out_shape=jax.ShapeDtypeStruct((n_tok, d), F32),
        grid_spec=pltpu.PrefetchScalarGridSpec(
            num_scalar_prefetch=4,
            grid=(n_tok // tile,),
            in_specs=[
                pl.BlockSpec((tile, d), lambda i, *_: (i, 0)),
                pl.BlockSpec((tile, LANES), lambda i, *_: (i, 0)),
                pl.BlockSpec(memory_space=pl.ANY),
                pl.BlockSpec((1, d), lambda i, *_: (0, 0)),
                pl.BlockSpec((1, d), lambda i, *_: (0, 0)),
            ],
            out_specs=pl.BlockSpec((tile, d), lambda i, *_: (i, 0)),
            scratch_shapes=[
                pltpu.VMEM((2, SORT_ROWS // SEG_ALIGN, SEG_ALIGN, d), BF16),
                pltpu.SemaphoreType.DMA((2,)),
            ],
        ),
        compiler_params=pltpu.CompilerParams(
            dimension_semantics=("arbitrary",), vmem_limit_bytes=VMEM_LIMIT),
        name="combine",
    )(tabs["tile_off"], tabs["glob_off"], tabs["seg"], tabs["big"], xt, rn, yp,
      ln_g.reshape(1, d), ln_b.reshape(1, d))


def _moe_layer(x, rg_w, rg_b, re_w, re_b, w_gate, w_up, w_down, layer, ln_g, ln_b):
    bt, s_len, d = x.shape
    n_tok = bt * s_len
    xt = x.reshape(n_tok, d)
    rt, rn, cnt = _router(xt, rg_w, rg_b, re_w, re_b)
    tabs = _routing_tables(cnt, n_tok)
    xp = _dispatch(xt, rt, tabs)
    yp = _expert_ffn(xp, tabs, w_gate, w_up, w_down, layer)
    out = _combine(xt, rn, yp, tabs, ln_g, ln_b)
    return out.reshape(bt, s_len, d)


def kernel(x, ab_w_in, ab_b_in, a_dw, a_dw_b, a_ln_g, a_ln_b, b_dw, ab_w_out, cd_w_in, cd_b_in, c_ln_g, c_ln_b, c_ws, c_ws_b, d_rel_bias, cd_w_out, mix_ln_g, mix_ln_b, moe_rg_w, moe_rg_b, moe_re_w, moe_re_b, moe_w_gate, moe_w_up, moe_w_down, ffn_ln_g, ffn_ln_b):
    for layer in range(DEPTH):
        i = layer // 2
        if layer % 2 == 0:
            x = _ab_mixer(x, ab_w_in[i], ab_b_in[i], a_dw[i], a_dw_b[i], a_ln_g[i], a_ln_b[i],
                          b_dw[i], ab_w_out[i], mix_ln_g[layer], mix_ln_b[layer])
        else:
            x = _cd_mixer(x, cd_w_in[i], cd_b_in[i], c_ln_g[i], c_ln_b[i], c_ws[i], c_ws_b[i],
                          d_rel_bias[i], cd_w_out[i], mix_ln_g[layer], mix_ln_b[layer])
        x = _moe_layer(x, moe_rg_w[layer], moe_rg_b[layer], moe_re_w[layer], moe_re_b[layer],
                       moe_w_gate, moe_w_up, moe_w_down, layer,
                       ffn_ln_g[layer], ffn_ln_b[layer])
    return x
```

```python
import functools

import jax
import jax.numpy as jnp
from jax import lax
from jax.experimental import pallas as pl
from jax.experimental.pallas import tpu as pltpu

F32 = jnp.float32
BF16 = jnp.bfloat16
I32 = jnp.int32

DEPTH = 2
CHUNK = 64
LEFT_CHUNKS = 8
HIST = LEFT_CHUNKS * CHUNK
CONV_A = 31
CONV_B = 3
HEADS_C = 8
SGU_BLOCK = 128
HEADS_D = 8
MAX_REL = 256
N_GROUPS = 4
EXPERTS_PER_GROUP = 8
N_EXPERTS = N_GROUPS * EXPERTS_PER_GROUP
TOP_K = 2
ALPHA = (2 * DEPTH) ** 0.25
LN_EPS = 1e-5
NEG_INF = -1e30
LOG2_E = 1.4426950408889634

LANES = 128
SUBLANES = 8
VMEM_LIMIT = 56 * 1024 * 1024

SEQ_TILE = 512
CONV_ROWS = 256
HALO = 32
Q_ROWS = 256
K_ROWS = Q_ROWS + HIST
MOE_TILE = 512
ROUTER_TILE = MOE_TILE
EXPERT_ROWS = 512
SEG_ALIGN = 2 * SUBLANES
SORT_ROWS = MOE_TILE * TOP_K + N_EXPERTS * SEG_ALIGN
SORT_CHUNK = 256
INLINE_GROUPS = 8
ROUTER_LANE0 = N_GROUPS
ROUTER_ROWS = -(-(N_GROUPS + N_EXPERTS) // SUBLANES) * SUBLANES
_B_SHIFTS = sorted({(HALO - (CONV_B - 1) + j) % SUBLANES for j in range(CONV_B)} - {0})


def _layer_norm(x, g, b):
    mu = jnp.mean(x, axis=-1, keepdims=True)
    xc = x - mu
    var = jnp.mean(xc * xc, axis=-1, keepdims=True)
    return xc * lax.rsqrt(var + LN_EPS) * g + b


def _const_spec(shape):
    nd = len(shape)
    return pl.BlockSpec(shape, lambda *_: (0,) * nd, pipeline_mode=pl.Buffered(1))


def _ab_mixer_kernel(x_ref, win_ref, bin_ref, adw_ref, adwb_ref, alng_ref, alnb_ref,
                     bdw_ref, wout_ref, lng_ref, lnb_ref, o_ref,
                     abuf, cbuf, gbuf, mbuf, asht, csht, *, d_a, d_b):
    ts = x_ref.shape[0]

    @pl.when(pl.program_id(1) == 0)
    def _():
        abuf[0:HALO, :] = jnp.zeros((HALO, d_a), F32)
        cbuf[0:HALO, :] = jnp.zeros((HALO, d_b), F32)

    x = x_ref[...]
    xb = x.astype(BF16)
    n_sh = HALO + ts - SUBLANES

    def tap(buf, sht, slot_of, r0, off):
        base = r0 + (off // SUBLANES) * SUBLANES
        if off % SUBLANES == 0:
            return buf[base:base + CONV_ROWS, :]
        return sht[slot_of(off % SUBLANES), base:base + CONV_ROWS, :]

    h_a = jnp.dot(xb, win_ref[:, 0:2 * d_a], preferred_element_type=F32) + bin_ref[:, 0:2 * d_a]
    abuf[HALO:HALO + ts, :] = h_a[:, 0:d_a] * jax.nn.sigmoid(h_a[:, d_a:])
    for sh in range(1, SUBLANES):
        asht[sh, 0:n_sh, :] = abuf[sh:sh + n_sh, :]
    for r0 in range(0, ts, CONV_ROWS):
        acc = jnp.broadcast_to(adwb_ref[...], (CONV_ROWS, d_a))
        for j in range(CONV_A):
            acc = acc + adw_ref[j:j + 1, :] * tap(abuf, asht, lambda sh: sh, r0, HALO - (CONV_A - 1) + j)
        a = _layer_norm(acc, alng_ref[...], alnb_ref[...])
        mbuf[r0:r0 + CONV_ROWS, 0:d_a] = (a * jax.nn.sigmoid(a)).astype(BF16)

    h_b = jnp.dot(xb, win_ref[:, 2 * d_a:], preferred_element_type=F32) + bin_ref[:, 2 * d_a:]
    gbuf[...] = h_b[:, 0:d_b]
    cbuf[HALO:HALO + ts, :] = h_b[:, d_b:2 * d_b] * h_b[:, 2 * d_b:]
    for sh in _B_SHIFTS:
        csht[_B_SHIFTS.index(sh), 0:n_sh, :] = cbuf[sh:sh + n_sh, :]
    for r0 in range(0, ts, CONV_ROWS):
        s = jnp.zeros((CONV_ROWS, d_b), F32)
        for j in range(CONV_B):
            s = s + bdw_ref[j:j + 1, :] * tap(cbuf, csht, _B_SHIFTS.index, r0, HALO - (CONV_B - 1) + j)
        mbuf[r0:r0 + CONV_ROWS, d_a:d_a + d_b] = (gbuf[r0:r0 + CONV_ROWS, :] * s).astype(BF16)

    abuf[0:HALO, :] = abuf[ts:ts + HALO, :]
    cbuf[0:HALO, :] = cbuf[ts:ts + HALO, :]

    mix = jnp.dot(mbuf[...], wout_ref[...], preferred_element_type=F32)
    o_ref[...] = _layer_norm(ALPHA * x + mix, lng_ref[...], lnb_ref[...])


def _ab_mixer(x, w_in, b_in, a_dw, a_dw_b, a_ln_g, a_ln_b, b_dw, w_out, ln_g, ln_b):
    bt, s_len, d = x.shape
    d_a = a_dw.shape[1]
    d_b = b_dw.shape[1]
    w_in_w = w_in.shape[1]
    ts = SEQ_TILE
    row = lambda v: v.reshape(1, -1)
    kern = functools.partial(_ab_mixer_kernel, d_a=d_a, d_b=d_b)
    return pl.pallas_call(
        kern,
        out_shape=jax.ShapeDtypeStruct((bt, s_len, d), F32),
        grid=(bt, s_len // ts),
        in_specs=[
            pl.BlockSpec((None, ts, d), lambda b, t: (b, t, 0)),
            _const_spec((d, w_in_w)),
            _const_spec((1, w_in_w)),
            _const_spec((CONV_A, d_a)),
            _const_spec((1, d_a)),
            _const_spec((1, d_a)),
            _const_spec((1, d_a)),
            _const_spec((CONV_B, d_b)),
            _const_spec((d_a + d_b, d)),
            _const_spec((1, d)),
            _const_spec((1, d)),
        ],
        out_specs=pl.BlockSpec((None, ts, d), lambda b, t: (b, t, 0)),
        scratch_shapes=[
            pltpu.VMEM((HALO + ts, d_a), F32),
            pltpu.VMEM((HALO + ts, d_b), F32),
            pltpu.VMEM((ts, d_b), F32),
            pltpu.VMEM((ts, d_a + d_b), BF16),
            pltpu.VMEM((SUBLANES, HALO + ts, d_a), F32),
            pltpu.VMEM((len(_B_SHIFTS), HALO + ts, d_b), F32),
        ],
        compiler_params=pltpu.CompilerParams(
            dimension_semantics=("arbitrary", "arbitrary"), vmem_limit_bytes=VMEM_LIMIT),
        name="ab_mixer",
    )(x, w_in.astype(BF16), row(b_in), a_dw, row(a_dw_b), row(a_ln_g), row(a_ln_b),
      b_dw, w_out.astype(BF16), row(ln_g), row(ln_b))


def _cd_mixer_kernel(x_ref, win_ref, bin_ref, clng_ref, clnb_ref, wsc_ref, wsb_ref,
                     bias_ref, wout_ref, lng_ref, lnb_ref, o_ref,
                     qbuf, kbuf, vbuf, wsm, vbd, mbuf, *, d_c, d_d):
    ts = x_ref.shape[0]
    first_in_seq = pl.program_id(1) == 0
    hc = d_c // HEADS_C
    hd = d_d // HEADS_D

    @pl.when(first_in_seq)
    def _():
        kbuf[0:HIST, :] = jnp.zeros((HIST, d_d), BF16)
        vbuf[0:HIST, :] = jnp.zeros((HIST, d_d), BF16)
        r = lax.broadcasted_iota(I32, (SGU_BLOCK, HEADS_C * SGU_BLOCK), 0)
        c = lax.broadcasted_iota(I32, (SGU_BLOCK, HEADS_C * SGU_BLOCK), 1) % SGU_BLOCK
        wsm[...] = jnp.where(c // CHUNK <= r // CHUNK, wsc_ref[...], 0.0).astype(BF16)

    x = x_ref[...]
    xb = x.astype(BF16)
    h_att = (jnp.dot(xb, win_ref[:, 2 * d_c:], preferred_element_type=F32) + bin_ref[:, 2 * d_c:])
    qbuf[...] = (h_att[:, 0:d_d] * (hd ** -0.5 * LOG2_E)).astype(BF16)
    kbuf[HIST:HIST + ts, :] = h_att[:, d_d:2 * d_d].astype(BF16)
    vbuf[HIST:HIST + ts, :] = h_att[:, 2 * d_d:].astype(BF16)

    def attention(qb):
        q0 = qb * Q_ROWS
        col = lax.broadcasted_iota(I32, (1, K_ROWS), 1)
        no_key = jnp.where(jnp.logical_and(first_in_seq, col + q0 < HIST), NEG_INF, 0.0)
        for hh in range(HEADS_D):
            q = qbuf[q0:q0 + Q_ROWS, hh * hd:(hh + 1) * hd]
            k = kbuf[q0:q0 + K_ROWS, hh * hd:(hh + 1) * hd]
            v = vbuf[q0:q0 + K_ROWS, hh * hd:(hh + 1) * hd]
            s = lax.dot_general(q, k, (((1,), (1,)), ((), ())), preferred_element_type=F32)
            s = s + bias_ref[hh] + no_key
            p = jnp.exp2(s - jnp.max(s, axis=-1, keepdims=True))
            l = jnp.sum(p, axis=-1, keepdims=True)
            o = jnp.dot(p.astype(BF16), v, preferred_element_type=F32)
            mbuf[q0:q0 + Q_ROWS, d_c + hh * hd:d_c + (hh + 1) * hd] = (o / l).astype(BF16)

    for qb in range(ts // Q_ROWS):
        attention(qb)

    kbuf[0:HIST, :] = kbuf[ts:ts + HIST, :]
    vbuf[0:HIST, :] = vbuf[ts:ts + HIST, :]

    h_sgu = (jnp.dot(xb, win_ref[:, 0:2 * d_c], preferred_element_type=F32) + bin_ref[:, 0:2 * d_c])
    u = h_sgu[:, 0:d_c]
    vn = _layer_norm(h_sgu[:, d_c:], clng_ref[...], clnb_ref[...])
    lane_head = lax.broadcasted_iota(I32, (SGU_BLOCK, d_c), 1) // hc
    for nb in range(ts // SGU_BLOCK):
        v_blk = vn[nb * SGU_BLOCK:(nb + 1) * SGU_BLOCK, :]
        for hh in range(HEADS_C):
            vbd[hh * SGU_BLOCK:(hh + 1) * SGU_BLOCK, :] = jnp.where(
                lane_head == hh, v_blk, 0.0).astype(BF16)
        gate = jnp.dot(wsm[...], vbd[...], preferred_element_type=F32) + wsb_ref[...]
        mbuf[nb * SGU_BLOCK:(nb + 1) * SGU_BLOCK, 0:d_c] = (
            u[nb * SGU_BLOCK:(nb + 1) * SGU_BLOCK, :] * gate).astype(BF16)

    mix = jnp.dot(mbuf[...], wout_ref[...], preferred_element_type=F32)
    o_ref[...] = _layer_norm(ALPHA * x + mix, lng_ref[...], lnb_ref[...])


def _attention_bias(d_rel_bias):
    heads = d_rel_bias.shape[0]
    i = jnp.arange(Q_ROWS)[:, None]
    j = jnp.arange(K_ROWS)[None, :]
    jb = j - (i // CHUNK) * CHUNK
    in_band = jnp.logical_and(jb >= 0, jb < (LEFT_CHUNKS + 1) * CHUNK)
    n_diag = Q_ROWS + K_ROWS - 1
    m = jnp.arange(n_diag)
    diag = d_rel_bias[:, jnp.clip(Q_ROWS - 1 + HIST - m, -MAX_REL, MAX_REL) + MAX_REL].astype(F32)
    period = jnp.pad(diag, ((0, 0), (0, 1)))
    flat = jnp.tile(period, (1, Q_ROWS))[:, :Q_ROWS * n_diag]
    rel = flat.reshape(heads, Q_ROWS, n_diag)[:, :, Q_ROWS - 1:Q_ROWS - 1 + K_ROWS]
    return jnp.where(in_band[None], rel * LOG2_E, NEG_INF)


def _cd_mixer(x, w_in, b_in, c_ln_g, c_ln_b, c_ws, c_ws_b, d_rel_bias, w_out, ln_g, ln_b):
    bt, s_len, d = x.shape
    d_c = c_ln_g.shape[0]
    d_d = w_out.shape[0] - d_c
    w_in_w = w_in.shape[1]
    ts = SEQ_TILE
    assert ts == HIST and ts % Q_ROWS == 0 and ts % SGU_BLOCK == 0
    row = lambda v: v.reshape(1, -1)
    ws_cat = jnp.transpose(c_ws, (1, 0, 2)).reshape(SGU_BLOCK, HEADS_C * SGU_BLOCK)
    bias_full = jnp.repeat(c_ws_b.T, d_c // HEADS_C, axis=1)
    att_bias = _attention_bias(d_rel_bias)
    kern = functools.partial(_cd_mixer_kernel, d_c=d_c, d_d=d_d)
    return pl.pallas_call(
        kern,
        out_shape=jax.ShapeDtypeStruct((bt, s_len, d), F32),
        grid=(bt, s_len // ts),
        in_specs=[
            pl.BlockSpec((None, ts, d), lambda b, t: (b, t, 0)),
            _const_spec((d, w_in_w)),
            _const_spec((1, w_in_w)),
            _const_spec((1, d_c)),
            _const_spec((1, d_c)),
            _const_spec((SGU_BLOCK, HEADS_C * SGU_BLOCK)),
            _const_spec((SGU_BLOCK, d_c)),
            _const_spec((HEADS_D, Q_ROWS, K_ROWS)),
            _const_spec((d_c + d_d, d)),
            _const_spec((1, d)),
            _const_spec((1, d)),
        ],
        out_specs=pl.BlockSpec((None, ts, d), lambda b, t: (b, t, 0)),
        scratch_shapes=[
            pltpu.VMEM((ts, d_d), BF16),
            pltpu.VMEM((HIST + ts, d_d), BF16),
            pltpu.VMEM((HIST + ts, d_d), BF16),
            pltpu.VMEM((SGU_BLOCK, HEADS_C * SGU_BLOCK), BF16),
            pltpu.VMEM((HEADS_C * SGU_BLOCK, d_c), BF16),
            pltpu.VMEM((ts, d_c + d_d), BF16),
        ],
        compiler_params=pltpu.CompilerParams(
            dimension_semantics=("arbitrary", "arbitrary"), vmem_limit_bytes=VMEM_LIMIT),
        name="cd_mixer",
    )(x, w_in.astype(BF16), row(b_in), row(c_ln_g), row(c_ln_b), ws_cat, bias_full,
      att_bias, w_out.astype(BF16), row(ln_g), row(ln_b))


def _router_kernel(x_ref, wr_ref, bt_ref, rt_ref, rn_ref, cnt_ref, after_ref, below_ref):
    tr = x_ref.shape[0]
    nr = ROUTER_ROWS

    @pl.when(pl.program_id(0) == 0)
    def _():
        rr = lax.broadcasted_iota(I32, (tr, tr), 0)
        cc = lax.broadcasted_iota(I32, (tr, tr), 1)
        after_ref[...] = jnp.where(rr < cc, 1.0, 0.0).astype(BF16)
        lr = lax.broadcasted_iota(I32, (LANES, LANES), 0)
        lc = lax.broadcasted_iota(I32, (LANES, LANES), 1)
        below_ref[...] = jnp.where(lc < lr, 1.0, 0.0).astype(BF16)

    x = x_ref[...]
    x_hi = x.astype(BF16)
    x_lo = (x - x_hi.astype(F32)).astype(BF16)
    p_hi = jnp.dot(x_hi, wr_ref[...], preferred_element_type=F32)
    p_lo = jnp.dot(x_lo, wr_ref[...], preferred_element_type=F32)
    logits = (p_hi + pltpu.roll(p_hi, LANES // 2, axis=1) + p_lo).T[0:nr] + bt_ref[...]
    row = lax.broadcasted_iota(I32, (nr, tr), 0).astype(F32)

    def top1(vals):
        m = jnp.max(vals, axis=0, keepdims=True)
        idx = jnp.min(jnp.where(vals == m, row, float(nr)), axis=0, keepdims=True)
        return m, idx

    g_mask = row < N_GROUPS
    g_max, g_idx = top1(jnp.where(g_mask, logits, NEG_INF))
    g_top = 1.0 / jnp.sum(jnp.where(g_mask, jnp.exp(logits - g_max), 0.0), axis=0, keepdims=True)

    e_lo = ROUTER_LANE0 + EXPERTS_PER_GROUP * g_idx
    e_vals = jnp.where(jnp.logical_and(row >= e_lo, row < e_lo + EXPERTS_PER_GROUP), logits, NEG_INF)
    m1, i1 = top1(e_vals)
    m2, i2 = top1(jnp.where(row == i1, NEG_INF, e_vals))
    ratio = jnp.exp(m2 - m1)
    w1 = g_top / (1.0 + ratio)
    w2 = g_top * ratio / (1.0 + ratio)

    chosen = jnp.logical_or(row == i1, row == i2)
    onehot = jnp.where(chosen, 1.0, 0.0)
    seen = jnp.dot(onehot.astype(BF16), after_ref[...], preferred_element_type=F32)
    counts = jnp.broadcast_to(jnp.sum(onehot, axis=1, keepdims=True), (nr, LANES))
    cnt_ref[...] = counts

    groups = ((counts.astype(I32) + (SEG_ALIGN - 1)) // SEG_ALIGN).astype(F32).astype(BF16)
    groups = jnp.concatenate([groups, jnp.zeros((LANES - nr, LANES), BF16)], axis=0)
    seg_start = SEG_ALIGN * jnp.dot(below_ref[...], groups, preferred_element_type=F32)[0:nr]
    where_row = seen + jnp.concatenate([seg_start] * (tr // LANES), axis=1)
    pos1 = jnp.sum(jnp.where(row == i1, where_row, 0.0), axis=0, keepdims=True)
    pos2 = jnp.sum(jnp.where(row == i2, where_row, 0.0), axis=0, keepdims=True)

    fields = [i1 - ROUTER_LANE0, i2 - ROUTER_LANE0, w1, w2, pos1, pos2]
    res = jnp.concatenate(fields + [jnp.zeros((SUBLANES - len(fields), tr), F32)], axis=0)
    rt_ref[...] = res
    rn_ref[...] = jnp.concatenate([res, jnp.zeros((LANES - SUBLANES, tr), F32)], axis=0).T


def _router(xt, rg_w, rg_b, re_w, re_b):
    n_tok, d = xt.shape
    tr = ROUTER_TILE
    w_all = jnp.concatenate([rg_w, jnp.transpose(re_w, (1, 0, 2)).reshape(d, N_EXPERTS)], axis=1)
    b_all = jnp.concatenate([rg_b, re_b.reshape(N_EXPERTS)])
    half = LANES // 2
    pad = half - w_all.shape[1]
    w_hi = w_all.astype(BF16)
    w_lo = (w_all - w_hi.astype(F32)).astype(BF16)
    w_all = jnp.concatenate([jnp.pad(w_hi, ((0, 0), (0, pad))), jnp.pad(w_lo, ((0, 0), (0, pad)))], axis=1)
    b_t = jnp.broadcast_to(jnp.pad(b_all, (0, ROUTER_ROWS - b_all.shape[0]))[:, None], (ROUTER_ROWS, tr))
    return pl.pallas_call(
        _router_kernel,
        out_shape=(
            jax.ShapeDtypeStruct((SUBLANES, n_tok), F32),
            jax.ShapeDtypeStruct((n_tok, LANES), F32),
            jax.ShapeDtypeStruct((n_tok // tr, ROUTER_ROWS, LANES), F32),
        ),
        grid=(n_tok // tr,),
        in_specs=[
            pl.BlockSpec((tr, d), lambda i: (i, 0)),
            _const_spec((d, LANES)),
            _const_spec((ROUTER_ROWS, tr)),
        ],
        out_specs=(
            pl.BlockSpec((SUBLANES, tr), lambda i: (0, i)),
            pl.BlockSpec((tr, LANES), lambda i: (i, 0)),
            pl.BlockSpec((None, ROUTER_ROWS, LANES), lambda i: (i, 0, 0)),
        ),
        scratch_shapes=[pltpu.VMEM((tr, tr), BF16), pltpu.VMEM((LANES, LANES), BF16)],
        compiler_params=pltpu.CompilerParams(
            dimension_semantics=("arbitrary",), vmem_limit_bytes=VMEM_LIMIT),
        name="router",
    )(xt, w_all, b_t)


def _round_up(v, m):
    return ((v + m - 1) // m) * m


def _num_expert_blocks(n_tok):
    n_tiles = n_tok // MOE_TILE
    rows = n_tok * TOP_K + n_tiles * N_EXPERTS * (SEG_ALIGN - 1)
    return -(-rows // EXPERT_ROWS) + N_EXPERTS


def _routing_tables(cnt, n_tok):
    counts = cnt[:, ROUTER_LANE0:ROUTER_LANE0 + N_EXPERTS, 0].astype(I32)
    seg = _round_up(counts, SEG_ALIGN)
    tile_off = jnp.cumsum(seg, axis=1) - seg
    tot = jnp.sum(seg, axis=0)
    reg = _round_up(tot, EXPERT_ROWS)
    reg_end = jnp.cumsum(reg)
    reg_start = reg_end - reg
    glob_off = reg_start[None, :] + jnp.cumsum(seg, axis=0) - seg
    n_blocks = _num_expert_blocks(n_tok)
    blk_start = jnp.arange(n_blocks, dtype=I32) * EXPERT_ROWS
    block_e = jnp.sum((reg_end[None, :] <= blk_start[:, None]).astype(I32), axis=1)
    block_e = jnp.minimum(block_e, N_EXPERTS - 1).astype(I32)
    n_used = (reg_end[-1] // EXPERT_ROWS).astype(I32).reshape(1)
    groups = lambda a: (a.reshape(-1) // SEG_ALIGN).astype(I32)
    big = (jnp.max(seg, axis=1) >= INLINE_GROUPS * SEG_ALIGN).astype(I32)
    ar = jnp.arange(N_EXPERTS, dtype=I32)
    later = jnp.where(jnp.logical_and(ar[None, :] > ar[:, None], reg[None, :] > 0), ar[None, :], N_EXPERTS)
    next_of = jnp.min(later, axis=1)
    next_of = jnp.where(next_of < N_EXPERTS, next_of, ar)
    next_e = jnp.sum(jnp.where(block_e[:, None] == ar[None, :], next_of[None, :], 0), axis=1).astype(I32)
    return dict(tile_off=groups(tile_off), glob_off=groups(glob_off), seg=groups(seg), big=big,
                tail_start=groups(reg_start + tot), tail_len=groups(reg - tot),
                block_e=block_e, n_used=n_used, next_e=next_e)


def _grouped(rows):
    return rows.reshape(rows.shape[0] // SEG_ALIGN, SEG_ALIGN, rows.shape[1])


def _ungrouped(groups):
    return groups.reshape(groups.shape[0] * SEG_ALIGN, groups.shape[2])


def _for_each_piece(length, max_len, fn):
    off = 0
    for size in reversed([1 << b for b in range(max_len.bit_length())]):
        take = (length & size) != 0

        @pl.when(take)
        def _(off=off, size=size):
            fn(off, size)

        off = off + jnp.where(take, size, 0)


def _start_segment_copies(copy, toff_ref, goff_ref, seg_ref, big_ref, t, live, inline):
    def per_expert(e, limit, enabled):
        idx = t * N_EXPERTS + e
        t0 = toff_ref[idx]
        g0 = goff_ref[idx]
        _for_each_piece(jnp.where(enabled, seg_ref[idx], 0), limit,
                        lambda off, size: copy(t0 + off, g0 + off, size))

    def looped(enabled):
        def body(e, c):
            per_expert(e, MOE_TILE // SEG_ALIGN, enabled)
            return c

        lax.fori_loop(0, N_EXPERTS, body, 0)

    if not inline:
        looped(live)
        return
    big = big_ref[t] != 0
    for e in range(N_EXPERTS):
        per_expert(e, INLINE_GROUPS - 1, jnp.logical_and(live, jnp.logical_not(big)))

    @pl.when(jnp.logical_and(live, big))
    def _():
        looped(True)


def _wait_segment_copies(wait, toff_ref, seg_ref, t):
    last = t * N_EXPERTS + N_EXPERTS - 1
    _for_each_piece(toff_ref[last] + seg_ref[last], SORT_ROWS // SEG_ALIGN, lambda off, size: wait(size))


def _dispatch_kernel(toff_ref, goff_ref, seg_ref, big_ref, tstart_ref, tlen_ref, nu_ref,
                     x_ref, rt_ref, xp_hbm, obuf, zbuf, sem, zsem):
    i = pl.program_id(0)
    n = pl.num_programs(0)
    tile = x_ref.shape[0]
    slot = i % 2

    def wait_segments(t, sl):
        _wait_segment_copies(
            lambda size: pltpu.make_async_copy(obuf.at[sl, pl.ds(0, size)], xp_hbm.at[pl.ds(0, size)],
                                               sem.at[sl]).wait(),
            toff_ref, seg_ref, t)

    def start_segments(t, sl, live, inline):
        _start_segment_copies(
            lambda tg, sg, size: pltpu.make_async_copy(
                obuf.at[sl, pl.ds(tg, size)], xp_hbm.at[pl.ds(sg, size)], sem.at[sl]).start(),
            toff_ref, goff_ref, seg_ref, big_ref, t, live, inline)

    @pl.when(i >= 2)
    def _():
        wait_segments(i - 2, slot)

    start_segments(jnp.maximum(i - 1, 0), 1 - slot, i >= 1, inline=True)

    pos1 = rt_ref[4:5, :].astype(I32)
    pos2 = rt_ref[5:6, :].astype(I32)
    xb = x_ref[...].astype(BF16)
    for c in range(SORT_ROWS // SORT_CHUNK):
        rows = lax.broadcasted_iota(I32, (SORT_CHUNK, tile), 0) + c * SORT_CHUNK
        hit = jnp.logical_or(rows == pos1, rows == pos2)
        g0, g1 = c * SORT_CHUNK // SEG_ALIGN, (c + 1) * SORT_CHUNK // SEG_ALIGN
        obuf[slot, g0:g1] = _grouped(jnp.dot(
            jnp.where(hit, 1.0, 0.0).astype(BF16), xb, preferred_element_type=F32).astype(BF16))

    def zero_fill(wait):
        def per_expert(e, c):
            def piece(off, size):
                cp = pltpu.make_async_copy(zbuf.at[pl.ds(0, size)],
                                           xp_hbm.at[pl.ds(tstart_ref[e] + off, size)], zsem)
                cp.wait() if wait else cp.start()

            _for_each_piece(tlen_ref[e], EXPERT_ROWS // SEG_ALIGN - 1, piece)
            return c

        lax.fori_loop(0, N_EXPERTS, per_expert, 0)

        zg = zbuf.shape[0]

        def per_zero_copy(h, c):
            cp = pltpu.make_async_copy(zbuf, xp_hbm.at[pl.ds(h * zg, zg)], zsem)
            cp.wait() if wait else cp.start()
            return c

        per_block = EXPERT_ROWS // SEG_ALIGN // zg
        lax.fori_loop(nu_ref[0] * per_block, xp_hbm.shape[0] // zg, per_zero_copy, 0)

    @pl.when(i == 0)
    def _():
        zbuf[...] = jnp.zeros(zbuf.shape, zbuf.dtype)
        zero_fill(False)

    @pl.when(i == n - 1)
    def _():
        start_segments(i, slot, True, inline=False)

        @pl.when(i >= 1)
        def _():
            wait_segments(i - 1, 1 - slot)

        wait_segments(i, slot)
        zero_fill(True)


def _dispatch(xt, rt, tabs):
    n_tok, d = xt.shape
    tile = MOE_TILE
    n_groups = _num_expert_blocks(n_tok) * EXPERT_ROWS // SEG_ALIGN
    return pl.pallas_call(
        _dispatch_kernel,
        out_shape=jax.ShapeDtypeStruct((n_groups, SEG_ALIGN, d), BF16),
        grid_spec=pltpu.PrefetchScalarGridSpec(
            num_scalar_prefetch=7,
            grid=(n_tok // tile,),
            in_specs=[
                pl.BlockSpec((tile, d), lambda i, *_: (i, 0)),
                pl.BlockSpec((SUBLANES, tile), lambda i, *_: (0, i)),
            ],
            out_specs=pl.BlockSpec(memory_space=pl.ANY),
            scratch_shapes=[
                pltpu.VMEM((2, SORT_ROWS // SEG_ALIGN, SEG_ALIGN, d), BF16),
                pltpu.VMEM((EXPERT_ROWS // SEG_ALIGN // 2, SEG_ALIGN, d), BF16),
                pltpu.SemaphoreType.DMA((2,)),
                pltpu.SemaphoreType.DMA,
            ],
        ),
        compiler_params=pltpu.CompilerParams(
            dimension_semantics=("arbitrary",), vmem_limit_bytes=VMEM_LIMIT),
        name="dispatch",
    )(tabs["tile_off"], tabs["glob_off"], tabs["seg"], tabs["big"], tabs["tail_start"],
      tabs["tail_len"], tabs["n_used"], xt, rt)


def _ffn_kernel(be_ref, nu_ref, nxt_ref, x_ref, wg_hbm, wu_hbm, wd_hbm, o_ref,
                wg_f, wu_f, wd_f, wg_b, wu_b, wd_b, sem, *, layer):
    i = pl.program_id(0)
    used = i < nu_ref[0]
    e = be_ref[i]

    def weight_copies(expert):
        return [pltpu.make_async_copy(src.at[layer, expert], dst, sem.at[k])
                for k, (src, dst) in enumerate(((wg_hbm, wg_f), (wu_hbm, wu_f), (wd_hbm, wd_f)))]

    @pl.when(i == 0)
    def _():
        for cp in weight_copies(e):
            cp.start()

    @pl.when(jnp.logical_and(used, jnp.logical_or(i == 0, e != be_ref[jnp.maximum(i - 1, 0)])))
    def _():
        for cp in weight_copies(e):
            cp.wait()
        wg_b[...] = wg_f[...].astype(BF16)
        wu_b[...] = wu_f[...].astype(BF16)
        wd_b[...] = wd_f[...].astype(BF16)

        @pl.when(nxt_ref[i] != e)
        def _():
            for cp in weight_copies(nxt_ref[i]):
                cp.start()

    @pl.when(used)
    def _():
        xb = _ungrouped(x_ref[...])
        g = jnp.dot(xb, wg_b[...], preferred_element_type=F32)
        u = jnp.dot(xb, wu_b[...], preferred_element_type=F32)
        hb = (g * jax.nn.sigmoid(g) * u).astype(BF16)
        o_ref[...] = _grouped(jnp.dot(hb, wd_b[...], preferred_element_type=F32).astype(BF16))

    @pl.when(jnp.logical_not(used))
    def _():
        o_ref[...] = jnp.zeros(o_ref.shape, o_ref.dtype)


def _expert_ffn(xp, tabs, w_gate, w_up, w_down, layer):
    n_groups, _, d = xp.shape
    d_e = w_gate.shape[3]
    blk_groups = EXPERT_ROWS // SEG_ALIGN
    n_blocks = n_groups // blk_groups
    blk = lambda i, be, nu, nxt: (jnp.minimum(i, nu[0] - 1), 0, 0)
    return pl.pallas_call(
        functools.partial(_ffn_kernel, layer=layer),
        out_shape=jax.ShapeDtypeStruct((n_groups, SEG_ALIGN, d), BF16),
        grid_spec=pltpu.PrefetchScalarGridSpec(
            num_scalar_prefetch=3,
            grid=(n_blocks,),
            in_specs=[
                pl.BlockSpec((blk_groups, SEG_ALIGN, d), blk),
                pl.BlockSpec(memory_space=pl.ANY),
                pl.BlockSpec(memory_space=pl.ANY),
                pl.BlockSpec(memory_space=pl.ANY),
            ],
            out_specs=pl.BlockSpec((blk_groups, SEG_ALIGN, d), lambda i, be, nu, nxt: (i, 0, 0)),
            scratch_shapes=[
                pltpu.VMEM((d, d_e), F32),
                pltpu.VMEM((d, d_e), F32),
                pltpu.VMEM((d_e, d), F32),
                pltpu.VMEM((d, d_e), BF16),
                pltpu.VMEM((d, d_e), BF16),
                pltpu.VMEM((d_e, d), BF16),
                pltpu.SemaphoreType.DMA((3,)),
            ],
        ),
        compiler_params=pltpu.CompilerParams(
            dimension_semantics=("arbitrary",), vmem_limit_bytes=VMEM_LIMIT),
        name="expert_ffn",
    )(tabs["block_e"], tabs["n_used"], tabs["next_e"], xp, w_gate, w_up, w_down)


def _combine_kernel(toff_ref, goff_ref, seg_ref, big_ref, x_ref, rn_ref, yp_hbm, lng_ref, lnb_ref,
                    o_ref, ybuf, sem):
    i = pl.program_id(0)
    n = pl.num_programs(0)
    tile = x_ref.shape[0]
    slot = i % 2

    def wait_segments(t, sl):
        _wait_segment_copies(
            lambda size: pltpu.make_async_copy(yp_hbm.at[pl.ds(0, size)], ybuf.at[sl, pl.ds(0, size)],
                                               sem.at[sl]).wait(),
            toff_ref, seg_ref, t)

    def start_segments(t, sl, live, inline):
        _start_segment_copies(
            lambda tg, sg, size: pltpu.make_async_copy(
                yp_hbm.at[pl.ds(sg, size)], ybuf.at[sl, pl.ds(tg, size)], sem.at[sl]).start(),
            toff_ref, goff_ref, seg_ref, big_ref, t, live, inline)

    @pl.when(i == 0)
    def _():
        ybuf[...] = jnp.zeros(ybuf.shape, ybuf.dtype)
        start_segments(0, 0, True, inline=False)

    wait_segments(i, slot)
    start_segments(jnp.minimum(i + 1, n - 1), 1 - slot, i + 1 < n, inline=True)

    rn = rn_ref[...]
    full = lambda col: jnp.broadcast_to(col, (tile, LANES))
    w1, w2 = full(rn[:, 2:3]), full(rn[:, 3:4])
    pos1, pos2 = full(rn[:, 4:5]), full(rn[:, 5:6])
    lane = lax.broadcasted_iota(I32, (tile, LANES), 1).astype(F32)

    def pick_lanes(col0):
        cols = lane + float(col0)
        return jnp.where(cols == pos1, w1, 0.0) + jnp.where(cols == pos2, w2, 0.0)

    ffn = jnp.zeros(o_ref.shape, F32)
    for c in range(SORT_ROWS // SORT_CHUNK):
        pick = jnp.concatenate([pick_lanes(c * SORT_CHUNK + l0) for l0 in range(0, SORT_CHUNK, LANES)],
                               axis=1).astype(BF16)
        g0, g1 = c * SORT_CHUNK // SEG_ALIGN, (c + 1) * SORT_CHUNK // SEG_ALIGN
        ffn = ffn + jnp.dot(pick, _ungrouped(ybuf[slot, g0:g1]), preferred_element_type=F32)
    o_ref[...] = _layer_norm(ALPHA * x_ref[...] + ffn, lng_ref[...], lnb_ref[...])


def _combine(xt, rn, yp, tabs, ln_g, ln_b):
    n_tok, d = xt.shape
    tile = MOE_TILE
    return pl.pallas_call(
        _combine_kernel,
        out_shape=jax.ShapeDtypeStruct((n_tok, d), F32),
        grid_spec=pltpu.PrefetchScalarGridSpec(
            num_scalar_prefetch=4,
            grid=(n_tok // tile,),
            in_specs=[
                pl.BlockSpec((tile, d), lambda i, *_: (i, 0)),
                pl.BlockSpec((tile, LANES), lambda i, *_: (i, 0)),
                pl.BlockSpec(memory_space=pl.ANY),
                pl.BlockSpec((1, d), lambda i, *_: (0, 0)),
                pl.BlockSpec((1, d), lambda i, *_: (0, 0)),
            ],
            out_specs=pl.BlockSpec((tile, d), lambda i, *_: (i, 0)),
            scratch_shapes=[
                pltpu.VMEM((2, SORT_ROWS // SEG_ALIGN, SEG_ALIGN, d), BF16),
                pltpu.SemaphoreType.DMA((2,)),
            ],
        ),
        compiler_params=pltpu.CompilerParams(
            dimension_semantics=("arbitrary",), vmem_limit_bytes=VMEM_LIMIT),
        name="combine",
    )(tabs["tile_off"], tabs["glob_off"], tabs["seg"], tabs["big"], xt, rn, yp,
      ln_g.reshape(1, d), ln_b.reshape(1, d))


def _moe_layer(x, rg_w, rg_b, re_w, re_b, w_gate, w_up, w_down, layer, ln_g, ln_b):
    bt, s_len, d = x.shape
    n_tok = bt * s_len
    xt = x.reshape(n_tok, d)
    rt, rn, cnt = _router(xt, rg_w, rg_b, re_w, re_b)
    tabs = _routing_tables(cnt, n_tok)
    xp = _dispatch(xt, rt, tabs)
    yp = _expert_ffn(xp, tabs, w_gate, w_up, w_down, layer)
    out = _combine(xt, rn, yp, tabs, ln_g, ln_b)
    return out.reshape(bt, s_len, d)


def kernel(x, ab_w_in, ab_b_in, a_dw, a_dw_b, a_ln_g, a_ln_b, b_dw, ab_w_out, cd_w_in, cd_b_in, c_ln_g, c_ln_b, c_ws, c_ws_b, d_rel_bias, cd_w_out, mix_ln_g, mix_ln_b, moe_rg_w, moe_rg_b, moe_re_w, moe_re_b, moe_w_gate, moe_w_up, moe_w_down, ffn_ln_g, ffn_ln_b):
    for layer in range(DEPTH):
        i = layer // 2
        if layer % 2 == 0:
            x = _ab_mixer(x, ab_w_in[i], ab_b_in[i], a_dw[i], a_dw_b[i], a_ln_g[i], a_ln_b[i],
                          b_dw[i], ab_w_out[i], mix_ln_g[layer], mix_ln_b[layer])
        else:
            x = _cd_mixer(x, cd_w_in[i], cd_b_in[i], c_ln_g[i], c_ln_b[i], c_ws[i], c_ws_b[i],
                          d_rel_bias[i], cd_w_out[i], mix_ln_g[layer], mix_ln_b[layer])
        x = _moe_layer(x, moe_rg_w[layer], moe_rg_b[layer], moe_re_w[layer], moe_re_b[layer],
                       moe_w_gate, moe_w_up, moe_w_down, layer,
                       ffn_ln_g[layer], ffn_ln_b[layer])
    return x
```

```python
import functools

import jax
import jax.numpy as jnp
from jax import lax
from jax.experimental import pallas as pl
from jax.experimental.pallas import tpu as pltpu

F32 = jnp.float32
BF16 = jnp.bfloat16
I32 = jnp.int32

DEPTH = 2
CHUNK = 64
LEFT_CHUNKS = 8
HIST = LEFT_CHUNKS * CHUNK
CONV_A = 31
CONV_B = 3
HEADS_C = 8
SGU_BLOCK = 128
HEADS_D = 8
MAX_REL = 256
N_GROUPS = 4
EXPERTS_PER_GROUP = 8
N_EXPERTS = N_GROUPS * EXPERTS_PER_GROUP
TOP_K = 2
ALPHA = (2 * DEPTH) ** 0.25
LN_EPS = 1e-5
NEG_INF = -1e30
LOG2_E = 1.4426950408889634

LANES = 128
SUBLANES = 8
VMEM_LIMIT = 56 * 1024 * 1024

SEQ_TILE = 512
CONV_ROWS = 256
HALO = 32
Q_ROWS = 256
K_ROWS = Q_ROWS + HIST
MOE_TILE = 512
ROUTER_TILE = MOE_TILE
EXPERT_ROWS = 512
SEG_ALIGN = 2 * SUBLANES
SORT_ROWS = MOE_TILE * TOP_K + N_EXPERTS * SEG_ALIGN
SORT_CHUNK = 256
INLINE_GROUPS = 8
ROUTER_LANE0 = N_GROUPS
ROUTER_ROWS = -(-(N_GROUPS + N_EXPERTS) // SUBLANES) * SUBLANES
_B_SHIFTS = sorted({(HALO - (CONV_B - 1) + j) % SUBLANES for j in range(CONV_B)} - {0})


def _layer_norm(x, g, b):
    mu = jnp.mean(x, axis=-1, keepdims=True)
    xc = x - mu
    var = jnp.mean(xc * xc, axis=-1, keepdims=True)
    return xc * lax.rsqrt(var + LN_EPS) * g + b


def _const_spec(shape):
    nd = len(shape)
    return pl.BlockSpec(shape, lambda *_: (0,) * nd, pipeline_mode=pl.Buffered(1))


def _ab_mixer_kernel(x_ref, win_ref, bin_ref, adw_ref, adwb_ref, alng_ref, alnb_ref,
                     bdw_ref, wout_ref, lng_ref, lnb_ref, o_ref,
                     abuf, cbuf, gbuf, mbuf, asht, csht, *, d_a, d_b):
    ts = x_ref.shape[0]

    @pl.when(pl.program_id(1) == 0)
    def _():
        abuf[0:HALO, :] = jnp.zeros((HALO, d_a), F32)
        cbuf[0:HALO, :] = jnp.zeros((HALO, d_b), F32)

    x = x_ref[...]
    xb = x.astype(BF16)
    n_sh = HALO + ts - SUBLANES

    def tap(buf, sht, slot_of, r0, off):
        base = r0 + (off // SUBLANES) * SUBLANES
        if off % SUBLANES == 0:
            return buf[base:base + CONV_ROWS, :]
        return sht[slot_of(off % SUBLANES), base:base + CONV_ROWS, :]

    h_a = jnp.dot(xb, win_ref[:, 0:2 * d_a], preferred_element_type=F32) + bin_ref[:, 0:2 * d_a]
    abuf[HALO:HALO + ts, :] = h_a[:, 0:d_a] * jax.nn.sigmoid(h_a[:, d_a:])
    for sh in range(1, SUBLANES):
        asht[sh, 0:n_sh, :] = abuf[sh:sh + n_sh, :]
    for r0 in range(0, ts, CONV_ROWS):
        acc = jnp.broadcast_to(adwb_ref[...], (CONV_ROWS, d_a))
        for j in range(CONV_A):
            acc = acc + adw_ref[j:j + 1, :] * tap(abuf, asht, lambda sh: sh, r0, HALO - (CONV_A - 1) + j)
        a = _layer_norm(acc, alng_ref[...], alnb_ref[...])
        mbuf[r0:r0 + CONV_ROWS, 0:d_a] = (a * jax.nn.sigmoid(a)).astype(BF16)

    h_b = jnp.dot(xb, win_ref[:, 2 * d_a:], preferred_element_type=F32) + bin_ref[:, 2 * d_a:]
    gbuf[...] = h_b[:, 0:d_b]
    cbuf[HALO:HALO + ts, :] = h_b[:, d_b:2 * d_b] * h_b[:, 2 * d_b:]
    for sh in _B_SHIFTS:
        csht[_B_SHIFTS.index(sh), 0:n_sh, :] = cbuf[sh:sh + n_sh, :]
    for r0 in range(0, ts, CONV_ROWS):
        s = jnp.zeros((CONV_ROWS, d_b), F32)
        for j in range(CONV_B):
            s = s + bdw_ref[j:j + 1, :] * tap(cbuf, csht, _B_SHIFTS.index, r0, HALO - (CONV_B - 1) + j)
        mbuf[r0:r0 + CONV_ROWS, d_a:d_a + d_b] = (gbuf[r0:r0 + CONV_ROWS, :] * s).astype(BF16)

    abuf[0:HALO, :] = abuf[ts:ts + HALO, :]
    cbuf[0:HALO, :] = cbuf[ts:ts + HALO, :]

    mix = jnp.dot(mbuf[...], wout_ref[...], preferred_element_type=F32)
    o_ref[...] = _layer_norm(ALPHA * x + mix, lng_ref[...], lnb_ref[...])


def _ab_mixer(x, w_in, b_in, a_dw, a_dw_b, a_ln_g, a_ln_b, b_dw, w_out, ln_g, ln_b):
    bt, s_len, d = x.shape
    d_a = a_dw.shape[1]
    d_b = b_dw.shape[1]
    w_in_w = w_in.shape[1]
    ts = SEQ_TILE
    row = lambda v: v.reshape(1, -1)
    kern = functools.partial(_ab_mixer_kernel, d_a=d_a, d_b=d_b)
    return pl.pallas_call(
        kern,
        out_shape=jax.ShapeDtypeStruct((bt, s_len, d), F32),
        grid=(bt, s_len // ts),
        in_specs=[
            pl.BlockSpec((None, ts, d), lambda b, t: (b, t, 0)),
            _const_spec((d, w_in_w)),
            _const_spec((1, w_in_w)),
            _const_spec((CONV_A, d_a)),
            _const_spec((1, d_a)),
            _const_spec((1, d_a)),
            _const_spec((1, d_a)),
            _const_spec((CONV_B, d_b)),
            _const_spec((d_a + d_b, d)),
            _const_spec((1, d)),
            _const_spec((1, d)),
        ],
        out_specs=pl.BlockSpec((None, ts, d), lambda b, t: (b, t, 0)),
        scratch_shapes=[
            pltpu.VMEM((HALO + ts, d_a), F32),
            pltpu.VMEM((HALO + ts, d_b), F32),
            pltpu.VMEM((ts, d_b), F32),
            pltpu.VMEM((ts, d_a + d_b), BF16),
            pltpu.VMEM((SUBLANES, HALO + ts, d_a), F32),
            pltpu.VMEM((len(_B_SHIFTS), HALO + ts, d_b), F32),
        ],
        compiler_params=pltpu.CompilerParams(
            dimension_semantics=("arbitrary", "arbitrary"), vmem_limit_bytes=VMEM_LIMIT),
        name="ab_mixer",
    )(x, w_in.astype(BF16), row(b_in), a_dw, row(a_dw_b), row(a_ln_g), row(a_ln_b),
      b_dw, w_out.astype(BF16), row(ln_g), row(ln_b))


def _cd_mixer_kernel(x_ref, win_ref, bin_ref, clng_ref, clnb_ref, wsc_ref, wsb_ref,
                     bias_ref, wout_ref, lng_ref, lnb_ref, wr_ref, bt_ref,
                     o_ref, rt_ref, rn_ref, cnt_ref,
                     qbuf, kbuf, vbuf, wsm, vbd, mbuf, after_ref, below_ref, *, d_c, d_d):
    ts = x_ref.shape[0]
    first_in_seq = pl.program_id(1) == 0
    hc = d_c // HEADS_C
    hd = d_d // HEADS_D

    @pl.when(first_in_seq)
    def _():
        kbuf[0:HIST, :] = jnp.zeros((HIST, d_d), BF16)
        vbuf[0:HIST, :] = jnp.zeros((HIST, d_d), BF16)
        r = lax.broadcasted_iota(I32, (SGU_BLOCK, HEADS_C * SGU_BLOCK), 0)
        c = lax.broadcasted_iota(I32, (SGU_BLOCK, HEADS_C * SGU_BLOCK), 1) % SGU_BLOCK
        wsm[...] = jnp.where(c // CHUNK <= r // CHUNK, wsc_ref[...], 0.0).astype(BF16)

    x = x_ref[...]
    xb = x.astype(BF16)
    h_att = (jnp.dot(xb, win_ref[:, 2 * d_c:], preferred_element_type=F32) + bin_ref[:, 2 * d_c:])
    qbuf[...] = (h_att[:, 0:d_d] * (hd ** -0.5 * LOG2_E)).astype(BF16)
    kbuf[HIST:HIST + ts, :] = h_att[:, d_d:2 * d_d].astype(BF16)
    vbuf[HIST:HIST + ts, :] = h_att[:, 2 * d_d:].astype(BF16)

    def attention(qb):
        q0 = qb * Q_ROWS
        col = lax.broadcasted_iota(I32, (1, K_ROWS), 1)
        no_key = jnp.where(jnp.logical_and(first_in_seq, col + q0 < HIST), NEG_INF, 0.0)
        for hh in range(HEADS_D):
            q = qbuf[q0:q0 + Q_ROWS, hh * hd:(hh + 1) * hd]
            k = kbuf[q0:q0 + K_ROWS, hh * hd:(hh + 1) * hd]
            v = vbuf[q0:q0 + K_ROWS, hh * hd:(hh + 1) * hd]
            s = lax.dot_general(q, k, (((1,), (1,)), ((), ())), preferred_element_type=F32)
            s = s + bias_ref[hh] + no_key
            p = jnp.exp2(s - jnp.max(s, axis=-1, keepdims=True))
            l = jnp.sum(p, axis=-1, keepdims=True)
            o = jnp.dot(p.astype(BF16), v, preferred_element_type=F32)
            mbuf[q0:q0 + Q_ROWS, d_c + hh * hd:d_c + (hh + 1) * hd] = (o / l).astype(BF16)

    for qb in range(ts // Q_ROWS):
        attention(qb)

    kbuf[0:HIST, :] = kbuf[ts:ts + HIST, :]
    vbuf[0:HIST, :] = vbuf[ts:ts + HIST, :]

    h_sgu = (jnp.dot(xb, win_ref[:, 0:2 * d_c], preferred_element_type=F32) + bin_ref[:, 0:2 * d_c])
    u = h_sgu[:, 0:d_c]
    vn = _layer_norm(h_sgu[:, d_c:], clng_ref[...], clnb_ref[...])
    lane_head = lax.broadcasted_iota(I32, (SGU_BLOCK, d_c), 1) // hc
    for nb in range(ts // SGU_BLOCK):
        v_blk = vn[nb * SGU_BLOCK:(nb + 1) * SGU_BLOCK, :]
        for hh in range(HEADS_C):
            vbd[hh * SGU_BLOCK:(hh + 1) * SGU_BLOCK, :] = jnp.where(
                lane_head == hh, v_blk, 0.0).astype(BF16)
        gate = jnp.dot(wsm[...], vbd[...], preferred_element_type=F32) + wsb_ref[...]
        mbuf[nb * SGU_BLOCK:(nb + 1) * SGU_BLOCK, 0:d_c] = (
            u[nb * SGU_BLOCK:(nb + 1) * SGU_BLOCK, :] * gate).astype(BF16)

    mix = jnp.dot(mbuf[...], wout_ref[...], preferred_element_type=F32)
    y = _layer_norm(ALPHA * x + mix, lng_ref[...], lnb_ref[...])
    o_ref[...] = y
    first_step = jnp.logical_and(pl.program_id(0) == 0, pl.program_id(1) == 0)
    _route_tile(y, first_step, wr_ref, bt_ref, rt_ref, rn_ref, cnt_ref, after_ref, below_ref)


def _attention_bias(d_rel_bias):
    heads = d_rel_bias.shape[0]
    i = jnp.arange(Q_ROWS)[:, None]
    j = jnp.arange(K_ROWS)[None, :]
    jb = j - (i // CHUNK) * CHUNK
    in_band = jnp.logical_and(jb >= 0, jb < (LEFT_CHUNKS + 1) * CHUNK)
    n_diag = Q_ROWS + K_ROWS - 1
    m = jnp.arange(n_diag)
    diag = d_rel_bias[:, jnp.clip(Q_ROWS - 1 + HIST - m, -MAX_REL, MAX_REL) + MAX_REL].astype(F32)
    period = jnp.pad(diag, ((0, 0), (0, 1)))
    flat = jnp.tile(period, (1, Q_ROWS))[:, :Q_ROWS * n_diag]
    rel = flat.reshape(heads, Q_ROWS, n_diag)[:, :, Q_ROWS - 1:Q_ROWS - 1 + K_ROWS]
    return jnp.where(in_band[None], rel * LOG2_E, NEG_INF)


def _cd_mixer(x, w_in, b_in, c_ln_g, c_ln_b, c_ws, c_ws_b, d_rel_bias, w_out, ln_g, ln_b, router):
    bt, s_len, d = x.shape
    d_c = c_ln_g.shape[0]
    d_d = w_out.shape[0] - d_c
    w_in_w = w_in.shape[1]
    ts = SEQ_TILE
    assert ts == HIST and ts % Q_ROWS == 0 and ts % SGU_BLOCK == 0
    row = lambda v: v.reshape(1, -1)
    ws_cat = jnp.transpose(c_ws, (1, 0, 2)).reshape(SGU_BLOCK, HEADS_C * SGU_BLOCK)
    bias_full = jnp.repeat(c_ws_b.T, d_c // HEADS_C, axis=1)
    att_bias = _attention_bias(d_rel_bias)
    kern = functools.partial(_cd_mixer_kernel, d_c=d_c, d_d=d_d)
    assert ts == ROUTER_TILE
    tiles_per_seq = s_len // ts
    wr, b_t = _router_operands(*router)
    r_shapes, r_specs, r_scratch = _router_out(bt * s_len, lambda b, t: b * tiles_per_seq + t)
    return pl.pallas_call(
        kern,
        out_shape=(jax.ShapeDtypeStruct((bt, s_len, d), F32),) + r_shapes,
        grid=(bt, s_len // ts),
        in_specs=[
            pl.BlockSpec((None, ts, d), lambda b, t: (b, t, 0)),
            _const_spec((d, w_in_w)),
            _const_spec((1, w_in_w)),
            _const_spec((1, d_c)),
            _const_spec((1, d_c)),
            _const_spec((SGU_BLOCK, HEADS_C * SGU_BLOCK)),
            _const_spec((SGU_BLOCK, d_c)),
            _const_spec((HEADS_D, Q_ROWS, K_ROWS)),
            _const_spec((d_c + d_d, d)),
            _const_spec((1, d)),
            _const_spec((1, d)),
            _const_spec((d, LANES)),
            _const_spec((ROUTER_ROWS, ts)),
        ],
        out_specs=(pl.BlockSpec((None, ts, d), lambda b, t: (b, t, 0)),) + r_specs,
        scratch_shapes=[
            pltpu.VMEM((ts, d_d), BF16),
            pltpu.VMEM((HIST + ts, d_d), BF16),
            pltpu.VMEM((HIST + ts, d_d), BF16),
            pltpu.VMEM((SGU_BLOCK, HEADS_C * SGU_BLOCK), BF16),
            pltpu.VMEM((HEADS_C * SGU_BLOCK, d_c), BF16),
            pltpu.VMEM((ts, d_c + d_d), BF16),
        ] + r_scratch,
        compiler_params=pltpu.CompilerParams(
            dimension_semantics=("arbitrary", "arbitrary"), vmem_limit_bytes=VMEM_LIMIT),
        name="cd_mixer",
    )(x, w_in.astype(BF16), row(b_in), row(c_ln_g), row(c_ln_b), ws_cat, bias_full,
      att_bias, w_out.astype(BF16), row(ln_g), row(ln_b), wr, b_t)


def _router_kernel(x_ref, wr_ref, bt_ref, rt_ref, rn_ref, cnt_ref, after_ref, below_ref):
    _route_tile(x_ref[...], pl.program_id(0) == 0, wr_ref, bt_ref, rt_ref, rn_ref, cnt_ref,
                after_ref, below_ref)


def _route_tile(x, first_step, wr_ref, bt_ref, rt_ref, rn_ref, cnt_ref, after_ref, below_ref):
    tr = x.shape[0]
    nr = ROUTER_ROWS

    @pl.when(first_step)
    def _():
        rr = lax.broadcasted_iota(I32, (tr, tr), 0)
        cc = lax.broadcasted_iota(I32, (tr, tr), 1)
        after_ref[...] = jnp.where(rr < cc, 1.0, 0.0).astype(BF16)
        lr = lax.broadcasted_iota(I32, (LANES, LANES), 0)
        lc = lax.broadcasted_iota(I32, (LANES, LANES), 1)
        below_ref[...] = jnp.where(lc < lr, 1.0, 0.0).astype(BF16)

    x_hi = x.astype(BF16)
    x_lo = (x - x_hi.astype(F32)).astype(BF16)
    p_hi = jnp.dot(x_hi, wr_ref[...], preferred_element_type=F32)
    p_lo = jnp.dot(x_lo, wr_ref[...], preferred_element_type=F32)
    logits = (p_hi + pltpu.roll(p_hi, LANES // 2, axis=1) + p_lo).T[0:nr] + bt_ref[...]
    row = lax.broadcasted_iota(I32, (nr, tr), 0).astype(F32)

    def top1(vals):
        m = jnp.max(vals, axis=0, keepdims=True)
        idx = jnp.min(jnp.where(vals == m, row, float(nr)), axis=0, keepdims=True)
        return m, idx

    g_mask = row < N_GROUPS
    g_max, g_idx = top1(jnp.where(g_mask, logits, NEG_INF))
    g_top = 1.0 / jnp.sum(jnp.where(g_mask, jnp.exp(logits - g_max), 0.0), axis=0, keepdims=True)

    e_lo = ROUTER_LANE0 + EXPERTS_PER_GROUP * g_idx
    e_vals = jnp.where(jnp.logical_and(row >= e_lo, row < e_lo + EXPERTS_PER_GROUP), logits, NEG_INF)
    m1, i1 = top1(e_vals)
    m2, i2 = top1(jnp.where(row == i1, NEG_INF, e_vals))
    ratio = jnp.exp(m2 - m1)
    w1 = g_top / (1.0 + ratio)
    w2 = g_top * ratio / (1.0 + ratio)

    chosen = jnp.logical_or(row == i1, row == i2)
    onehot = jnp.where(chosen, 1.0, 0.0)
    seen = jnp.dot(onehot.astype(BF16), after_ref[...], preferred_element_type=F32)
    counts = jnp.broadcast_to(jnp.sum(onehot, axis=1, keepdims=True), (nr, LANES))
    cnt_ref[...] = counts

    groups = ((counts.astype(I32) + (SEG_ALIGN - 1)) // SEG_ALIGN).astype(F32).astype(BF16)
    groups = jnp.concatenate([groups, jnp.zeros((LANES - nr, LANES), BF16)], axis=0)
    seg_start = SEG_ALIGN * jnp.dot(below_ref[...], groups, preferred_element_type=F32)[0:nr]
    where_row = seen + jnp.concatenate([seg_start] * (tr // LANES), axis=1)
    pos1 = jnp.sum(jnp.where(row == i1, where_row, 0.0), axis=0, keepdims=True)
    pos2 = jnp.sum(jnp.where(row == i2, where_row, 0.0), axis=0, keepdims=True)

    fields = [i1 - ROUTER_LANE0, i2 - ROUTER_LANE0, w1, w2, pos1, pos2]
    res = jnp.concatenate(fields + [jnp.zeros((SUBLANES - len(fields), tr), F32)], axis=0)
    rt_ref[...] = res
    rn_ref[...] = jnp.concatenate([res, jnp.zeros((LANES - SUBLANES, tr), F32)], axis=0).T


def _router_out(n_tok, index):
    tr = ROUTER_TILE
    shapes = (jax.ShapeDtypeStruct((SUBLANES, n_tok), F32),
              jax.ShapeDtypeStruct((n_tok, LANES), F32),
              jax.ShapeDtypeStruct((n_tok // tr, ROUTER_ROWS, LANES), F32))
    specs = (pl.BlockSpec((SUBLANES, tr), lambda *g: (0, index(*g))),
             pl.BlockSpec((tr, LANES), lambda *g: (index(*g), 0)),
             pl.BlockSpec((None, ROUTER_ROWS, LANES), lambda *g: (index(*g), 0, 0)))
    scratch = [pltpu.VMEM((tr, tr), BF16), pltpu.VMEM((LANES, LANES), BF16)]
    return shapes, specs, scratch


def _router_operands(rg_w, rg_b, re_w, re_b):
    d = rg_w.shape[0]
    tr = ROUTER_TILE
    w_all = jnp.concatenate([rg_w, jnp.transpose(re_w, (1, 0, 2)).reshape(d, N_EXPERTS)], axis=1)
    b_all = jnp.concatenate([rg_b, re_b.reshape(N_EXPERTS)])
    half = LANES // 2
    pad = half - w_all.shape[1]
    w_hi = w_all.astype(BF16)
    w_lo = (w_all - w_hi.astype(F32)).astype(BF16)
    w_all = jnp.concatenate([jnp.pad(w_hi, ((0, 0), (0, pad))), jnp.pad(w_lo, ((0, 0), (0, pad)))], axis=1)
    b_t = jnp.broadcast_to(jnp.pad(b_all, (0, ROUTER_ROWS - b_all.shape[0]))[:, None], (ROUTER_ROWS, tr))
    return w_all, b_t


def _router(xt, rg_w, rg_b, re_w, re_b):
    n_tok, d = xt.shape
    tr = ROUTER_TILE
    w_all, b_t = _router_operands(rg_w, rg_b, re_w, re_b)
    shapes, specs, scratch = _router_out(n_tok, lambda i: i)
    return pl.pallas_call(
        _router_kernel,
        out_shape=shapes,
        grid=(n_tok // tr,),
        in_specs=[
            pl.BlockSpec((tr, d), lambda i: (i, 0)),
            _const_spec((d, LANES)),
            _const_spec((ROUTER_ROWS, tr)),
        ],
        out_specs=specs,
        scratch_shapes=scratch,
        compiler_params=pltpu.CompilerParams(
            dimension_semantics=("arbitrary",), vmem_limit_bytes=VMEM_LIMIT),
        name="router",
    )(xt, w_all, b_t)


def _round_up(v, m):
    return ((v + m - 1) // m) * m


def _num_expert_blocks(n_tok):
    n_tiles = n_tok // MOE_TILE
    rows = n_tok * TOP_K + n_tiles * N_EXPERTS * (SEG_ALIGN - 1)
    return -(-rows // EXPERT_ROWS) + N_EXPERTS


def _routing_tables(cnt, n_tok):
    counts = cnt[:, ROUTER_LANE0:ROUTER_LANE0 + N_EXPERTS, 0].astype(I32)
    seg = _round_up(counts, SEG_ALIGN)
    tile_off = jnp.cumsum(seg, axis=1) - seg
    tot = jnp.sum(seg, axis=0)
    reg = _round_up(tot, EXPERT_ROWS)
    reg_end = jnp.cumsum(reg)
    reg_start = reg_end - reg
    glob_off = reg_start[None, :] + jnp.cumsum(seg, axis=0) - seg
    n_blocks = _num_expert_blocks(n_tok)
    blk_start = jnp.arange(n_blocks, dtype=I32) * EXPERT_ROWS
    block_e = jnp.sum((reg_end[None, :] <= blk_start[:, None]).astype(I32), axis=1)
    block_e = jnp.minimum(block_e, N_EXPERTS - 1).astype(I32)
    n_used = (reg_end[-1] // EXPERT_ROWS).astype(I32).reshape(1)
    groups = lambda a: (a.reshape(-1) // SEG_ALIGN).astype(I32)
    big = (jnp.max(seg, axis=1) >= INLINE_GROUPS * SEG_ALIGN).astype(I32)
    ar = jnp.arange(N_EXPERTS, dtype=I32)
    later = jnp.where(jnp.logical_and(ar[None, :] > ar[:, None], reg[None, :] > 0), ar[None, :], N_EXPERTS)
    next_of = jnp.min(later, axis=1)
    next_of = jnp.where(next_of < N_EXPERTS, next_of, ar)
    next_e = jnp.sum(jnp.where(block_e[:, None] == ar[None, :], next_of[None, :], 0), axis=1).astype(I32)
    return dict(tile_off=groups(tile_off), glob_off=groups(glob_off), seg=groups(seg), big=big,
                tail_start=groups(reg_start + tot), tail_len=groups(reg - tot),
                block_e=block_e, n_used=n_used, next_e=next_e)


def _grouped(rows):
    return rows.reshape(rows.shape[0] // SEG_ALIGN, SEG_ALIGN, rows.shape[1])


def _ungrouped(groups):
    return groups.reshape(groups.shape[0] * SEG_ALIGN, groups.shape[2])


def _for_each_piece(length, max_len, fn):
    off = 0
    for size in reversed([1 << b for b in range(max_len.bit_length())]):
        take = (length & size) != 0

        @pl.when(take)
        def _(off=off, size=size):
            fn(off, size)

        off = off + jnp.where(take, size, 0)


def _start_segment_copies(copy, toff_ref, goff_ref, seg_ref, big_ref, t, live, inline):
    def per_expert(e, limit, enabled):
        idx = t * N_EXPERTS + e
        t0 = toff_ref[idx]
        g0 = goff_ref[idx]
        _for_each_piece(jnp.where(enabled, seg_ref[idx], 0), limit,
                        lambda off, size: copy(t0 + off, g0 + off, size))

    def looped(enabled):
        def body(e, c):
            per_expert(e, MOE_TILE // SEG_ALIGN, enabled)
            return c

        lax.fori_loop(0, N_EXPERTS, body, 0)

    if not inline:
        looped(live)
        return
    big = big_ref[t] != 0
    for e in range(N_EXPERTS):
        per_expert(e, INLINE_GROUPS - 1, jnp.logical_and(live, jnp.logical_not(big)))

    @pl.when(jnp.logical_and(live, big))
    def _():
        looped(True)


def _wait_segment_copies(wait, toff_ref, seg_ref, t):
    last = t * N_EXPERTS + N_EXPERTS - 1
    _for_each_piece(toff_ref[last] + seg_ref[last], SORT_ROWS // SEG_ALIGN, lambda off, size: wait(size))


def _dispatch_kernel(toff_ref, goff_ref, seg_ref, big_ref, tstart_ref, tlen_ref, nu_ref,
                     x_ref, rt_ref, xp_hbm, obuf, zbuf, sem, zsem):
    i = pl.program_id(0)
    n = pl.num_programs(0)
    tile = x_ref.shape[0]
    slot = i % 2

    def wait_segments(t, sl):
        _wait_segment_copies(
            lambda size: pltpu.make_async_copy(obuf.at[sl, pl.ds(0, size)], xp_hbm.at[pl.ds(0, size)],
                                               sem.at[sl]).wait(),
            toff_ref, seg_ref, t)

    def start_segments(t, sl, live, inline):
        _start_segment_copies(
            lambda tg, sg, size: pltpu.make_async_copy(
                obuf.at[sl, pl.ds(tg, size)], xp_hbm.at[pl.ds(sg, size)], sem.at[sl]).start(),
            toff_ref, goff_ref, seg_ref, big_ref, t, live, inline)

    @pl.when(i >= 2)
    def _():
        wait_segments(i - 2, slot)

    start_segments(jnp.maximum(i - 1, 0), 1 - slot, i >= 1, inline=True)

    pos1 = rt_ref[4:5, :].astype(I32)
    pos2 = rt_ref[5:6, :].astype(I32)
    xb = x_ref[...].astype(BF16)
    for c in range(SORT_ROWS // SORT_CHUNK):
        rows = lax.broadcasted_iota(I32, (SORT_CHUNK, tile), 0) + c * SORT_CHUNK
        hit = jnp.logical_or(rows == pos1, rows == pos2)
        g0, g1 = c * SORT_CHUNK // SEG_ALIGN, (c + 1) * SORT_CHUNK // SEG_ALIGN
        obuf[slot, g0:g1] = _grouped(jnp.dot(
            jnp.where(hit, 1.0, 0.0).astype(BF16), xb, preferred_element_type=F32).astype(BF16))

    def zero_fill(wait):
        def per_expert(e, c):
            def piece(off, size):
                cp = pltpu.make_async_copy(zbuf.at[pl.ds(0, size)],
                                           xp_hbm.at[pl.ds(tstart_ref[e] + off, size)], zsem)
                cp.wait() if wait else cp.start()

            _for_each_piece(tlen_ref[e], EXPERT_ROWS // SEG_ALIGN - 1, piece)
            return c

        lax.fori_loop(0, N_EXPERTS, per_expert, 0)

        zg = zbuf.shape[0]

        def per_zero_copy(h, c):
            cp = pltpu.make_async_copy(zbuf, xp_hbm.at[pl.ds(h * zg, zg)], zsem)
            cp.wait() if wait else cp.start()
            return c

        per_block = EXPERT_ROWS // SEG_ALIGN // zg
        lax.fori_loop(nu_ref[0] * per_block, xp_hbm.shape[0] // zg, per_zero_copy, 0)

    @pl.when(i == 0)
    def _():
        zbuf[...] = jnp.zeros(zbuf.shape, zbuf.dtype)
        zero_fill(False)

    @pl.when(i == n - 1)
    def _():
        start_segments(i, slot, True, inline=False)

        @pl.when(i >= 1)
        def _():
            wait_segments(i - 1, 1 - slot)

        wait_segments(i, slot)
        zero_fill(True)


def _dispatch(xt, rt, tabs):
    n_tok, d = xt.shape
    tile = MOE_TILE
    n_groups = _num_expert_blocks(n_tok) * EXPERT_ROWS // SEG_ALIGN
    return pl.pallas_call(
        _dispatch_kernel,
        out_shape=jax.ShapeDtypeStruct((n_groups, SEG_ALIGN, d), BF16),
        grid_spec=pltpu.PrefetchScalarGridSpec(
            num_scalar_prefetch=7,
            grid=(n_tok // tile,),
            in_specs=[
                pl.BlockSpec((tile, d), lambda i, *_: (i, 0)),
                pl.BlockSpec((SUBLANES, tile), lambda i, *_: (0, i)),
            ],
            out_specs=pl.BlockSpec(memory_space=pl.ANY),
            scratch_shapes=[
                pltpu.VMEM((2, SORT_ROWS // SEG_ALIGN, SEG_ALIGN, d), BF16),
                pltpu.VMEM((EXPERT_ROWS // SEG_ALIGN // 2, SEG_ALIGN, d), BF16),
                pltpu.SemaphoreType.DMA((2,)),
                pltpu.SemaphoreType.DMA,
            ],
        ),
        compiler_params=pltpu.CompilerParams(
            dimension_semantics=("arbitrary",), vmem_limit_bytes=VMEM_LIMIT),
        name="dispatch",
    )(tabs["tile_off"], tabs["glob_off"], tabs["seg"], tabs["big"], tabs["tail_start"],
      tabs["tail_len"], tabs["n_used"], xt, rt)


def _ffn_kernel(be_ref, nu_ref, nxt_ref, x_ref, wg_hbm, wu_hbm, wd_hbm, o_ref,
                wg_f, wu_f, wd_f, wg_b, wu_b, wd_b, sem, *, layer):
    i = pl.program_id(0)
    used = i < nu_ref[0]
    e = be_ref[i]

    def weight_copies(expert):
        return [pltpu.make_async_copy(src.at[layer, expert], dst, sem.at[k])
                for k, (src, dst) in enumerate(((wg_hbm, wg_f), (wu_hbm, wu_f), (wd_hbm, wd_f)))]

    @pl.when(i == 0)
    def _():
        for cp in weight_copies(e):
            cp.start()

    @pl.when(jnp.logical_and(used, jnp.logical_or(i == 0, e != be_ref[jnp.maximum(i - 1, 0)])))
    def _():
        for cp in weight_copies(e):
            cp.wait()
        wg_b[...] = wg_f[...].astype(BF16)
        wu_b[...] = wu_f[...].astype(BF16)
        wd_b[...] = wd_f[...].astype(BF16)

        @pl.when(nxt_ref[i] != e)
        def _():
            for cp in weight_copies(nxt_ref[i]):
                cp.start()

    @pl.when(used)
    def _():
        xb = _ungrouped(x_ref[...])
        g = jnp.dot(xb, wg_b[...], preferred_element_type=F32)
        u = jnp.dot(xb, wu_b[...], preferred_element_type=F32)
        hb = (g * jax.nn.sigmoid(g) * u).astype(BF16)
        o_ref[...] = _grouped(jnp.dot(hb, wd_b[...], preferred_element_type=F32).astype(BF16))

    @pl.when(jnp.logical_not(used))
    def _():
        o_ref[...] = jnp.zeros(o_ref.shape, o_ref.dtype)


def _expert_ffn(xp, tabs, w_gate, w_up, w_down, layer):
    n_groups, _, d = xp.shape
    d_e = w_gate.shape[3]
    blk_groups = EXPERT_ROWS // SEG_ALIGN
    n_blocks = n_groups // blk_groups
    blk = lambda i, be, nu, nxt: (jnp.minimum(i, nu[0] - 1), 0, 0)
    return pl.pallas_call(
        functools.partial(_ffn_kernel, layer=layer),
        out_shape=jax.ShapeDtypeStruct((n_groups, SEG_ALIGN, d), BF16),
        grid_spec=pltpu.PrefetchScalarGridSpec(
            num_scalar_prefetch=3,
            grid=(n_blocks,),
            in_specs=[
                pl.BlockSpec((blk_groups, SEG_ALIGN, d), blk),
                pl.BlockSpec(memory_space=pl.ANY),
                pl.BlockSpec(memory_space=pl.ANY),
                pl.BlockSpec(memory_space=pl.ANY),
            ],
            out_specs=pl.BlockSpec((blk_groups, SEG_ALIGN, d), lambda i, be, nu, nxt: (i, 0, 0)),
            scratch_shapes=[
                pltpu.VMEM((d, d_e), F32),
                pltpu.VMEM((d, d_e), F32),
                pltpu.VMEM((d_e, d), F32),
                pltpu.VMEM((d, d_e), BF16),
                pltpu.VMEM((d, d_e), BF16),
                pltpu.VMEM((d_e, d), BF16),
                pltpu.SemaphoreType.DMA((3,)),
            ],
        ),
        compiler_params=pltpu.CompilerParams(
            dimension_semantics=("arbitrary",), vmem_limit_bytes=VMEM_LIMIT),
        name="expert_ffn",
    )(tabs["block_e"], tabs["n_used"], tabs["next_e"], xp, w_gate, w_up, w_down)


def _combine_kernel(toff_ref, goff_ref, seg_ref, big_ref, x_ref, rn_ref, yp_hbm, lng_ref, lnb_ref,
                    o_ref, ybuf, sem):
    i = pl.program_id(0)
    n = pl.num_programs(0)
    tile = x_ref.shape[0]
    slot = i % 2

    def wait_segments(t, sl):
        _wait_segment_copies(
            lambda size: pltpu.make_async_copy(yp_hbm.at[pl.ds(0, size)], ybuf.at[sl, pl.ds(0, size)],
                                               sem.at[sl]).wait(),
            toff_ref, seg_ref, t)

    def start_segments(t, sl, live, inline):
        _start_segment_copies(
            lambda tg, sg, size: pltpu.make_async_copy(
                yp_hbm.at[pl.ds(sg, size)], ybuf.at[sl, pl.ds(tg, size)], sem.at[sl]).start(),
            toff_ref, goff_ref, seg_ref, big_ref, t, live, inline)

    @pl.when(i == 0)
    def _():
        ybuf[...] = jnp.zeros(ybuf.shape, ybuf.dtype)
        start_segments(0, 0, True, inline=False)

    wait_segments(i, slot)
    start_segments(jnp.minimum(i + 1, n - 1), 1 - slot, i + 1 < n, inline=True)

    rn = rn_ref[...]
    full = lambda col: jnp.broadcast_to(col, (tile, LANES))
    w1, w2 = full(rn[:, 2:3]), full(rn[:, 3:4])
    pos1, pos2 = full(rn[:, 4:5]), full(rn[:, 5:6])
    lane = lax.broadcasted_iota(I32, (tile, LANES), 1).astype(F32)

    def pick_lanes(col0):
        cols = lane + float(col0)
        return jnp.where(cols == pos1, w1, 0.0) + jnp.where(cols == pos2, w2, 0.0)

    ffn = jnp.zeros(o_ref.shape, F32)
    for c in range(SORT_ROWS // SORT_CHUNK):
        pick = jnp.concatenate([pick_lanes(c * SORT_CHUNK + l0) for l0 in range(0, SORT_CHUNK, LANES)],
                               axis=1).astype(BF16)
        g0, g1 = c * SORT_CHUNK // SEG_ALIGN, (c + 1) * SORT_CHUNK // SEG_ALIGN
        ffn = ffn + jnp.dot(pick, _ungrouped(ybuf[slot, g0:g1]), preferred_element_type=F32)
    o_ref[...] = _layer_norm(ALPHA * x_ref[...] + ffn, lng_ref[...], lnb_ref[...])


def _combine(xt, rn, yp, tabs, ln_g, ln_b):
    n_tok, d = xt.shape
    tile = MOE_TILE
    return pl.pallas_call(
        _combine_kernel,
        out_shape=jax.ShapeDtypeStruct((n_tok, d), F32),
        grid_spec=pltpu.PrefetchScalarGridSpec(
            num_scalar_prefetch=4,
            grid=(n_tok // tile,),
            in_specs=[
                pl.BlockSpec((tile, d), lambda i, *_: (i, 0)),
                pl.BlockSpec((tile, LANES), lambda i, *_: (i, 0)),
                pl.BlockSpec(memory_space=pl.ANY),
                pl.BlockSpec((1, d), lambda i, *_: (0, 0)),
                pl.BlockSpec((1, d), lambda i, *_: (0, 0)),
            ],
            out_specs=pl.BlockSpec((tile, d), lambda i, *_: (i, 0)),
            scratch_shapes=[
                pltpu.VMEM((2, SORT_ROWS // SEG_ALIGN, SEG_ALIGN, d), BF16),
                pltpu.SemaphoreType.DMA((2,)),
            ],
        ),
        compiler_params=pltpu.CompilerParams(
            dimension_semantics=("arbitrary",), vmem_limit_bytes=VMEM_LIMIT),
        name="combine",
    )(tabs["tile_off"], tabs["glob_off"], tabs["seg"], tabs["big"], xt, rn, yp,
      ln_g.reshape(1, d), ln_b.reshape(1, d))


def _moe_layer(x, router, w_gate, w_up, w_down, layer, ln_g, ln_b, routing=None):
    bt, s_len, d = x.shape
    n_tok = bt * s_len
    xt = x.reshape(n_tok, d)
    rt, rn, cnt = routing if routing is not None else _router(xt, *router)
    tabs = _routing_tables(cnt, n_tok)
    xp = _dispatch(xt, rt, tabs)
    yp = _expert_ffn(xp, tabs, w_gate, w_up, w_down, layer)
    out = _combine(xt, rn, yp, tabs, ln_g, ln_b)
    return out.reshape(bt, s_len, d)


def kernel(x, ab_w_in, ab_b_in, a_dw, a_dw_b, a_ln_g, a_ln_b, b_dw, ab_w_out, cd_w_in, cd_b_in, c_ln_g, c_ln_b, c_ws, c_ws_b, d_rel_bias, cd_w_out, mix_ln_g, mix_ln_b, moe_rg_w, moe_rg_b, moe_re_w, moe_re_b, moe_w_gate, moe_w_up, moe_w_down, ffn_ln_g, ffn_ln_b):
    for layer in range(DEPTH):
        i = layer // 2
        router = (moe_rg_w[layer], moe_rg_b[layer], moe_re_w[layer], moe_re_b[layer])
        routing = None
        if layer % 2 == 0:
            x = _ab_mixer(x, ab_w_in[i], ab_b_in[i], a_dw[i], a_dw_b[i], a_ln_g[i], a_ln_b[i],
                          b_dw[i], ab_w_out[i], mix_ln_g[layer], mix_ln_b[layer])
        else:
            x, *routing = _cd_mixer(x, cd_w_in[i], cd_b_in[i], c_ln_g[i], c_ln_b[i], c_ws[i], c_ws_b[i],
                                    d_rel_bias[i], cd_w_out[i], mix_ln_g[layer], mix_ln_b[layer], router)
        x = _moe_layer(x, router, moe_w_gate, moe_w_up, moe_w_down, layer,
                       ffn_ln_g[layer], ffn_ln_b[layer], routing)
    return x
```
